```python
import math
import jax
import jax.numpy as jnp
from jax import lax
import numpy as np

D_MODEL = 1024
BATCH = 2
SEQ = 16384
DEPTH = 2

N_MIXERS = 2
HEAD_DIM = 64
N_HEADS = 12
N_KV_HEADS_B = 2
N_MEM_HEADS = 4
MEM_LEN = 256
ATTN_WIDTH = N_HEADS * HEAD_DIM
MEM_WIDTH = N_MEM_HEADS * HEAD_DIM
BRANCH_WIDTH = ATTN_WIDTH + MEM_WIDTH
KV_WIDTH_B = N_KV_HEADS_B * HEAD_DIM
IN_WIDTH_A = 3 * ATTN_WIDTH + MEM_WIDTH + BRANCH_WIDTH
IN_WIDTH_B = ATTN_WIDTH + 2 * KV_WIDTH_B + MEM_WIDTH + BRANCH_WIDTH
MOBA_BLOCK = 256
MOBA_TOPK = 3
MOBA_QCHUNK = 32
WINDOW = 128
N_LAYERS_A = (DEPTH + 1) // 2
N_LAYERS_B = DEPTH // 2
RMS_EPS = 1e-6

kernel_name = "moba_swa_sink_hybrid_trunk"


def rms_norm(x, g):
    xf = x.astype(jnp.float32)
    y = xf * lax.rsqrt(jnp.mean(xf * xf, axis=-1, keepdims=True) + RMS_EPS)
    return (y * g.astype(jnp.float32)).astype(x.dtype)


def alibi_slopes(n_heads):
    def pow2_slopes(n):
        start = 2.0 ** (-8.0 / n)
        return [start ** (i + 1) for i in range(n)]
    if math.log2(n_heads).is_integer():
        vals = pow2_slopes(n_heads)
    else:
        c = 2 ** math.floor(math.log2(n_heads))
        vals = pow2_slopes(c) + pow2_slopes(2 * c)[0::2][: n_heads - c]
    return jnp.asarray(np.array(vals, dtype=np.float32))


def moba_attention(q, k, v, slopes):
    b, s, h, dh = q.shape
    nblk = -(-s // MOBA_BLOCK)
    sp = nblk * MOBA_BLOCK
    topk = min(MOBA_TOPK, nblk)
    pad = ((0, 0), (0, sp - s), (0, 0), (0, 0))
    q = jnp.pad(q, pad).transpose(0, 2, 1, 3)
    kb = jnp.pad(k, pad).transpose(0, 2, 1, 3).reshape(b, h, nblk, MOBA_BLOCK, dh)
    vb = jnp.pad(v, pad).transpose(0, 2, 1, 3).reshape(b, h, nblk, MOBA_BLOCK, dh)
    kmean = jnp.mean(kb.astype(jnp.float32), axis=3).astype(kb.dtype)
    scale = dh ** -0.5
    bi = jnp.arange(b)[:, None, None, None]
    hi = jnp.arange(h)[None, :, None, None]
    blk_ids = jnp.arange(nblk, dtype=jnp.int32)
    offs = jnp.arange(MOBA_BLOCK, dtype=jnp.int32)
    qoffs = jnp.arange(MOBA_QCHUNK, dtype=jnp.int32)
    sl5 = slopes[None, :, None, None, None]
    sl4 = slopes[None, :, None, None]

    def chunk(c):
        start = c * MOBA_QCHUNK
        j = start // MOBA_BLOCK
        qc = lax.dynamic_slice_in_dim(q, start, MOBA_QCHUNK, axis=2)
        tpos = start + qoffs
        gate = jnp.einsum('bhqd,bhnd->bhqn', qc, kmean).astype(jnp.float32)
        gate = jnp.where(blk_ids < j, gate, -jnp.inf)
        _, idx = lax.top_k(gate, topk)
        valid = idx < j
        ksel = kb[bi, hi, idx]
        vsel = vb[bi, hi, idx]
        s_sel = jnp.einsum('bhqd,bhqkld->bhqkl', qc, ksel).astype(jnp.float32) * scale
        spos = idx[..., None] * MOBA_BLOCK + offs
        dist_sel = (tpos[None, None, :, None, None] - spos).astype(jnp.float32)
        s_sel = s_sel - sl5 * dist_sel
        s_sel = jnp.where(valid[..., None], s_sel, -jnp.inf).reshape(b, h, MOBA_QCHUNK, topk * MOBA_BLOCK)
        kown = lax.dynamic_index_in_dim(kb, j, axis=2, keepdims=False)
        vown = lax.dynamic_index_in_dim(vb, j, axis=2, keepdims=False)
        dist_own = tpos[:, None] - (j * MOBA_BLOCK + offs)[None, :]
        s_own = jnp.einsum('bhqd,bhld->bhql', qc, kown).astype(jnp.float32) * scale
        s_own = s_own - sl4 * dist_own.astype(jnp.float32)
        s_own = jnp.where(dist_own >= 0, s_own, -jnp.inf)
        p = jax.nn.softmax(jnp.concatenate([s_sel, s_own], axis=-1), axis=-1).astype(vb.dtype)
        p_sel = p[..., : topk * MOBA_BLOCK].reshape(b, h, MOBA_QCHUNK, topk, MOBA_BLOCK)
        p_own = p[..., topk * MOBA_BLOCK:]
        return (jnp.einsum('bhqkl,bhqkld->bhqd', p_sel, vsel)
                + jnp.einsum('bhql,bhld->bhqd', p_own, vown))

    n_chunks = sp // MOBA_QCHUNK
    out = lax.map(chunk, jnp.arange(n_chunks, dtype=jnp.int32))
    out = out.transpose(1, 0, 3, 2, 4).reshape(b, sp, h * dh)
    return out[:, :s]


def swa_sink_attention(q, k, v, sinks, slopes):
    b, s, hq, dh = q.shape
    hkv = k.shape[2]
    g = hq // hkv
    nb = s // WINDOW
    qb = q.reshape(b, nb, WINDOW, hkv, g, dh)
    kb = k.reshape(b, nb, WINDOW, hkv, dh)
    vb = v.reshape(b, nb, WINDOW, hkv, dh)
    padb = ((0, 0), (1, 0), (0, 0), (0, 0), (0, 0))
    kband = jnp.concatenate([jnp.pad(kb, padb)[:, :-1], kb], axis=2)
    vband = jnp.concatenate([jnp.pad(vb, padb)[:, :-1], vb], axis=2)
    logits = jnp.einsum('bnqkgd,bnlkd->bkgnql', qb, kband).astype(jnp.float32) * (dh ** -0.5)
    qi = jnp.arange(WINDOW, dtype=jnp.int32)
    li = jnp.arange(2 * WINDOW, dtype=jnp.int32)
    dist = qi[:, None] + WINDOW - li[None, :]
    kpos = jnp.arange(nb, dtype=jnp.int32)[:, None] * WINDOW - WINDOW + li[None, :]
    mask = ((dist >= 0) & (dist < WINDOW))[None] & (kpos >= 0)[:, None, :]
    sl = slopes.reshape(hkv, g)[None, :, :, None, None, None]
    logits = logits - sl * dist.astype(jnp.float32)
    logits = jnp.where(mask, logits, -jnp.inf)
    sink = sinks.astype(jnp.float32).reshape(hkv, g)[None, :, :, None, None, None]
    m = jnp.maximum(jnp.max(logits, axis=-1, keepdims=True), sink)
    e = jnp.exp(logits - m)
    p = e / (jnp.sum(e, axis=-1, keepdims=True) + jnp.exp(sink - m))
    out = jnp.einsum('bkgnql,bnlkd->bnqkgd', p.astype(vband.dtype), vband)
    return out.reshape(b, s, hq * dh)


def memory_attention(qm, mem_k, mem_v):
    b, s, hm, dh = qm.shape
    logits = jnp.einsum('bshd,bmhd->bhsm', qm, mem_k).astype(jnp.float32) * (dh ** -0.5)
    p = jax.nn.softmax(logits, axis=-1).astype(mem_v.dtype)
    return jnp.einsum('bhsm,bmhd->bshd', p, mem_v).reshape(b, s, hm * dh)


def setup_inputs(seed: int = 0) -> dict:
    key = jax.random.key(seed)
    ks = jax.random.split(key, 10)
    f32 = jnp.float32
    x = jax.random.normal(ks[0], (BATCH, SEQ, D_MODEL), f32)
    mem = jax.random.normal(ks[1], (BATCH, MEM_LEN, D_MODEL), f32)
    norm_g = 1.0 + 0.05 * jax.random.normal(ks[2], (DEPTH, D_MODEL), f32)
    w_in_a = jax.random.normal(ks[3], (N_LAYERS_A, D_MODEL, IN_WIDTH_A), f32) * (D_MODEL ** -0.5)
    w_in_b = jax.random.normal(ks[4], (N_LAYERS_B, D_MODEL, IN_WIDTH_B), f32) * (D_MODEL ** -0.5)
    sinks_b = 0.5 * jax.random.normal(ks[5], (N_LAYERS_B, N_HEADS), f32)
    w_mem_kv = jax.random.normal(ks[6], (DEPTH, D_MODEL, 2 * MEM_WIDTH), f32) * (D_MODEL ** -0.5)
    w_out = jax.random.normal(ks[7], (DEPTH, BRANCH_WIDTH, D_MODEL), f32) * (BRANCH_WIDTH ** -0.5)
    mem_norm_g = 1.0 + 0.05 * jax.random.normal(ks[8], (D_MODEL,), f32)
    final_norm_g = 1.0 + 0.05 * jax.random.normal(ks[9], (D_MODEL,), f32)
    return {"x": x, "mem": mem, "norm_g": norm_g, "w_in_a": w_in_a, "w_in_b": w_in_b,
            "sinks_b": sinks_b, "w_mem_kv": w_mem_kv, "w_out": w_out,
            "mem_norm_g": mem_norm_g, "final_norm_g": final_norm_g}


def reference(x, mem, norm_g, w_in_a, w_in_b, sinks_b, w_mem_kv, w_out, mem_norm_g, final_norm_g):
    b, s, _ = x.shape
    slopes = alibi_slopes(N_HEADS)
    mem_n = rms_norm(mem, mem_norm_g)
    m_len = mem.shape[1]
    for i in range(DEPTH):
        h = rms_norm(x, norm_g[i])
        kvm = mem_n @ w_mem_kv[i]
        mem_k = kvm[..., :MEM_WIDTH].reshape(b, m_len, N_MEM_HEADS, HEAD_DIM)
        mem_v = kvm[..., MEM_WIDTH:].reshape(b, m_len, N_MEM_HEADS, HEAD_DIM)
        if i % N_MIXERS == 0:
            proj = h @ w_in_a[i // N_MIXERS]
            q, k, v, qm, z = jnp.split(
                proj, [ATTN_WIDTH, 2 * ATTN_WIDTH, 3 * ATTN_WIDTH, 3 * ATTN_WIDTH + MEM_WIDTH], axis=-1)
            y_self = moba_attention(q.reshape(b, s, N_HEADS, HEAD_DIM),
                                    k.reshape(b, s, N_HEADS, HEAD_DIM),
                                    v.reshape(b, s, N_HEADS, HEAD_DIM), slopes)
        else:
            proj = h @ w_in_b[i // N_MIXERS]
            q, k, v, qm, z = jnp.split(
                proj, [ATTN_WIDTH, ATTN_WIDTH + KV_WIDTH_B, ATTN_WIDTH + 2 * KV_WIDTH_B,
                       ATTN_WIDTH + 2 * KV_WIDTH_B + MEM_WIDTH], axis=-1)
            y_self = swa_sink_attention(q.reshape(b, s, N_HEADS, HEAD_DIM),
                                        k.reshape(b, s, N_KV_HEADS_B, HEAD_DIM),
                                        v.reshape(b, s, N_KV_HEADS_B, HEAD_DIM),
                                        sinks_b[i // N_MIXERS], slopes)
        y_mem = memory_attention(qm.reshape(b, s, N_MEM_HEADS, HEAD_DIM), mem_k, mem_v)
        y = jnp.concatenate([y_self, y_mem], axis=-1) * jax.nn.silu(z)
        x = x + y @ w_out[i]
    return rms_norm(x, final_norm_g)
```

```python
import functools
import math

import jax
import jax.numpy as jnp
import numpy as np
from jax import lax
from jax.experimental import pallas as pl
from jax.experimental.pallas import tpu as pltpu

HEAD_DIM = 64
N_HEADS = 12
N_KV_HEADS_B = 2
N_MEM_HEADS = 4
ATTN_WIDTH = N_HEADS * HEAD_DIM
MEM_WIDTH = N_MEM_HEADS * HEAD_DIM
BRANCH_WIDTH = ATTN_WIDTH + MEM_WIDTH
KV_WIDTH_B = N_KV_HEADS_B * HEAD_DIM
MOBA_BLOCK = 256
MOBA_TOPK = 3
WINDOW = 128
RMS_EPS = 1e-6

LANES = 128
HEAD_PAIR = LANES // HEAD_DIM
N_PAIRS = N_HEADS // HEAD_PAIR
SEL_COLS = 64
ALIBI_COLS = 3
AUG = 2 * LANES
MASKED = -(2.0 ** 30)
ROW_TILE = 512
SWA_TILE = 256
VMEM_LIMIT = 56 * 1024 * 1024

F32 = jnp.float32
BF16 = jnp.bfloat16
_NT = (((1,), (1,)), ((), ()))


def _alibi_slopes(n_heads):
    def pow2_slopes(n):
        start = 2.0 ** (-8.0 / n)
        return [start ** (i + 1) for i in range(n)]
    if math.log2(n_heads).is_integer():
        vals = pow2_slopes(n_heads)
    else:
        c = 2 ** math.floor(math.log2(n_heads))
        vals = pow2_slopes(c) + pow2_slopes(2 * c)[0::2][: n_heads - c]
    return np.array(vals, dtype=np.float32)


def _bf16_pieces(v):
    rest = np.asarray(v, np.float32)
    pieces = []
    for _ in range(ALIBI_COLS):
        p = rest.astype(BF16).astype(np.float32)
        pieces.append(p)
        rest = (rest - p).astype(np.float32)
    assert not rest.any()
    return pieces


def _rms_bf16(x, g):
    ms = jnp.mean(x * x, axis=-1, keepdims=True)
    return (x * lax.rsqrt(ms + RMS_EPS) * g).astype(BF16)


def _params(*sem):
    return pltpu.CompilerParams(dimension_semantics=sem, vmem_limit_bytes=VMEM_LIMIT)


def _mem_kv_kernel(mem_ref, g_ref, wk_ref, wvT_ref, k_ref, vT_ref):
    h = _rms_bf16(mem_ref[0], g_ref[...])
    k = jnp.dot(h, wk_ref[0], preferred_element_type=F32)
    vT = lax.dot_general(wvT_ref[0], h, _NT, preferred_element_type=F32)
    lane = lax.broadcasted_iota(jnp.int32, k.shape, 1)
    for hh in range(N_MEM_HEADS):
        mine = (lane >= hh * HEAD_DIM) & (lane < (hh + 1) * HEAD_DIM)
        k_ref[0, 0, hh] = jnp.where(mine, k, 0.0).astype(BF16)
    vT_ref[0, 0] = vT.astype(BF16)


def _mem_kv(mem, mem_norm_g, w_mem_kv):
    b, m, d = mem.shape
    depth = w_mem_kv.shape[0]
    wk = w_mem_kv[:, :, :MEM_WIDTH].astype(BF16)
    wvT = jnp.swapaxes(w_mem_kv[:, :, MEM_WIDTH:], 1, 2).astype(BF16)
    return pl.pallas_call(
        _mem_kv_kernel,
        grid=(depth, b),
        in_specs=[
            pl.BlockSpec((1, m, d), lambda i, bb: (bb, 0, 0)),
            pl.BlockSpec((1, d), lambda i, bb: (0, 0)),
            pl.BlockSpec((1, d, MEM_WIDTH), lambda i, bb: (i, 0, 0)),
            pl.BlockSpec((1, MEM_WIDTH, d), lambda i, bb: (i, 0, 0)),
        ],
        out_specs=[
            pl.BlockSpec((1, 1, N_MEM_HEADS, m, MEM_WIDTH), lambda i, bb: (i, bb, 0, 0, 0)),
            pl.BlockSpec((1, 1, MEM_WIDTH, m), lambda i, bb: (i, bb, 0, 0)),
        ],
        out_shape=[
            jax.ShapeDtypeStruct((depth, b, N_MEM_HEADS, m, MEM_WIDTH), BF16),
            jax.ShapeDtypeStruct((depth, b, MEM_WIDTH, m), BF16),
        ],
        compiler_params=_params("arbitrary", "arbitrary"),
        name="mem_kv",
    )(mem, mem_norm_g.reshape(1, d), wk, wvT)


def _moba_projection(h, wT_ref, w_ref, qT_ref, kaug_ref, vT_ref, kmean_ref, qmT_ref, z_ref):
    tm = h.shape[0]
    nb = tm // MOBA_BLOCK
    outT = lax.dot_general(wT_ref[...], h, _NT, preferred_element_type=F32)
    out = jnp.dot(h, w_ref[...], preferred_element_type=F32)
    for p in range(N_PAIRS):
        for bb in range(nb):
            cols = slice(bb * MOBA_BLOCK, (bb + 1) * MOBA_BLOCK)
            qT_ref[0, p, bb] = outT[p * LANES:(p + 1) * LANES, cols].astype(BF16)
            vT_ref[0, p, bb] = outT[ATTN_WIDTH + p * LANES:ATTN_WIDTH + (p + 1) * LANES, cols].astype(BF16)
    qmT_ref[0] = outT[2 * ATTN_WIDTH:, :].astype(BF16)
    k = out[:, :ATTN_WIDTH]
    z_ref[0] = out[:, ATTN_WIDTH:]
    row = lax.broadcasted_iota(jnp.int32, (tm, LANES), 0)
    lane = lax.broadcasted_iota(jnp.int32, (tm, LANES), 1)
    blk = pl.program_id(1) * nb + row // MOBA_BLOCK
    s_rel = (row % MOBA_BLOCK).astype(F32)
    ext = jnp.where(lane < SEL_COLS, (lane == blk).astype(F32),
                    jnp.where(lane < SEL_COLS + ALIBI_COLS, s_rel, 0.0)).astype(BF16)
    for p in range(N_PAIRS):
        kaug_ref[0, p, :, 0:LANES] = k[:, p * LANES:(p + 1) * LANES].astype(BF16)
        kaug_ref[0, p, :, LANES:AUG] = ext
    for bb in range(nb):
        kmean_ref[0, 0, bb:bb + 1, :] = jnp.mean(
            k[bb * MOBA_BLOCK:(bb + 1) * MOBA_BLOCK, :], axis=0, keepdims=True)


def _moba_projection_specs(b, s, tm):
    nblk = s // MOBA_BLOCK
    nb = tm // MOBA_BLOCK
    specs = [
        pl.BlockSpec((1, N_PAIRS, nb, LANES, MOBA_BLOCK), lambda bb, i: (bb, 0, i, 0, 0)),
        pl.BlockSpec((1, N_PAIRS, tm, AUG), lambda bb, i: (bb, 0, i, 0)),
        pl.BlockSpec((1, N_PAIRS, nb, LANES, MOBA_BLOCK), lambda bb, i: (bb, 0, i, 0, 0)),
        pl.BlockSpec((1, 1, nb, ATTN_WIDTH), lambda bb, i: (bb, i, 0, 0)),
        pl.BlockSpec((1, MEM_WIDTH, tm), lambda bb, i: (bb, 0, i)),
        pl.BlockSpec((1, tm, BRANCH_WIDTH), lambda bb, i: (bb, i, 0)),
    ]
    shapes = [
        jax.ShapeDtypeStruct((b, N_PAIRS, nblk, LANES, MOBA_BLOCK), BF16),
        jax.ShapeDtypeStruct((b, N_PAIRS, s, AUG), BF16),
        jax.ShapeDtypeStruct((b, N_PAIRS, nblk, LANES, MOBA_BLOCK), BF16),
        jax.ShapeDtypeStruct((b, s // tm, nb, ATTN_WIDTH), F32),
        jax.ShapeDtypeStruct((b, MEM_WIDTH, s), BF16),
        jax.ShapeDtypeStruct((b, s, BRANCH_WIDTH), F32),
    ]
    return specs, shapes


def _swa_projection(h, wT_ref, w_ref, qT_ref, k_ref, vT_ref, qmT_ref, z_ref):
    outT = lax.dot_general(wT_ref[...], h, _NT, preferred_element_type=F32)
    out = jnp.dot(h, w_ref[...], preferred_element_type=F32)
    qT_ref[0] = outT[:ATTN_WIDTH].astype(BF16)
    vT_ref[0] = outT[ATTN_WIDTH:ATTN_WIDTH + KV_WIDTH_B].astype(BF16)
    qmT_ref[0] = outT[ATTN_WIDTH + KV_WIDTH_B:].astype(BF16)
    k_ref[0] = out[:, :KV_WIDTH_B].astype(BF16)
    z_ref[0] = out[:, KV_WIDTH_B:]


def _swa_projection_specs(b, s, tm):
    specs = [
        pl.BlockSpec((1, ATTN_WIDTH, tm), lambda bb, i: (bb, 0, i)),
        pl.BlockSpec((1, tm, KV_WIDTH_B), lambda bb, i: (bb, i, 0)),
        pl.BlockSpec((1, KV_WIDTH_B, tm), lambda bb, i: (bb, 0, i)),
        pl.BlockSpec((1, MEM_WIDTH, tm), lambda bb, i: (bb, 0, i)),
        pl.BlockSpec((1, tm, BRANCH_WIDTH), lambda bb, i: (bb, i, 0)),
    ]
    shapes = [
        jax.ShapeDtypeStruct((b, ATTN_WIDTH, s), BF16),
        jax.ShapeDtypeStruct((b, s, KV_WIDTH_B), BF16),
        jax.ShapeDtypeStruct((b, KV_WIDTH_B, s), BF16),
        jax.ShapeDtypeStruct((b, MEM_WIDTH, s), BF16),
        jax.ShapeDtypeStruct((b, s, BRANCH_WIDTH), F32),
    ]
    return specs, shapes


_PROJECTIONS = {
    "moba": (_moba_projection, _moba_projection_specs),
    "swa": (_swa_projection, _swa_projection_specs),
}


def _split_in_weights(w_in, mixer):
    scale = HEAD_DIM ** -0.5
    kvw = ATTN_WIDTH if mixer == "moba" else KV_WIDTH_B
    o = np.cumsum([0, ATTN_WIDTH, kvw, kvw, MEM_WIDTH, BRANCH_WIDTH])
    q, k, v, qm, z = (w_in[:, o[i]:o[i + 1]] for i in range(5))
    wT = jnp.concatenate([q * scale, v, qm * scale], axis=1).T.astype(BF16)
    w = jnp.concatenate([k, z], axis=1).astype(BF16)
    return wT, w


def _first_projection_kernel(x_ref, g_ref, wT_ref, w_ref, *out_refs, mixer):
    h = _rms_bf16(x_ref[0], g_ref[...])
    _PROJECTIONS[mixer][0](h, wT_ref, w_ref, *out_refs)


def _first_projection(x, g, w_in, mixer):
    b, s, d = x.shape
    tm = min(ROW_TILE, s)
    wT, w = _split_in_weights(w_in, mixer)
    specs, shapes = _PROJECTIONS[mixer][1](b, s, tm)
    return pl.pallas_call(
        functools.partial(_first_projection_kernel, mixer=mixer),
        grid=(b, s // tm),
        in_specs=[
            pl.BlockSpec((1, tm, d), lambda bb, i: (bb, i, 0)),
            pl.BlockSpec((1, d), lambda bb, i: (0, 0)),
            pl.BlockSpec(wT.shape, lambda bb, i: (0, 0)),
            pl.BlockSpec(w.shape, lambda bb, i: (0, 0)),
        ],
        out_specs=specs,
        out_shape=shapes,
        compiler_params=_params("parallel", "arbitrary"),
        name=f"in_proj_{mixer}",
    )(x, g.reshape(1, d), wT, w)


def _memory_attention_T(qmT, mk_ref, mvT_ref):
    outs = []
    for hh in range(N_MEM_HEADS):
        sT = jnp.dot(mk_ref[0, hh], qmT, preferred_element_type=F32)
        m = jnp.max(sT, axis=0, keepdims=True)
        p = jnp.exp(sT - m)
        l = jnp.sum(p, axis=0, keepdims=True)
        o = jnp.dot(mvT_ref[0, hh * HEAD_DIM:(hh + 1) * HEAD_DIM, :], p.astype(BF16),
                    preferred_element_type=F32)
        outs.append(o / l)
    return jnp.concatenate(outs, axis=0)


def _epilogue_kernel(y_ref, qmT_ref, z_ref, x_ref, mk_ref, mvT_ref, wout_ref, g_ref, *rest, mixer):
    ymem = _memory_attention_T(qmT_ref[0], mk_ref, mvT_ref).T
    y = jnp.concatenate([y_ref[0].astype(F32), ymem], axis=-1)
    z = z_ref[0]
    gated = (y * (z / (1.0 + jnp.exp(-z)))).astype(BF16)
    xn = x_ref[0] + jnp.dot(gated, wout_ref[...], preferred_element_type=F32)
    if mixer is None:
        (o_ref,) = rest
        ms = jnp.mean(xn * xn, axis=-1, keepdims=True)
        o_ref[0] = xn * lax.rsqrt(ms + RMS_EPS) * g_ref[...]
    else:
        wT_ref, w_ref, xo_ref = rest[:3]
        xo_ref[0] = xn
        _PROJECTIONS[mixer][0](_rms_bf16(xn, g_ref[...]), wT_ref, w_ref, *rest[3:])


def _epilogue(y_self, qmT, z, x, mem_k, mem_vT, layer, w_out, g, w_in_next=None, mixer=None):
    b, s, d = x.shape
    tm = min(ROW_TILE, s)
    m = mem_k.shape[3]
    in_specs = [
        pl.BlockSpec((1, tm, ATTN_WIDTH), lambda bb, i: (bb, i, 0)),
        pl.BlockSpec((1, MEM_WIDTH, tm), lambda bb, i: (bb, 0, i)),
        pl.BlockSpec((1, tm, BRANCH_WIDTH), lambda bb, i: (bb, i, 0)),
        pl.BlockSpec((1, tm, d), lambda bb, i: (bb, i, 0)),
        pl.BlockSpec((1, N_MEM_HEADS, m, MEM_WIDTH), lambda bb, i: (bb, 0, 0, 0)),
        pl.BlockSpec((1, MEM_WIDTH, m), lambda bb, i: (bb, 0, 0)),
        pl.BlockSpec((BRANCH_WIDTH, d), lambda bb, i: (0, 0)),
        pl.BlockSpec((1, d), lambda bb, i: (0, 0)),
    ]
    args = [y_self, qmT, z, x, mem_k[layer], mem_vT[layer], w_out.astype(BF16), g.reshape(1, d)]
    x_spec = pl.BlockSpec((1, tm, d), lambda bb, i: (bb, i, 0))
    x_shape = jax.ShapeDtypeStruct((b, s, d), F32)
    if mixer is None:
        out_specs, out_shape = [x_spec], [x_shape]
    else:
        wT, w = _split_in_weights(w_in_next, mixer)
        in_specs += [pl.BlockSpec(wT.shape, lambda bb, i: (0, 0)),
                     pl.BlockSpec(w.shape, lambda bb, i: (0, 0))]
        args += [wT, w]
        specs, shapes = _PROJECTIONS[mixer][1](b, s, tm)
        out_specs, out_shape = [x_spec] + specs, [x_shape] + shapes
    return pl.pallas_call(
        functools.partial(_epilogue_kernel, mixer=mixer),
        grid=(b, s // tm),
        in_specs=in_specs,
        out_specs=out_specs,
        out_shape=out_shape,
        compiler_params=_params("parallel", "arbitrary"),
        name=f"epilogue_{mixer}",
    )(*args)


def _moba_kernel(qT_ref, kaug_ref, vT_ref, kmean_ref, srows_ref, slope_ref, o_ref):
    j = pl.program_id(2)
    nblk = kmean_ref.shape[1]
    tq = qT_ref.shape[-1]
    qT = qT_ref[0, 0, 0]
    kmean = kmean_ref[0]
    drow = lax.broadcasted_iota(jnp.int32, qT.shape, 0)
    klane = lax.broadcasted_iota(jnp.int32, kmean.shape, 1)
    blk = lax.broadcasted_iota(jnp.int32, (nblk, tq), 0)
    krow = lax.broadcasted_iota(jnp.int32, (MOBA_BLOCK, tq), 0)
    qlane = lax.broadcasted_iota(jnp.int32, (MOBA_BLOCK, tq), 1)
    causal = krow <= qlane

    q_aug, slopes = [], []
    for hh in range(HEAD_PAIR):
        lo, hi = hh * HEAD_DIM, (hh + 1) * HEAD_DIM
        q_h = jnp.where((drow >= lo) & (drow < hi), qT, jnp.zeros_like(qT))
        km_h = jnp.where((klane >= lo) & (klane < hi), kmean, 0.0).astype(BF16)
        gate = jnp.dot(km_h, qT, preferred_element_type=F32)
        gate = jnp.where(blk < j, gate, -jnp.inf)
        sel = blk == j
        for _ in range(MOBA_TOPK):
            best = jnp.max(gate, axis=0, keepdims=True)
            first = jnp.min(jnp.where(gate == best, blk, nblk), axis=0, keepdims=True)
            pick = blk == first
            sel = sel | (pick & (blk < j))
            gate = jnp.where(pick, -jnp.inf, gate)
        bias = jnp.where(sel, 0.0, MASKED).astype(BF16)
        if nblk < SEL_COLS:
            bias = jnp.concatenate([bias, jnp.zeros((SEL_COLS - nblk, tq), BF16)], axis=0)
        q_aug.append(jnp.concatenate([q_h, bias, srows_ref[0, hh]], axis=0))
        slopes.append(slope_ref[0, hh])

    def tile(n, hh, shift):
        sT = jnp.dot(kaug_ref[0, 0, n], q_aug[hh], preferred_element_type=F32)
        return sT

    carry = []
    for hh in range(HEAD_PAIR):
        sT = jnp.where(causal, tile(j, hh, None), MASKED)
        m = jnp.max(sT, axis=0, keepdims=True)
        p = jnp.exp(sT - m)
        l = jnp.sum(p, axis=0, keepdims=True)
        acc = jnp.dot(vT_ref[0, 0, j, hh * HEAD_DIM:(hh + 1) * HEAD_DIM, :], p.astype(BF16),
                      preferred_element_type=F32)
        carry += [m, l, acc]

    def past_block(n, carry):
        out = []
        off = ((n - j) * MOBA_BLOCK).astype(F32)
        for hh in range(HEAD_PAIR):
            m, l, acc = carry[3 * hh:3 * hh + 3]
            sT = tile(n, hh, None)
            c = slopes[hh] * off
            m_new = jnp.maximum(m, jnp.max(sT, axis=0, keepdims=True) + c)
            p = jnp.exp(sT - (m_new - c))
            alpha = jnp.exp(m - m_new)
            l = alpha * l + jnp.sum(p, axis=0, keepdims=True)
            acc = alpha * acc + jnp.dot(
                vT_ref[0, 0, n, hh * HEAD_DIM:(hh + 1) * HEAD_DIM, :], p.astype(BF16),
                preferred_element_type=F32)
            out += [m_new, l, acc]
        return out

    carry = lax.fori_loop(0, j, past_block, carry)
    oT = jnp.concatenate([carry[3 * hh + 2] / carry[3 * hh + 1] for hh in range(HEAD_PAIR)], axis=0)
    o_ref[0] = oT.T.astype(o_ref.dtype)


def _moba_tables(tq):
    slopes = _alibi_slopes(N_HEADS)
    pieces = _bf16_pieces(slopes)
    rows = np.zeros((N_HEADS, AUG - LANES - SEL_COLS, tq), np.float32)
    for c, piece in enumerate(pieces):
        rows[:, c, :] = piece[:, None]
    srows = jnp.asarray(rows.reshape(N_PAIRS, HEAD_PAIR, AUG - LANES - SEL_COLS, tq), BF16)
    svec = jnp.asarray(np.broadcast_to(slopes[:, None, None], (N_HEADS, 1, tq))
                       .reshape(N_PAIRS, HEAD_PAIR, 1, tq), F32)
    return srows, svec


def _moba_attention(qT, kaug, vT, kmean):
    b, _, nblk, _, tq = qT.shape
    assert nblk <= SEL_COLS and tq == MOBA_BLOCK
    s = nblk * MOBA_BLOCK
    kaug = kaug.reshape(b, N_PAIRS, nblk, MOBA_BLOCK, AUG)
    srows, svec = _moba_tables(tq)
    return pl.pallas_call(
        _moba_kernel,
        grid=(b, N_PAIRS, nblk),
        in_specs=[
            pl.BlockSpec((1, 1, 1, LANES, tq), lambda bb, p, j: (bb, p, j, 0, 0)),
            pl.BlockSpec((1, 1, nblk, MOBA_BLOCK, AUG), lambda bb, p, j: (bb, p, 0, 0, 0)),
            pl.BlockSpec((1, 1, nblk, LANES, MOBA_BLOCK), lambda bb, p, j: (bb, p, 0, 0, 0)),
            pl.BlockSpec((1, nblk, LANES), lambda bb, p, j: (bb, 0, p)),
            pl.BlockSpec((1, HEAD_PAIR, AUG - LANES - SEL_COLS, tq), lambda bb, p, j: (p, 0, 0, 0)),
            pl.BlockSpec((1, HEAD_PAIR, 1, tq), lambda bb, p, j: (p, 0, 0, 0)),
        ],
        out_specs=pl.BlockSpec((1, tq, LANES), lambda bb, p, j: (bb, j, p)),
        out_shape=jax.ShapeDtypeStruct((b, s, ATTN_WIDTH), BF16),
        compiler_params=_params("parallel", "parallel", "arbitrary"),
        name="moba_attention",
    )(qT, kaug, vT, kmean, srows, svec)


def _swa_kernel(qT_ref, kp_ref, kc_ref, vTp_ref, vTc_ref, bias_ref, sink_ref, o_ref, yT_ref):
    j = pl.program_id(1)
    tq = qT_ref.shape[-1]
    kwin = jnp.concatenate([kp_ref[0], kc_ref[0]], axis=0)
    vTwin = jnp.concatenate([vTp_ref[0], vTc_ref[0]], axis=1)
    krow = lax.broadcasted_iota(jnp.int32, (WINDOW + tq, tq), 0)
    in_seq = (krow >= WINDOW) | (j > 0)
    zeros = jnp.zeros((HEAD_DIM, tq), BF16)
    group = N_HEADS // N_KV_HEADS_B
    for h in range(N_HEADS):
        g = h // group
        q_h = qT_ref[0, h * HEAD_DIM:(h + 1) * HEAD_DIM, :]
        q_pad = jnp.concatenate([q_h, zeros] if g == 0 else [zeros, q_h], axis=0)
        sT = jnp.dot(kwin, q_pad, preferred_element_type=F32) + bias_ref[h]
        sT = jnp.where(in_seq, sT, MASKED)
        sink = sink_ref[h]
        m = jnp.maximum(jnp.max(sT, axis=0, keepdims=True), sink)
        p = jnp.exp(sT - m)
        denom = jnp.sum(p, axis=0, keepdims=True) + jnp.exp(sink - m)
        o = jnp.dot(vTwin[g * HEAD_DIM:(g + 1) * HEAD_DIM, :], p.astype(BF16),
                    preferred_element_type=F32)
        yT_ref[h * HEAD_DIM:(h + 1) * HEAD_DIM, :] = o / denom
    o_ref[0] = yT_ref[...].T.astype(o_ref.dtype)


def _swa_bias_table(tq):
    slopes = _alibi_slopes(N_HEADS)
    kw = np.arange(WINDOW + tq)[:, None]
    q = np.arange(tq)[None, :]
    dist = q + WINDOW - kw
    ok = (dist >= 0) & (dist < WINDOW)
    table = np.where(ok[None], -slopes[:, None, None] * dist[None].astype(np.float32),
                     np.float32(MASKED))
    return jnp.asarray(table, F32)


def _swa_attention(qT, k, vT, sinks):
    b, _, s = qT.shape
    tq = min(SWA_TILE, s)
    r = tq // WINDOW
    bias = _swa_bias_table(tq)
    sink_rows = jnp.broadcast_to(sinks.astype(F32)[:, None, None], (N_HEADS, 1, tq))
    prev = lambda j: jnp.maximum(j * r - 1, 0)
    return pl.pallas_call(
        _swa_kernel,
        grid=(b, s // tq),
        in_specs=[
            pl.BlockSpec((1, ATTN_WIDTH, tq), lambda bb, j: (bb, 0, j)),
            pl.BlockSpec((1, WINDOW, KV_WIDTH_B), lambda bb, j: (bb, prev(j), 0)),
            pl.BlockSpec((1, tq, KV_WIDTH_B), lambda bb, j: (bb, j, 0)),
            pl.BlockSpec((1, KV_WIDTH_B, WINDOW), lambda bb, j: (bb, 0, prev(j))),
            pl.BlockSpec((1, KV_WIDTH_B, tq), lambda bb, j: (bb, 0, j)),
            pl.BlockSpec(bias.shape, lambda bb, j: (0, 0, 0)),
            pl.BlockSpec(sink_rows.shape, lambda bb, j: (0, 0, 0)),
        ],
        out_specs=pl.BlockSpec((1, tq, ATTN_WIDTH), lambda bb, j: (bb, j, 0)),
        out_shape=jax.ShapeDtypeStruct((b, s, ATTN_WIDTH), BF16),
        scratch_shapes=[pltpu.VMEM((ATTN_WIDTH, tq), F32)],
        compiler_params=_params("parallel", "arbitrary"),
        name="swa_attention",
    )(qT, k, k, vT, vT, bias, sink_rows)


def kernel(x, mem, norm_g, w_in_a, w_in_b, sinks_b, w_mem_kv, w_out, mem_norm_g, final_norm_g):
    depth = norm_g.shape[0]
    b, s, _ = x.shape
    assert s % ROW_TILE == 0 or s < ROW_TILE
    mem_k, mem_vT = _mem_kv(mem, mem_norm_g, w_mem_kv)
    mixers = ["moba" if i % 2 == 0 else "swa" for i in range(depth)]
    w_in = [w_in_a[i // 2] if i % 2 == 0 else w_in_b[i // 2] for i in range(depth)]
    proj = _first_projection(x, norm_g[0], w_in[0], mixers[0])
    for i in range(depth):
        if mixers[i] == "moba":
            qT, kaug, vT, kmean, qmT, z = proj
            y_self = _moba_attention(qT, kaug, vT, kmean.reshape(b, s // MOBA_BLOCK, ATTN_WIDTH))
        else:
            qT, k, vT, qmT, z = proj
            y_self = _swa_attention(qT, k, vT, sinks_b[i // 2])
        if i + 1 < depth:
            x, *proj = _epilogue(y_self, qmT, z, x, mem_k, mem_vT, i, w_out[i], norm_g[i + 1],
                                 w_in[i + 1], mixers[i + 1])
        else:
            (x,) = _epilogue(y_self, qmT, z, x, mem_k, mem_vT, i, w_out[i], final_norm_g)
    return x
```

```python
import functools
import math

import jax
import jax.numpy as jnp
import numpy as np
from jax import lax
from jax.experimental import pallas as pl
from jax.experimental.pallas import tpu as pltpu

HEAD_DIM = 64
N_HEADS = 12
N_KV_HEADS_B = 2
N_MEM_HEADS = 4
ATTN_WIDTH = N_HEADS * HEAD_DIM
MEM_WIDTH = N_MEM_HEADS * HEAD_DIM
BRANCH_WIDTH = ATTN_WIDTH + MEM_WIDTH
KV_WIDTH_B = N_KV_HEADS_B * HEAD_DIM
MOBA_BLOCK = 256
MOBA_TOPK = 3
WINDOW = 128
RMS_EPS = 1e-6

LANES = 128
HEAD_PAIR = LANES // HEAD_DIM
N_PAIRS = N_HEADS // HEAD_PAIR
SEL_COLS = 64
ALIBI_COLS = 4
ONES_ROWS = 16
V_ROWS = HEAD_DIM + ONES_ROWS
LOG2E = math.log2(math.e)
AUG = 2 * LANES
MASKED = -(2.0 ** 30)
OVERFLOW_GUARD = 2.0 ** 100
PAST_UNROLL = 8
ROW_TILE = 512
SWA_TILE = 256
VMEM_LIMIT = 56 * 1024 * 1024

F32 = jnp.float32
BF16 = jnp.bfloat16
_NT = (((1,), (1,)), ((), ()))


def _alibi_slopes(n_heads):
    def pow2_slopes(n):
        start = 2.0 ** (-8.0 / n)
        return [start ** (i + 1) for i in range(n)]
    if math.log2(n_heads).is_integer():
        vals = pow2_slopes(n_heads)
    else:
        c = 2 ** math.floor(math.log2(n_heads))
        vals = pow2_slopes(c) + pow2_slopes(2 * c)[0::2][: n_heads - c]
    return np.array(vals, dtype=np.float32)


def _bf16_pieces(v):
    rest = np.asarray(v, np.float64)
    pieces = []
    for _ in range(ALIBI_COLS):
        p = rest.astype(BF16).astype(np.float64)
        pieces.append(p.astype(np.float32))
        rest = rest - p
    assert np.all(np.abs(rest) <= np.abs(v) * 2.0 ** -30)
    return pieces


def _rms_bf16(x, g):
    ms = jnp.mean(x * x, axis=-1, keepdims=True)
    return (x * lax.rsqrt(ms + RMS_EPS) * g).astype(BF16)


def _params(*sem):
    return pltpu.CompilerParams(dimension_semantics=sem, vmem_limit_bytes=VMEM_LIMIT)


def _mem_kv_kernel(mem_ref, g_ref, wk_ref, wvT_ref, k_ref, vT_ref):
    h = _rms_bf16(mem_ref[0], g_ref[...])
    k = jnp.dot(h, wk_ref[0], preferred_element_type=F32)
    vT = lax.dot_general(wvT_ref[0], h, _NT, preferred_element_type=F32)
    lane = lax.broadcasted_iota(jnp.int32, k.shape, 1)
    for hh in range(N_MEM_HEADS):
        mine = (lane >= hh * HEAD_DIM) & (lane < (hh + 1) * HEAD_DIM)
        k_ref[0, 0, hh] = jnp.where(mine, k, 0.0).astype(BF16)
    vT_ref[0, 0] = vT.astype(BF16)


def _mem_kv(mem, mem_norm_g, w_mem_kv):
    b, m, d = mem.shape
    depth = w_mem_kv.shape[0]
    wk = w_mem_kv[:, :, :MEM_WIDTH].astype(BF16)
    wvT = jnp.swapaxes(w_mem_kv[:, :, MEM_WIDTH:], 1, 2).astype(BF16)
    return pl.pallas_call(
        _mem_kv_kernel,
        grid=(depth, b),
        in_specs=[
            pl.BlockSpec((1, m, d), lambda i, bb: (bb, 0, 0)),
            pl.BlockSpec((1, d), lambda i, bb: (0, 0)),
            pl.BlockSpec((1, d, MEM_WIDTH), lambda i, bb: (i, 0, 0)),
            pl.BlockSpec((1, MEM_WIDTH, d), lambda i, bb: (i, 0, 0)),
        ],
        out_specs=[
            pl.BlockSpec((1, 1, N_MEM_HEADS, m, MEM_WIDTH), lambda i, bb: (i, bb, 0, 0, 0)),
            pl.BlockSpec((1, 1, MEM_WIDTH, m), lambda i, bb: (i, bb, 0, 0)),
        ],
        out_shape=[
            jax.ShapeDtypeStruct((depth, b, N_MEM_HEADS, m, MEM_WIDTH), BF16),
            jax.ShapeDtypeStruct((depth, b, MEM_WIDTH, m), BF16),
        ],
        compiler_params=_params("arbitrary", "arbitrary"),
        name="mem_kv",
    )(mem, mem_norm_g.reshape(1, d), wk, wvT)


def _moba_projection(h, wT_ref, w_ref, qT_ref, kaug_ref, vT_ref, kmean_ref, qmT_ref, z_ref):
    tm = h.shape[0]
    nb = tm // MOBA_BLOCK
    outT = lax.dot_general(wT_ref[...], h, _NT, preferred_element_type=F32)
    out = jnp.dot(h, w_ref[...], preferred_element_type=F32)
    ones = jnp.ones((ONES_ROWS, MOBA_BLOCK), BF16)
    for p in range(N_PAIRS):
        for bb in range(nb):
            cols = slice(bb * MOBA_BLOCK, (bb + 1) * MOBA_BLOCK)
            qT_ref[0, p, bb] = outT[p * LANES:(p + 1) * LANES, cols].astype(BF16)
            for hh in range(HEAD_PAIR):
                v0 = ATTN_WIDTH + (p * HEAD_PAIR + hh) * HEAD_DIM
                vT_ref[0, p, bb, hh * V_ROWS:hh * V_ROWS + HEAD_DIM] = (
                    outT[v0:v0 + HEAD_DIM, cols].astype(BF16))
                vT_ref[0, p, bb, hh * V_ROWS + HEAD_DIM:(hh + 1) * V_ROWS] = ones
    qmT_ref[0] = outT[2 * ATTN_WIDTH:, :].astype(BF16)
    k = out[:, :ATTN_WIDTH]
    z_ref[0] = out[:, ATTN_WIDTH:]
    row = lax.broadcasted_iota(jnp.int32, (tm, LANES), 0)
    lane = lax.broadcasted_iota(jnp.int32, (tm, LANES), 1)
    blk = pl.program_id(1) * nb + row // MOBA_BLOCK
    s_rel = (row % MOBA_BLOCK).astype(F32)
    ext = jnp.where(lane < SEL_COLS, (lane == blk).astype(F32),
                    jnp.where(lane < SEL_COLS + ALIBI_COLS, s_rel, 0.0)).astype(BF16)
    for p in range(N_PAIRS):
        kaug_ref[0, p, :, 0:LANES] = k[:, p * LANES:(p + 1) * LANES].astype(BF16)
        kaug_ref[0, p, :, LANES:AUG] = ext
    for bb in range(nb):
        kmean_ref[0, 0, bb:bb + 1, :] = jnp.mean(
            k[bb * MOBA_BLOCK:(bb + 1) * MOBA_BLOCK, :], axis=0, keepdims=True)


def _moba_projection_specs(b, s, tm):
    nblk = s // MOBA_BLOCK
    nb = tm // MOBA_BLOCK
    specs = [
        pl.BlockSpec((1, N_PAIRS, nb, LANES, MOBA_BLOCK), lambda bb, i: (bb, 0, i, 0, 0)),
        pl.BlockSpec((1, N_PAIRS, tm, AUG), lambda bb, i: (bb, 0, i, 0)),
        pl.BlockSpec((1, N_PAIRS, nb, HEAD_PAIR * V_ROWS, MOBA_BLOCK),
                     lambda bb, i: (bb, 0, i, 0, 0)),
        pl.BlockSpec((1, 1, nb, ATTN_WIDTH), lambda bb, i: (bb, i, 0, 0)),
        pl.BlockSpec((1, MEM_WIDTH, tm), lambda bb, i: (bb, 0, i)),
        pl.BlockSpec((1, tm, BRANCH_WIDTH), lambda bb, i: (bb, i, 0)),
    ]
    shapes = [
        jax.ShapeDtypeStruct((b, N_PAIRS, nblk, LANES, MOBA_BLOCK), BF16),
        jax.ShapeDtypeStruct((b, N_PAIRS, s, AUG), BF16),
        jax.ShapeDtypeStruct((b, N_PAIRS, nblk, HEAD_PAIR * V_ROWS, MOBA_BLOCK), BF16),
        jax.ShapeDtypeStruct((b, s // tm, nb, ATTN_WIDTH), F32),
        jax.ShapeDtypeStruct((b, MEM_WIDTH, s), BF16),
        jax.ShapeDtypeStruct((b, s, BRANCH_WIDTH), F32),
    ]
    return specs, shapes


def _swa_projection(h, wT_ref, w_ref, qT_ref, k_ref, vT_ref, qmT_ref, z_ref):
    outT = lax.dot_general(wT_ref[...], h, _NT, preferred_element_type=F32)
    out = jnp.dot(h, w_ref[...], preferred_element_type=F32)
    qT_ref[0] = outT[:ATTN_WIDTH].astype(BF16)
    vT_ref[0] = outT[ATTN_WIDTH:ATTN_WIDTH + KV_WIDTH_B].astype(BF16)
    qmT_ref[0] = outT[ATTN_WIDTH + KV_WIDTH_B:].astype(BF16)
    k_ref[0] = out[:, :KV_WIDTH_B].astype(BF16)
    z_ref[0] = out[:, KV_WIDTH_B:]


def _swa_projection_specs(b, s, tm):
    specs = [
        pl.BlockSpec((1, ATTN_WIDTH, tm), lambda bb, i: (bb, 0, i)),
        pl.BlockSpec((1, tm, KV_WIDTH_B), lambda bb, i: (bb, i, 0)),
        pl.BlockSpec((1, KV_WIDTH_B, tm), lambda bb, i: (bb, 0, i)),
        pl.BlockSpec((1, MEM_WIDTH, tm), lambda bb, i: (bb, 0, i)),
        pl.BlockSpec((1, tm, BRANCH_WIDTH), lambda bb, i: (bb, i, 0)),
    ]
    shapes = [
        jax.ShapeDtypeStruct((b, ATTN_WIDTH, s), BF16),
        jax.ShapeDtypeStruct((b, s, KV_WIDTH_B), BF16),
        jax.ShapeDtypeStruct((b, KV_WIDTH_B, s), BF16),
        jax.ShapeDtypeStruct((b, MEM_WIDTH, s), BF16),
        jax.ShapeDtypeStruct((b, s, BRANCH_WIDTH), F32),
    ]
    return specs, shapes


_PROJECTIONS = {
    "moba": (_moba_projection, _moba_projection_specs),
    "swa": (_swa_projection, _swa_projection_specs),
}


def _split_in_weights(w_in, mixer):
    scale = HEAD_DIM ** -0.5
    kvw = ATTN_WIDTH if mixer == "moba" else KV_WIDTH_B
    o = np.cumsum([0, ATTN_WIDTH, kvw, kvw, MEM_WIDTH, BRANCH_WIDTH])
    q, k, v, qm, z = (w_in[:, o[i]:o[i + 1]] for i in range(5))
    q_scale = scale * LOG2E if mixer == "moba" else scale
    wT = jnp.concatenate([q * q_scale, v, qm * scale], axis=1).T.astype(BF16)
    w = jnp.concatenate([k, z], axis=1).astype(BF16)
    return wT, w


def _first_projection_kernel(x_ref, g_ref, wT_ref, w_ref, *out_refs, mixer):
    h = _rms_bf16(x_ref[0], g_ref[...])
    _PROJECTIONS[mixer][0](h, wT_ref, w_ref, *out_refs)


def _first_projection(x, g, w_in, mixer):
    b, s, d = x.shape
    tm = min(ROW_TILE, s)
    wT, w = _split_in_weights(w_in, mixer)
    specs, shapes = _PROJECTIONS[mixer][1](b, s, tm)
    return pl.pallas_call(
        functools.partial(_first_projection_kernel, mixer=mixer),
        grid=(b, s // tm),
        in_specs=[
            pl.BlockSpec((1, tm, d), lambda bb, i: (bb, i, 0)),
            pl.BlockSpec((1, d), lambda bb, i: (0, 0)),
            pl.BlockSpec(wT.shape, lambda bb, i: (0, 0)),
            pl.BlockSpec(w.shape, lambda bb, i: (0, 0)),
        ],
        out_specs=specs,
        out_shape=shapes,
        compiler_params=_params("parallel", "arbitrary"),
        name=f"in_proj_{mixer}",
    )(x, g.reshape(1, d), wT, w)


def _memory_attention_T(qmT, mk_ref, mvT_ref):
    outs = []
    for hh in range(N_MEM_HEADS):
        sT = jnp.dot(mk_ref[0, hh], qmT, preferred_element_type=F32)
        m = jnp.max(sT, axis=0, keepdims=True)
        p = jnp.exp(sT - m)
        l = jnp.sum(p, axis=0, keepdims=True)
        o = jnp.dot(mvT_ref[0, hh * HEAD_DIM:(hh + 1) * HEAD_DIM, :], p.astype(BF16),
                    preferred_element_type=F32)
        outs.append(o / l)
    return jnp.concatenate(outs, axis=0)


def _epilogue_kernel(y_ref, qmT_ref, z_ref, x_ref, mk_ref, mvT_ref, wout_ref, g_ref, *rest, mixer):
    ymem = _memory_attention_T(qmT_ref[0], mk_ref, mvT_ref).T
    y = jnp.concatenate([y_ref[0].astype(F32), ymem], axis=-1)
    z = z_ref[0]
    gated = (y * (z / (1.0 + jnp.exp(-z)))).astype(BF16)
    xn = x_ref[0] + jnp.dot(gated, wout_ref[...], preferred_element_type=F32)
    if mixer is None:
        (o_ref,) = rest
        ms = jnp.mean(xn * xn, axis=-1, keepdims=True)
        o_ref[0] = xn * lax.rsqrt(ms + RMS_EPS) * g_ref[...]
    else:
        wT_ref, w_ref, xo_ref = rest[:3]
        xo_ref[0] = xn
        _PROJECTIONS[mixer][0](_rms_bf16(xn, g_ref[...]), wT_ref, w_ref, *rest[3:])


def _epilogue(y_self, qmT, z, x, mem_k, mem_vT, layer, w_out, g, w_in_next=None, mixer=None):
    b, s, d = x.shape
    tm = min(ROW_TILE, s)
    m = mem_k.shape[3]
    in_specs = [
        pl.BlockSpec((1, tm, ATTN_WIDTH), lambda bb, i: (bb, i, 0)),
        pl.BlockSpec((1, MEM_WIDTH, tm), lambda bb, i: (bb, 0, i)),
        pl.BlockSpec((1, tm, BRANCH_WIDTH), lambda bb, i: (bb, i, 0)),
        pl.BlockSpec((1, tm, d), lambda bb, i: (bb, i, 0)),
        pl.BlockSpec((1, N_MEM_HEADS, m, MEM_WIDTH), lambda bb, i: (bb, 0, 0, 0)),
        pl.BlockSpec((1, MEM_WIDTH, m), lambda bb, i: (bb, 0, 0)),
        pl.BlockSpec((BRANCH_WIDTH, d), lambda bb, i: (0, 0)),
        pl.BlockSpec((1, d), lambda bb, i: (0, 0)),
    ]
    args = [y_self, qmT, z, x, mem_k[layer], mem_vT[layer], w_out.astype(BF16), g.reshape(1, d)]
    x_spec = pl.BlockSpec((1, tm, d), lambda bb, i: (bb, i, 0))
    x_shape = jax.ShapeDtypeStruct((b, s, d), F32)
    if mixer is None:
        out_specs, out_shape = [x_spec], [x_shape]
    else:
        wT, w = _split_in_weights(w_in_next, mixer)
        in_specs += [pl.BlockSpec(wT.shape, lambda bb, i: (0, 0)),
                     pl.BlockSpec(w.shape, lambda bb, i: (0, 0))]
        args += [wT, w]
        specs, shapes = _PROJECTIONS[mixer][1](b, s, tm)
        out_specs, out_shape = [x_spec] + specs, [x_shape] + shapes
    return pl.pallas_call(
        functools.partial(_epilogue_kernel, mixer=mixer),
        grid=(b, s // tm),
        in_specs=in_specs,
        out_specs=out_specs,
        out_shape=out_shape,
        compiler_params=_params("parallel", "arbitrary"),
        name=f"epilogue_{mixer}",
    )(*args)


def _moba_kernel(qT_ref, kaug_ref, vT_ref, kmean_ref, srows_ref, slope_ref, o_ref, qaug_ref):
    j = pl.program_id(2)
    nblk = kmean_ref.shape[1]
    tq = qT_ref.shape[-1]
    qT = qT_ref[0, 0, 0]
    kmean = kmean_ref[0]
    drow = lax.broadcasted_iota(jnp.int32, qT.shape, 0)
    klane = lax.broadcasted_iota(jnp.int32, kmean.shape, 1)
    blk = lax.broadcasted_iota(jnp.int32, (nblk, tq), 0)
    krow = lax.broadcasted_iota(jnp.int32, (MOBA_BLOCK, tq), 0)
    qlane = lax.broadcasted_iota(jnp.int32, (MOBA_BLOCK, tq), 1)
    causal = krow <= qlane

    for hh in range(HEAD_PAIR):
        lo, hi = hh * HEAD_DIM, (hh + 1) * HEAD_DIM
        q_h = jnp.where((drow >= lo) & (drow < hi), qT, jnp.zeros_like(qT))
        km_h = jnp.where((klane >= lo) & (klane < hi), kmean, 0.0).astype(BF16)
        gate = jnp.dot(km_h, qT, preferred_element_type=F32)
        gate = jnp.where(blk < j, gate, -jnp.inf)
        sel = blk == j
        for _ in range(MOBA_TOPK):
            best = jnp.max(gate, axis=0, keepdims=True)
            first = jnp.min(jnp.where(gate == best, blk, nblk), axis=0, keepdims=True)
            pick = blk == first
            sel = sel | (pick & (blk < j))
            gate = jnp.where(pick, -jnp.inf, gate)
        bias = jnp.where(sel, 0.0, MASKED).astype(BF16)
        if nblk < SEL_COLS:
            bias = jnp.concatenate([bias, jnp.zeros((SEL_COLS - nblk, tq), BF16)], axis=0)
        qaug_ref[hh] = jnp.concatenate([q_h, bias, srows_ref[0, hh]], axis=0)

    def scores(n, hh):
        return jnp.dot(kaug_ref[0, 0, n], qaug_ref[hh], preferred_element_type=F32)

    def weighted_values(n, hh, p):
        return jnp.dot(vT_ref[0, 0, n, hh * V_ROWS:(hh + 1) * V_ROWS, :], p.astype(BF16),
                       preferred_element_type=F32)

    def block_offset(n, hh):
        return slope_ref[0, hh] * ((n - j) * MOBA_BLOCK).astype(F32)

    own_max, own_acc = [], []
    for hh in range(HEAD_PAIR):
        sT = jnp.where(causal, scores(j, hh), MASKED)
        m = jnp.max(sT, axis=0, keepdims=True)
        own_max.append(m)
        own_acc.append(weighted_values(j, hh, jnp.exp2(sT - m)))

    def finish(accs):
        oT = jnp.concatenate([acc[:HEAD_DIM] / acc[HEAD_DIM:HEAD_DIM + 1] for acc in accs], axis=0)
        o_ref[0] = oT.T.astype(o_ref.dtype)

    def fixed_shift_blocks(g, accs):
        slots = [g * PAST_UNROLL + u for u in range(PAST_UNROLL)]
        blocks = [jnp.minimum(n, j) for n in slots]
        sT = [[scores(n, hh) for hh in range(HEAD_PAIR)] for n in blocks]
        accs = list(accs)
        for u, n in enumerate(blocks):
            for hh in range(HEAD_PAIR):
                shift = jnp.where(slots[u] < j, own_max[hh] - block_offset(n, hh), jnp.inf)
                accs[hh] += weighted_values(n, hh, jnp.exp2(sT[u][hh] - shift))
        return accs

    fast = lax.fori_loop(0, (j + PAST_UNROLL - 1) // PAST_UNROLL, fixed_shift_blocks, own_acc)
    finish(fast)
    denominators = jnp.maximum(fast[0][HEAD_DIM:HEAD_DIM + 1], fast[1][HEAD_DIM:HEAD_DIM + 1])
    in_range = jnp.max(denominators) < OVERFLOW_GUARD

    @pl.when(jnp.logical_not(in_range))
    def _():
        def running_max_block(n, carry):
            out = []
            for hh in range(HEAD_PAIR):
                m, acc = carry[2 * hh:2 * hh + 2]
                sT = scores(n, hh)
                c = block_offset(n, hh)
                m_new = jnp.maximum(m, jnp.max(sT, axis=0, keepdims=True) + c)
                acc = jnp.exp2(m - m_new) * acc + weighted_values(n, hh, jnp.exp2(sT - (m_new - c)))
                out += [m_new, acc]
            return out

        slow = lax.fori_loop(0, j, running_max_block,
                             [t for hh in range(HEAD_PAIR) for t in (own_max[hh], own_acc[hh])])
        finish(slow[1::2])


def _moba_tables(tq):
    slopes = _alibi_slopes(N_HEADS).astype(np.float64) * LOG2E
    pieces = _bf16_pieces(slopes)
    rows = np.zeros((N_HEADS, AUG - LANES - SEL_COLS, tq), np.float32)
    for c, piece in enumerate(pieces):
        rows[:, c, :] = piece[:, None]
    srows = jnp.asarray(rows.reshape(N_PAIRS, HEAD_PAIR, AUG - LANES - SEL_COLS, tq), BF16)
    svec = jnp.asarray(np.broadcast_to(slopes[:, None, None], (N_HEADS, 1, tq))
                       .reshape(N_PAIRS, HEAD_PAIR, 1, tq), F32)
    return srows, svec


def _moba_attention(qT, kaug, vT, kmean):
    b, _, nblk, _, tq = qT.shape
    assert nblk <= SEL_COLS and tq == MOBA_BLOCK
    s = nblk * MOBA_BLOCK
    kaug = kaug.reshape(b, N_PAIRS, nblk, MOBA_BLOCK, AUG)
    srows, svec = _moba_tables(tq)
    return pl.pallas_call(
        _moba_kernel,
        grid=(b, N_PAIRS, nblk),
        in_specs=[
            pl.BlockSpec((1, 1, 1, LANES, tq), lambda bb, p, j: (bb, p, j, 0, 0)),
            pl.BlockSpec((1, 1, nblk, MOBA_BLOCK, AUG), lambda bb, p, j: (bb, p, 0, 0, 0)),
            pl.BlockSpec((1, 1, nblk, HEAD_PAIR * V_ROWS, MOBA_BLOCK),
                         lambda bb, p, j: (bb, p, 0, 0, 0)),
            pl.BlockSpec((1, nblk, LANES), lambda bb, p, j: (bb, 0, p)),
            pl.BlockSpec((1, HEAD_PAIR, AUG - LANES - SEL_COLS, tq), lambda bb, p, j: (p, 0, 0, 0)),
            pl.BlockSpec((1, HEAD_PAIR, 1, tq), lambda bb, p, j: (p, 0, 0, 0)),
        ],
        out_specs=pl.BlockSpec((1, tq, LANES), lambda bb, p, j: (bb, j, p)),
        out_shape=jax.ShapeDtypeStruct((b, s, ATTN_WIDTH), BF16),
        scratch_shapes=[pltpu.VMEM((HEAD_PAIR, AUG, tq), BF16)],
        compiler_params=_params("parallel", "parallel", "arbitrary"),
        name="moba_attention",
    )(qT, kaug, vT, kmean, srows, svec)


def _swa_kernel(qT_ref, kp_ref, kc_ref, vTp_ref, vTc_ref, bias_ref, sink_ref, o_ref, yT_ref):
    j = pl.program_id(1)
    tq = qT_ref.shape[-1]
    kwin = jnp.concatenate([kp_ref[0], kc_ref[0]], axis=0)
    vTwin = jnp.concatenate([vTp_ref[0], vTc_ref[0]], axis=1)
    krow = lax.broadcasted_iota(jnp.int32, (WINDOW + tq, tq), 0)
    in_seq = (krow >= WINDOW) | (j > 0)
    zeros = jnp.zeros((HEAD_DIM, tq), BF16)
    group = N_HEADS // N_KV_HEADS_B
    for h in range(N_HEADS):
        g = h // group
        q_h = qT_ref[0, h * HEAD_DIM:(h + 1) * HEAD_DIM, :]
        q_pad = jnp.concatenate([q_h, zeros] if g == 0 else [zeros, q_h], axis=0)
        sT = jnp.dot(kwin, q_pad, preferred_element_type=F32) + bias_ref[h]
        sT = jnp.where(in_seq, sT, MASKED)
        sink = sink_ref[h]
        m = jnp.maximum(jnp.max(sT, axis=0, keepdims=True), sink)
        p = jnp.exp(sT - m)
        denom = jnp.sum(p, axis=0, keepdims=True) + jnp.exp(sink - m)
        o = jnp.dot(vTwin[g * HEAD_DIM:(g + 1) * HEAD_DIM, :], p.astype(BF16),
                    preferred_element_type=F32)
        yT_ref[h * HEAD_DIM:(h + 1) * HEAD_DIM, :] = o / denom
    o_ref[0] = yT_ref[...].T.astype(o_ref.dtype)


def _swa_bias_table(tq):
    slopes = _alibi_slopes(N_HEADS)
    kw = np.arange(WINDOW + tq)[:, None]
    q = np.arange(tq)[None, :]
    dist = q + WINDOW - kw
    ok = (dist >= 0) & (dist < WINDOW)
    table = np.where(ok[None], -slopes[:, None, None] * dist[None].astype(np.float32),
                     np.float32(MASKED))
    return jnp.asarray(table, F32)


def _swa_attention(qT, k, vT, sinks):
    b, _, s = qT.shape
    tq = min(SWA_TILE, s)
    r = tq // WINDOW
    bias = _swa_bias_table(tq)
    sink_rows = jnp.broadcast_to(sinks.astype(F32)[:, None, None], (N_HEADS, 1, tq))
    prev = lambda j: jnp.maximum(j * r - 1, 0)
    return pl.pallas_call(
        _swa_kernel,
        grid=(b, s // tq),
        in_specs=[
            pl.BlockSpec((1, ATTN_WIDTH, tq), lambda bb, j: (bb, 0, j)),
            pl.BlockSpec((1, WINDOW, KV_WIDTH_B), lambda bb, j: (bb, prev(j), 0)),
            pl.BlockSpec((1, tq, KV_WIDTH_B), lambda bb, j: (bb, j, 0)),
            pl.BlockSpec((1, KV_WIDTH_B, WINDOW), lambda bb, j: (bb, 0, prev(j))),
            pl.BlockSpec((1, KV_WIDTH_B, tq), lambda bb, j: (bb, 0, j)),
            pl.BlockSpec(bias.shape, lambda bb, j: (0, 0, 0)),
            pl.BlockSpec(sink_rows.shape, lambda bb, j: (0, 0, 0)),
        ],
        out_specs=pl.BlockSpec((1, tq, ATTN_WIDTH), lambda bb, j: (bb, j, 0)),
        out_shape=jax.ShapeDtypeStruct((b, s, ATTN_WIDTH), BF16),
        scratch_shapes=[pltpu.VMEM((ATTN_WIDTH, tq), F32)],
        compiler_params=_params("parallel", "arbitrary"),
        name="swa_attention",
    )(qT, k, k, vT, vT, bias, sink_rows)


def kernel(x, mem, norm_g, w_in_a, w_in_b, sinks_b, w_mem_kv, w_out, mem_norm_g, final_norm_g):
    depth = norm_g.shape[0]
    b, s, _ = x.shape
    assert s % ROW_TILE == 0 or s < ROW_TILE
    mem_k, mem_vT = _mem_kv(mem, mem_norm_g, w_mem_kv)
    mixers = ["moba" if i % 2 == 0 else "swa" for i in range(depth)]
    w_in = [w_in_a[i // 2] if i % 2 == 0 else w_in_b[i // 2] for i in range(depth)]
    proj = _first_projection(x, norm_g[0], w_in[0], mixers[0])
    for i in range(depth):
        if mixers[i] == "moba":
            qT, kaug, vT, kmean, qmT, z = proj
            y_self = _moba_attention(qT, kaug, vT, kmean.reshape(b, s // MOBA_BLOCK, ATTN_WIDTH))
        else:
            qT, k, vT, qmT, z = proj
            y_self = _swa_attention(qT, k, vT, sinks_b[i // 2])
        if i + 1 < depth:
            x, *proj = _epilogue(y_self, qmT, z, x, mem_k, mem_vT, i, w_out[i], norm_g[i + 1],
                                 w_in[i + 1], mixers[i + 1])
        else:
            (x,) = _epilogue(y_self, qmT, z, x, mem_k, mem_vT, i, w_out[i], final_norm_g)
    return x
```

```python
import functools
import math

import jax
import jax.numpy as jnp
import numpy as np
from jax import lax
from jax.experimental import pallas as pl
from jax.experimental.pallas import tpu as pltpu

HEAD_DIM = 64
N_HEADS = 12
N_KV_HEADS_B = 2
N_MEM_HEADS = 4
ATTN_WIDTH = N_HEADS * HEAD_DIM
MEM_WIDTH = N_MEM_HEADS * HEAD_DIM
BRANCH_WIDTH = ATTN_WIDTH + MEM_WIDTH
KV_WIDTH_B = N_KV_HEADS_B * HEAD_DIM
MOBA_BLOCK = 256
MOBA_TOPK = 3
WINDOW = 128
RMS_EPS = 1e-6

LANES = 128
HEAD_PAIR = LANES // HEAD_DIM
N_PAIRS = N_HEADS // HEAD_PAIR
SEL_COLS = 64
ALIBI_COLS = 4
ONES_ROWS = 16
V_ROWS = HEAD_DIM + ONES_ROWS
LOG2E = math.log2(math.e)
AUG = 2 * LANES
MASKED = -(2.0 ** 30)
OVERFLOW_GUARD = 2.0 ** 100
PAST_UNROLL = 16
TAIL_UNROLL = 4
ROW_TILE = 512
SWA_TILE = 256
SWA_HEADS_AHEAD = 4
VMEM_LIMIT = 56 * 1024 * 1024

F32 = jnp.float32
BF16 = jnp.bfloat16
_NT = (((1,), (1,)), ((), ()))


def _alibi_slopes(n_heads):
    def pow2_slopes(n):
        start = 2.0 ** (-8.0 / n)
        return [start ** (i + 1) for i in range(n)]
    if math.log2(n_heads).is_integer():
        vals = pow2_slopes(n_heads)
    else:
        c = 2 ** math.floor(math.log2(n_heads))
        vals = pow2_slopes(c) + pow2_slopes(2 * c)[0::2][: n_heads - c]
    return np.array(vals, dtype=np.float32)


def _bf16_pieces(v):
    rest = np.asarray(v, np.float64)
    pieces = []
    for _ in range(ALIBI_COLS):
        p = rest.astype(BF16).astype(np.float64)
        pieces.append(p.astype(np.float32))
        rest = rest - p
    assert np.all(np.abs(rest) <= np.abs(v) * 2.0 ** -30)
    return pieces


def _rms_bf16(x, g):
    ms = jnp.mean(x * x, axis=-1, keepdims=True)
    return (x * lax.rsqrt(ms + RMS_EPS) * g).astype(BF16)


def _params(*sem):
    return pltpu.CompilerParams(dimension_semantics=sem, vmem_limit_bytes=VMEM_LIMIT)


def _mem_kv_kernel(mem_ref, g_ref, wk_ref, wvT_ref, k_ref, vT_ref):
    h = _rms_bf16(mem_ref[0], g_ref[...])
    k = jnp.dot(h, wk_ref[0], preferred_element_type=F32)
    vT = lax.dot_general(wvT_ref[0], h, _NT, preferred_element_type=F32)
    lane = lax.broadcasted_iota(jnp.int32, k.shape, 1)
    for hh in range(N_MEM_HEADS):
        mine = (lane >= hh * HEAD_DIM) & (lane < (hh + 1) * HEAD_DIM)
        k_ref[0, 0, hh] = jnp.where(mine, k, 0.0).astype(BF16)
        vT_ref[0, 0, hh * V_ROWS:hh * V_ROWS + HEAD_DIM] = (
            vT[hh * HEAD_DIM:(hh + 1) * HEAD_DIM].astype(BF16))
        vT_ref[0, 0, hh * V_ROWS + HEAD_DIM:(hh + 1) * V_ROWS] = jnp.ones(
            (ONES_ROWS, vT.shape[1]), BF16)


def _mem_kv(mem, mem_norm_g, w_mem_kv):
    b, m, d = mem.shape
    depth = w_mem_kv.shape[0]
    wk = w_mem_kv[:, :, :MEM_WIDTH].astype(BF16)
    wvT = jnp.swapaxes(w_mem_kv[:, :, MEM_WIDTH:], 1, 2).astype(BF16)
    return pl.pallas_call(
        _mem_kv_kernel,
        grid=(depth, b),
        in_specs=[
            pl.BlockSpec((1, m, d), lambda i, bb: (bb, 0, 0)),
            pl.BlockSpec((1, d), lambda i, bb: (0, 0)),
            pl.BlockSpec((1, d, MEM_WIDTH), lambda i, bb: (i, 0, 0)),
            pl.BlockSpec((1, MEM_WIDTH, d), lambda i, bb: (i, 0, 0)),
        ],
        out_specs=[
            pl.BlockSpec((1, 1, N_MEM_HEADS, m, MEM_WIDTH), lambda i, bb: (i, bb, 0, 0, 0)),
            pl.BlockSpec((1, 1, N_MEM_HEADS * V_ROWS, m), lambda i, bb: (i, bb, 0, 0)),
        ],
        out_shape=[
            jax.ShapeDtypeStruct((depth, b, N_MEM_HEADS, m, MEM_WIDTH), BF16),
            jax.ShapeDtypeStruct((depth, b, N_MEM_HEADS * V_ROWS, m), BF16),
        ],
        compiler_params=_params("arbitrary", "arbitrary"),
        name="mem_kv",
    )(mem, mem_norm_g.reshape(1, d), wk, wvT)


def _moba_projection(h, wT_ref, w_ref, qT_ref, kaug_ref, vT_ref, kmean_ref, qmT_ref, z_ref):
    tm = h.shape[0]
    nb = tm // MOBA_BLOCK
    outT = lax.dot_general(wT_ref[...], h, _NT, preferred_element_type=F32)
    out = jnp.dot(h, w_ref[...], preferred_element_type=F32)
    ones = jnp.ones((ONES_ROWS, MOBA_BLOCK), BF16)
    for p in range(N_PAIRS):
        for bb in range(nb):
            cols = slice(bb * MOBA_BLOCK, (bb + 1) * MOBA_BLOCK)
            qT_ref[0, p, bb] = outT[p * LANES:(p + 1) * LANES, cols].astype(BF16)
            for hh in range(HEAD_PAIR):
                v0 = ATTN_WIDTH + (p * HEAD_PAIR + hh) * HEAD_DIM
                vT_ref[0, p, bb, hh * V_ROWS:hh * V_ROWS + HEAD_DIM] = (
                    outT[v0:v0 + HEAD_DIM, cols].astype(BF16))
                vT_ref[0, p, bb, hh * V_ROWS + HEAD_DIM:(hh + 1) * V_ROWS] = ones
    qmT_ref[0] = outT[2 * ATTN_WIDTH:, :].astype(BF16)
    k = out[:, :ATTN_WIDTH]
    z_ref[0] = out[:, ATTN_WIDTH:].astype(BF16)
    row = lax.broadcasted_iota(jnp.int32, (tm, LANES), 0)
    lane = lax.broadcasted_iota(jnp.int32, (tm, LANES), 1)
    blk = pl.program_id(1) * nb + row // MOBA_BLOCK
    s_rel = (row % MOBA_BLOCK).astype(F32)
    ext = jnp.where(lane < SEL_COLS, (lane == blk).astype(F32),
                    jnp.where(lane < SEL_COLS + ALIBI_COLS, s_rel, 0.0)).astype(BF16)
    for p in range(N_PAIRS):
        kaug_ref[0, p, :, 0:LANES] = k[:, p * LANES:(p + 1) * LANES].astype(BF16)
        kaug_ref[0, p, :, LANES:AUG] = ext
    for bb in range(nb):
        kmean_ref[0, 0, bb:bb + 1, :] = jnp.mean(
            k[bb * MOBA_BLOCK:(bb + 1) * MOBA_BLOCK, :], axis=0, keepdims=True)


def _moba_projection_specs(b, s, tm):
    nblk = s // MOBA_BLOCK
    nb = tm // MOBA_BLOCK
    specs = [
        pl.BlockSpec((1, N_PAIRS, nb, LANES, MOBA_BLOCK), lambda bb, i: (bb, 0, i, 0, 0)),
        pl.BlockSpec((1, N_PAIRS, tm, AUG), lambda bb, i: (bb, 0, i, 0)),
        pl.BlockSpec((1, N_PAIRS, nb, HEAD_PAIR * V_ROWS, MOBA_BLOCK),
                     lambda bb, i: (bb, 0, i, 0, 0)),
        pl.BlockSpec((1, 1, nb, ATTN_WIDTH), lambda bb, i: (bb, i, 0, 0)),
        pl.BlockSpec((1, MEM_WIDTH, tm), lambda bb, i: (bb, 0, i)),
        pl.BlockSpec((1, tm, BRANCH_WIDTH), lambda bb, i: (bb, i, 0)),
    ]
    shapes = [
        jax.ShapeDtypeStruct((b, N_PAIRS, nblk, LANES, MOBA_BLOCK), BF16),
        jax.ShapeDtypeStruct((b, N_PAIRS, s, AUG), BF16),
        jax.ShapeDtypeStruct((b, N_PAIRS, nblk, HEAD_PAIR * V_ROWS, MOBA_BLOCK), BF16),
        jax.ShapeDtypeStruct((b, s // tm, nb, ATTN_WIDTH), F32),
        jax.ShapeDtypeStruct((b, MEM_WIDTH, s), BF16),
        jax.ShapeDtypeStruct((b, s, BRANCH_WIDTH), BF16),
    ]
    return specs, shapes


def _swa_projection(h, wT_ref, w_ref, qT_ref, k_ref, vT_ref, qmT_ref, z_ref):
    outT = lax.dot_general(wT_ref[...], h, _NT, preferred_element_type=F32)
    out = jnp.dot(h, w_ref[...], preferred_element_type=F32)
    tm = h.shape[0]
    qT_ref[0] = outT[:ATTN_WIDTH].astype(BF16)
    for g in range(N_KV_HEADS_B):
        v0 = ATTN_WIDTH + g * HEAD_DIM
        vT_ref[0, g * V_ROWS:g * V_ROWS + HEAD_DIM] = outT[v0:v0 + HEAD_DIM].astype(BF16)
        vT_ref[0, g * V_ROWS + HEAD_DIM:(g + 1) * V_ROWS] = jnp.ones((ONES_ROWS, tm), BF16)
    qmT_ref[0] = outT[ATTN_WIDTH + KV_WIDTH_B:].astype(BF16)
    k_ref[0] = out[:, :KV_WIDTH_B].astype(BF16)
    z_ref[0] = out[:, KV_WIDTH_B:].astype(BF16)


def _swa_projection_specs(b, s, tm):
    specs = [
        pl.BlockSpec((1, ATTN_WIDTH, tm), lambda bb, i: (bb, 0, i)),
        pl.BlockSpec((1, tm, KV_WIDTH_B), lambda bb, i: (bb, i, 0)),
        pl.BlockSpec((1, N_KV_HEADS_B * V_ROWS, tm), lambda bb, i: (bb, 0, i)),
        pl.BlockSpec((1, MEM_WIDTH, tm), lambda bb, i: (bb, 0, i)),
        pl.BlockSpec((1, tm, BRANCH_WIDTH), lambda bb, i: (bb, i, 0)),
    ]
    shapes = [
        jax.ShapeDtypeStruct((b, ATTN_WIDTH, s), BF16),
        jax.ShapeDtypeStruct((b, s, KV_WIDTH_B), BF16),
        jax.ShapeDtypeStruct((b, N_KV_HEADS_B * V_ROWS, s), BF16),
        jax.ShapeDtypeStruct((b, MEM_WIDTH, s), BF16),
        jax.ShapeDtypeStruct((b, s, BRANCH_WIDTH), BF16),
    ]
    return specs, shapes


_PROJECTIONS = {
    "moba": (_moba_projection, _moba_projection_specs),
    "swa": (_swa_projection, _swa_projection_specs),
}


def _split_in_weights(w_in, mixer):
    scale = HEAD_DIM ** -0.5 * LOG2E
    kvw = ATTN_WIDTH if mixer == "moba" else KV_WIDTH_B
    o = np.cumsum([0, ATTN_WIDTH, kvw, kvw, MEM_WIDTH, BRANCH_WIDTH])
    q, k, v, qm, z = (w_in[:, o[i]:o[i + 1]] for i in range(5))
    wT = jnp.concatenate([q * scale, v, qm * scale], axis=1).T.astype(BF16)
    w = jnp.concatenate([k, z], axis=1).astype(BF16)
    return wT, w


def _first_projection_kernel(x_ref, g_ref, wT_ref, w_ref, *out_refs, mixer):
    h = _rms_bf16(x_ref[0], g_ref[...])
    _PROJECTIONS[mixer][0](h, wT_ref, w_ref, *out_refs)


def _first_projection(x, g, w_in, mixer):
    b, s, d = x.shape
    tm = min(ROW_TILE, s)
    wT, w = _split_in_weights(w_in, mixer)
    specs, shapes = _PROJECTIONS[mixer][1](b, s, tm)
    return pl.pallas_call(
        functools.partial(_first_projection_kernel, mixer=mixer),
        grid=(b, s // tm),
        in_specs=[
            pl.BlockSpec((1, tm, d), lambda bb, i: (bb, i, 0)),
            pl.BlockSpec((1, d), lambda bb, i: (0, 0)),
            pl.BlockSpec(wT.shape, lambda bb, i: (0, 0)),
            pl.BlockSpec(w.shape, lambda bb, i: (0, 0)),
        ],
        out_specs=specs,
        out_shape=shapes,
        compiler_params=_params("parallel", "arbitrary"),
        name=f"in_proj_{mixer}",
    )(x, g.reshape(1, d), wT, w)


def _memory_attention_T(qmT, mk_ref, mvT_ref):
    sT = [jnp.dot(mk_ref[0, hh], qmT, preferred_element_type=F32)
          for hh in range(N_MEM_HEADS)]
    outs = []
    for hh in range(N_MEM_HEADS):
        p = jnp.exp2(sT[hh] - jnp.max(sT[hh], axis=0, keepdims=True))
        acc = jnp.dot(mvT_ref[0, hh * V_ROWS:(hh + 1) * V_ROWS, :], p.astype(BF16),
                      preferred_element_type=F32)
        outs.append(acc[:HEAD_DIM] * (1.0 / acc[HEAD_DIM:HEAD_DIM + 1]))
    return jnp.concatenate(outs, axis=0)


def _epilogue_kernel(y_ref, qmT_ref, z_ref, x_ref, mk_ref, mvT_ref, wout_ref, g_ref, *rest, mixer):
    ymem = _memory_attention_T(qmT_ref[0], mk_ref, mvT_ref).T
    y = jnp.concatenate([y_ref[0].astype(F32), ymem], axis=-1)
    z = z_ref[0].astype(F32)
    gated = (y * (z / (1.0 + jnp.exp(-z)))).astype(BF16)
    xn = x_ref[0] + jnp.dot(gated, wout_ref[...], preferred_element_type=F32)
    if mixer is None:
        (o_ref,) = rest
        ms = jnp.mean(xn * xn, axis=-1, keepdims=True)
        o_ref[0] = xn * lax.rsqrt(ms + RMS_EPS) * g_ref[...]
    else:
        wT_ref, w_ref, xo_ref = rest[:3]
        xo_ref[0] = xn
        _PROJECTIONS[mixer][0](_rms_bf16(xn, g_ref[...]), wT_ref, w_ref, *rest[3:])


def _epilogue(y_self, qmT, z, x, mem_k, mem_vT, layer, w_out, g, w_in_next=None, mixer=None):
    b, s, d = x.shape
    tm = min(ROW_TILE, s)
    m = mem_k.shape[3]
    in_specs = [
        pl.BlockSpec((1, tm, ATTN_WIDTH), lambda bb, i: (bb, i, 0)),
        pl.BlockSpec((1, MEM_WIDTH, tm), lambda bb, i: (bb, 0, i)),
        pl.BlockSpec((1, tm, BRANCH_WIDTH), lambda bb, i: (bb, i, 0)),
        pl.BlockSpec((1, tm, d), lambda bb, i: (bb, i, 0)),
        pl.BlockSpec((1, N_MEM_HEADS, m, MEM_WIDTH), lambda bb, i: (bb, 0, 0, 0)),
        pl.BlockSpec((1, N_MEM_HEADS * V_ROWS, m), lambda bb, i: (bb, 0, 0)),
        pl.BlockSpec((BRANCH_WIDTH, d), lambda bb, i: (0, 0)),
        pl.BlockSpec((1, d), lambda bb, i: (0, 0)),
    ]
    args = [y_self, qmT, z, x, mem_k[layer], mem_vT[layer], w_out.astype(BF16), g.reshape(1, d)]
    x_spec = pl.BlockSpec((1, tm, d), lambda bb, i: (bb, i, 0))
    x_shape = jax.ShapeDtypeStruct((b, s, d), F32)
    if mixer is None:
        out_specs, out_shape = [x_spec], [x_shape]
    else:
        wT, w = _split_in_weights(w_in_next, mixer)
        in_specs += [pl.BlockSpec(wT.shape, lambda bb, i: (0, 0)),
                     pl.BlockSpec(w.shape, lambda bb, i: (0, 0))]
        args += [wT, w]
        specs, shapes = _PROJECTIONS[mixer][1](b, s, tm)
        out_specs, out_shape = [x_spec] + specs, [x_shape] + shapes
    return pl.pallas_call(
        functools.partial(_epilogue_kernel, mixer=mixer),
        grid=(b, s // tm),
        in_specs=in_specs,
        out_specs=out_specs,
        out_shape=out_shape,
        compiler_params=_params("parallel", "arbitrary"),
        name=f"epilogue_{mixer}",
    )(*args)


def _moba_kernel(qT_ref, kaug_ref, vT_ref, kmean_ref, srows_ref, slope_ref, o_ref, qaug_ref):
    j = pl.program_id(2)
    nblk = kmean_ref.shape[1]
    tq = qT_ref.shape[-1]
    qT = qT_ref[0, 0, 0]
    kmean = kmean_ref[0]
    drow = lax.broadcasted_iota(jnp.int32, qT.shape, 0)
    klane = lax.broadcasted_iota(jnp.int32, kmean.shape, 1)
    blk = lax.broadcasted_iota(jnp.int32, (nblk, tq), 0)
    krow = lax.broadcasted_iota(jnp.int32, (MOBA_BLOCK, tq), 0)
    qlane = lax.broadcasted_iota(jnp.int32, (MOBA_BLOCK, tq), 1)
    causal = krow <= qlane

    def weighted_values(n, hh, p):
        return jnp.dot(vT_ref[0, 0, n, hh * V_ROWS:(hh + 1) * V_ROWS, :], p.astype(BF16),
                       preferred_element_type=F32)

    q_heads, gates, own_sT = [], [], []
    no_choice = jnp.zeros((SEL_COLS, tq), BF16)
    for hh in range(HEAD_PAIR):
        lo, hi = hh * HEAD_DIM, (hh + 1) * HEAD_DIM
        q_heads.append(jnp.where((drow >= lo) & (drow < hi), qT, jnp.zeros_like(qT)))
        km_h = jnp.where((klane >= lo) & (klane < hi), kmean, 0.0).astype(BF16)
        gates.append(jnp.dot(km_h, qT, preferred_element_type=F32))
        q_own = jnp.concatenate([q_heads[hh], no_choice, srows_ref[0, hh]], axis=0)
        own_sT.append(jnp.dot(kaug_ref[0, 0, j], q_own, preferred_element_type=F32))

    own_max, own_acc = [], []
    for hh in range(HEAD_PAIR):
        sT = jnp.where(causal, own_sT[hh], MASKED)
        m = jnp.max(sT, axis=0, keepdims=True)
        own_max.append(m)
        own_acc.append(weighted_values(j, hh, jnp.exp2(sT - m)))

    for hh in range(HEAD_PAIR):
        gate = jnp.where(blk < j, gates[hh], -jnp.inf)
        bias = jnp.full((nblk, tq), MASKED, F32)
        for _ in range(MOBA_TOPK):
            best = jnp.max(gate, axis=0, keepdims=True)
            first = jnp.min(jnp.where(gate == best, blk, nblk), axis=0, keepdims=True)
            pick = blk == first
            chosen = jnp.where(best > -jnp.inf, 0.0, MASKED)
            bias = jnp.where(pick, jnp.maximum(bias, chosen), bias)
            gate = jnp.where(pick, -jnp.inf, gate)
        bias = bias.astype(BF16)
        if nblk < SEL_COLS:
            bias = jnp.concatenate([bias, jnp.zeros((SEL_COLS - nblk, tq), BF16)], axis=0)
        qaug_ref[hh] = jnp.concatenate([q_heads[hh], bias, srows_ref[0, hh]], axis=0)

    def scores(n, hh):
        return jnp.dot(kaug_ref[0, 0, n], qaug_ref[hh], preferred_element_type=F32)

    def block_offset(n, hh):
        return slope_ref[0, hh] * ((n - j) * MOBA_BLOCK).astype(F32)

    def finish(accs):
        oT = jnp.concatenate([acc[:HEAD_DIM] * (1.0 / acc[HEAD_DIM:HEAD_DIM + 1]) for acc in accs],
                             axis=0)
        o_ref[0] = oT.T.astype(o_ref.dtype)

    def fixed_shift_blocks(first, count, accs):
        slots = [first + u for u in range(count)]
        blocks = [jnp.minimum(n, j) for n in slots]
        sT = [[scores(n, hh) for hh in range(HEAD_PAIR)] for n in blocks]
        accs = list(accs)
        for u, n in enumerate(blocks):
            for hh in range(HEAD_PAIR):
                shift = jnp.where(slots[u] < j, own_max[hh] - block_offset(n, hh), jnp.inf)
                accs[hh] += weighted_values(n, hh, jnp.exp2(sT[u][hh] - shift))
        return accs

    long_trips = j // PAST_UNROLL
    done = long_trips * PAST_UNROLL
    fast = lax.fori_loop(
        0, long_trips, lambda g, a: fixed_shift_blocks(g * PAST_UNROLL, PAST_UNROLL, a), own_acc)
    fast = lax.fori_loop(
        0, (j - done + TAIL_UNROLL - 1) // TAIL_UNROLL,
        lambda g, a: fixed_shift_blocks(done + g * TAIL_UNROLL, TAIL_UNROLL, a), fast)
    finish(fast)
    denominators = jnp.maximum(fast[0][HEAD_DIM:HEAD_DIM + 1], fast[1][HEAD_DIM:HEAD_DIM + 1])
    in_range = jnp.max(denominators) < OVERFLOW_GUARD

    @pl.when(jnp.logical_not(in_range))
    def _():
        def running_max_block(n, carry):
            out = []
            for hh in range(HEAD_PAIR):
                m, acc = carry[2 * hh:2 * hh + 2]
                sT = scores(n, hh)
                c = block_offset(n, hh)
                m_new = jnp.maximum(m, jnp.max(sT, axis=0, keepdims=True) + c)
                acc = jnp.exp2(m - m_new) * acc + weighted_values(n, hh, jnp.exp2(sT - (m_new - c)))
                out += [m_new, acc]
            return out

        slow = lax.fori_loop(0, j, running_max_block,
                             [t for hh in range(HEAD_PAIR) for t in (own_max[hh], own_acc[hh])])
        finish(slow[1::2])


def _moba_tables(tq):
    slopes = _alibi_slopes(N_HEADS).astype(np.float64) * LOG2E
    pieces = _bf16_pieces(slopes)
    rows = np.zeros((N_HEADS, AUG - LANES - SEL_COLS, tq), np.float32)
    for c, piece in enumerate(pieces):
        rows[:, c, :] = piece[:, None]
    srows = jnp.asarray(rows.reshape(N_PAIRS, HEAD_PAIR, AUG - LANES - SEL_COLS, tq), BF16)
    svec = jnp.asarray(np.broadcast_to(slopes[:, None, None], (N_HEADS, 1, tq))
                       .reshape(N_PAIRS, HEAD_PAIR, 1, tq), F32)
    return srows, svec


def _moba_attention(qT, kaug, vT, kmean):
    b, _, nblk, _, tq = qT.shape
    assert nblk <= SEL_COLS and tq == MOBA_BLOCK
    s = nblk * MOBA_BLOCK
    kaug = kaug.reshape(b, N_PAIRS, nblk, MOBA_BLOCK, AUG)
    srows, svec = _moba_tables(tq)
    return pl.pallas_call(
        _moba_kernel,
        grid=(b, N_PAIRS, nblk),
        in_specs=[
            pl.BlockSpec((1, 1, 1, LANES, tq), lambda bb, p, j: (bb, p, j, 0, 0)),
            pl.BlockSpec((1, 1, nblk, MOBA_BLOCK, AUG), lambda bb, p, j: (bb, p, 0, 0, 0)),
            pl.BlockSpec((1, 1, nblk, HEAD_PAIR * V_ROWS, MOBA_BLOCK),
                         lambda bb, p, j: (bb, p, 0, 0, 0)),
            pl.BlockSpec((1, nblk, LANES), lambda bb, p, j: (bb, 0, p)),
            pl.BlockSpec((1, HEAD_PAIR, AUG - LANES - SEL_COLS, tq), lambda bb, p, j: (p, 0, 0, 0)),
            pl.BlockSpec((1, HEAD_PAIR, 1, tq), lambda bb, p, j: (p, 0, 0, 0)),
        ],
        out_specs=pl.BlockSpec((1, tq, LANES), lambda bb, p, j: (bb, j, p)),
        out_shape=jax.ShapeDtypeStruct((b, s, ATTN_WIDTH), BF16),
        scratch_shapes=[pltpu.VMEM((HEAD_PAIR, AUG, tq), BF16)],
        compiler_params=_params("parallel", "parallel", "arbitrary"),
        name="moba_attention",
    )(qT, kaug, vT, kmean, srows, svec)


def _swa_kernel(qT_ref, kp_ref, kc_ref, vTp_ref, vTc_ref, bias_ref, sink_ref, o_ref, yT_ref):
    tq = qT_ref.shape[-1]
    kwin = jnp.concatenate([kp_ref[0], kc_ref[0]], axis=0)
    vTwin = jnp.concatenate([vTp_ref[0], vTc_ref[0]], axis=1)
    zeros = jnp.zeros((HEAD_DIM, tq), BF16)
    group = N_HEADS // N_KV_HEADS_B

    def all_heads(sink_is_shift):
        worst = jnp.zeros((1, tq), F32)
        sT = {}
        for h in range(N_HEADS):
            g = h // group
            if h % SWA_HEADS_AHEAD == 0:
                for h2 in range(h, min(h + SWA_HEADS_AHEAD, N_HEADS)):
                    q_h = qT_ref[0, h2 * HEAD_DIM:(h2 + 1) * HEAD_DIM, :]
                    q_pad = jnp.concatenate([q_h, zeros] if h2 // group == 0 else [zeros, q_h], axis=0)
                    sT[h2] = jnp.dot(kwin, q_pad, preferred_element_type=F32)
            s_h = sT.pop(h) + bias_ref[0, h]
            sink = sink_ref[h] * LOG2E
            m = sink if sink_is_shift else jnp.maximum(jnp.max(s_h, axis=0, keepdims=True), sink)
            acc = jnp.dot(vTwin[g * V_ROWS:(g + 1) * V_ROWS, :], jnp.exp2(s_h - m).astype(BF16),
                          preferred_element_type=F32)
            denom = acc[HEAD_DIM:HEAD_DIM + 1] + jnp.exp2(sink - m)
            worst = jnp.maximum(worst, denom)
            yT_ref[h * HEAD_DIM:(h + 1) * HEAD_DIM, :] = acc[:HEAD_DIM] * (1.0 / denom)
        o_ref[0] = yT_ref[...].T.astype(o_ref.dtype)
        return worst

    in_range = jnp.max(all_heads(sink_is_shift=True)) < OVERFLOW_GUARD

    @pl.when(jnp.logical_not(in_range))
    def _():
        all_heads(sink_is_shift=False)


def _swa_bias_table(tq):
    slopes = _alibi_slopes(N_HEADS).astype(np.float64) * LOG2E
    kw = np.arange(WINDOW + tq)[:, None]
    q = np.arange(tq)[None, :]
    dist = q + WINDOW - kw
    ok = (dist >= 0) & (dist < WINDOW)
    first_tile = ok & (kw >= WINDOW)
    table = np.stack([np.where(valid[None], -slopes[:, None, None] * dist[None], MASKED)
                      for valid in (first_tile, ok)])
    return jnp.asarray(table, F32)


def _swa_attention(qT, k, vT, sinks):
    b, _, s = qT.shape
    tq = min(SWA_TILE, s)
    r = tq // WINDOW
    bias = _swa_bias_table(tq)
    sink_rows = jnp.broadcast_to(sinks.astype(F32)[:, None, None], (N_HEADS, 1, tq))
    prev = lambda j: jnp.maximum(j * r - 1, 0)
    return pl.pallas_call(
        _swa_kernel,
        grid=(b, s // tq),
        in_specs=[
            pl.BlockSpec((1, ATTN_WIDTH, tq), lambda bb, j: (bb, 0, j)),
            pl.BlockSpec((1, WINDOW, KV_WIDTH_B), lambda bb, j: (bb, prev(j), 0)),
            pl.BlockSpec((1, tq, KV_WIDTH_B), lambda bb, j: (bb, j, 0)),
            pl.BlockSpec((1, N_KV_HEADS_B * V_ROWS, WINDOW), lambda bb, j: (bb, 0, prev(j))),
            pl.BlockSpec((1, N_KV_HEADS_B * V_ROWS, tq), lambda bb, j: (bb, 0, j)),
            pl.BlockSpec((1,) + bias.shape[1:], lambda bb, j: (jnp.minimum(j, 1), 0, 0, 0)),
            pl.BlockSpec(sink_rows.shape, lambda bb, j: (0, 0, 0)),
        ],
        out_specs=pl.BlockSpec((1, tq, ATTN_WIDTH), lambda bb, j: (bb, j, 0)),
        out_shape=jax.ShapeDtypeStruct((b, s, ATTN_WIDTH), BF16),
        scratch_shapes=[pltpu.VMEM((ATTN_WIDTH, tq), F32)],
        compiler_params=_params("parallel", "arbitrary"),
        name="swa_attention",
    )(qT, k, k, vT, vT, bias, sink_rows)


def kernel(x, mem, norm_g, w_in_a, w_in_b, sinks_b, w_mem_kv, w_out, mem_norm_g, final_norm_g):
    depth = norm_g.shape[0]
    b, s, _ = x.shape
    assert s % ROW_TILE == 0 or s < ROW_TILE
    mem_k, mem_vT = _mem_kv(mem, mem_norm_g, w_mem_kv)
    mixers = ["moba" if i % 2 == 0 else "swa" for i in range(depth)]
    w_in = [w_in_a[i // 2] if i % 2 == 0 else w_in_b[i // 2] for i in range(depth)]
    proj = _first_projection(x, norm_g[0], w_in[0], mixers[0])
    for i in range(depth):
        if mixers[i] == "moba":
            qT, kaug, vT, kmean, qmT, z = proj
            y_self = _moba_attention(qT, kaug, vT, kmean.reshape(b, s // MOBA_BLOCK, ATTN_WIDTH))
        else:
            qT, k, vT, qmT, z = proj
            y_self = _swa_attention(qT, k, vT, sinks_b[i // 2])
        if i + 1 < depth:
            x, *proj = _epilogue(y_self, qmT, z, x, mem_k, mem_vT, i, w_out[i], norm_g[i + 1],
                                 w_in[i + 1], mixers[i + 1])
        else:
            (x,) = _epilogue(y_self, qmT, z, x, mem_k, mem_vT, i, w_out[i], final_norm_g)
    return x
```

```python
import functools
import math

import jax
import jax.numpy as jnp
import numpy as np
from jax import lax
from jax.experimental import pallas as pl
from jax.experimental.pallas import tpu as pltpu

HEAD_DIM = 64
N_HEADS = 12
N_KV_HEADS_B = 2
N_MEM_HEADS = 4
ATTN_WIDTH = N_HEADS * HEAD_DIM
MEM_WIDTH = N_MEM_HEADS * HEAD_DIM
BRANCH_WIDTH = ATTN_WIDTH + MEM_WIDTH
KV_WIDTH_B = N_KV_HEADS_B * HEAD_DIM
MOBA_BLOCK = 256
MOBA_TOPK = 3
WINDOW = 128
RMS_EPS = 1e-6

LANES = 128
HEAD_PAIR = LANES // HEAD_DIM
N_PAIRS = N_HEADS // HEAD_PAIR
SEL_COLS = 64
ALIBI_COLS = 4
ONES_ROWS = 16
V_ROWS = HEAD_DIM + ONES_ROWS
LOG2E = math.log2(math.e)
AUG = 2 * LANES
MASKED = -(2.0 ** 30)
OVERFLOW_GUARD = 2.0 ** 100
EXP2_IS_ZERO_BELOW = -150.0
BOUND_SLACK = 1.0
BF16_ROUND_UP = 1.0 + 2.0 ** -7
PAST_UNROLL = 16
TAIL_UNROLL = 4
ROW_TILE = 512
SWA_TILE = 256
VMEM_LIMIT = 56 * 1024 * 1024

F32 = jnp.float32
BF16 = jnp.bfloat16
_NT = (((1,), (1,)), ((), ()))


def _alibi_slopes(n_heads):
    def pow2_slopes(n):
        start = 2.0 ** (-8.0 / n)
        return [start ** (i + 1) for i in range(n)]
    if math.log2(n_heads).is_integer():
        vals = pow2_slopes(n_heads)
    else:
        c = 2 ** math.floor(math.log2(n_heads))
        vals = pow2_slopes(c) + pow2_slopes(2 * c)[0::2][: n_heads - c]
    return np.array(vals, dtype=np.float32)


def _bf16_pieces(v):
    rest = np.asarray(v, np.float64)
    pieces = []
    for _ in range(ALIBI_COLS):
        p = rest.astype(BF16).astype(np.float64)
        pieces.append(p.astype(np.float32))
        rest = rest - p
    assert np.all(np.abs(rest) <= np.abs(v) * 2.0 ** -30)
    return pieces


def _rms_bf16(x, g):
    ms = jnp.mean(x * x, axis=-1, keepdims=True)
    return (x * lax.rsqrt(ms + RMS_EPS) * g).astype(BF16)


def _params(*sem):
    return pltpu.CompilerParams(dimension_semantics=sem, vmem_limit_bytes=VMEM_LIMIT)


def _mem_kv_kernel(mem_ref, g_ref, wk_ref, wvT_ref, k_ref, vT_ref):
    h = _rms_bf16(mem_ref[0], g_ref[...])
    k = jnp.dot(h, wk_ref[0], preferred_element_type=F32)
    vT = lax.dot_general(wvT_ref[0], h, _NT, preferred_element_type=F32)
    lane = lax.broadcasted_iota(jnp.int32, k.shape, 1)
    for hh in range(N_MEM_HEADS):
        mine = (lane >= hh * HEAD_DIM) & (lane < (hh + 1) * HEAD_DIM)
        k_ref[0, 0, hh] = jnp.where(mine, k, 0.0).astype(BF16)
        vT_ref[0, 0, hh * V_ROWS:hh * V_ROWS + HEAD_DIM] = (
            vT[hh * HEAD_DIM:(hh + 1) * HEAD_DIM].astype(BF16))
        vT_ref[0, 0, hh * V_ROWS + HEAD_DIM:(hh + 1) * V_ROWS] = jnp.ones(
            (ONES_ROWS, vT.shape[1]), BF16)


def _mem_kv(mem, mem_norm_g, w_mem_kv):
    b, m, d = mem.shape
    depth = w_mem_kv.shape[0]
    wk = w_mem_kv[:, :, :MEM_WIDTH].astype(BF16)
    wvT = jnp.swapaxes(w_mem_kv[:, :, MEM_WIDTH:], 1, 2).astype(BF16)
    return pl.pallas_call(
        _mem_kv_kernel,
        grid=(depth, b),
        in_specs=[
            pl.BlockSpec((1, m, d), lambda i, bb: (bb, 0, 0)),
            pl.BlockSpec((1, d), lambda i, bb: (0, 0)),
            pl.BlockSpec((1, d, MEM_WIDTH), lambda i, bb: (i, 0, 0)),
            pl.BlockSpec((1, MEM_WIDTH, d), lambda i, bb: (i, 0, 0)),
        ],
        out_specs=[
            pl.BlockSpec((1, 1, N_MEM_HEADS, m, MEM_WIDTH), lambda i, bb: (i, bb, 0, 0, 0)),
            pl.BlockSpec((1, 1, N_MEM_HEADS * V_ROWS, m), lambda i, bb: (i, bb, 0, 0)),
        ],
        out_shape=[
            jax.ShapeDtypeStruct((depth, b, N_MEM_HEADS, m, MEM_WIDTH), BF16),
            jax.ShapeDtypeStruct((depth, b, N_MEM_HEADS * V_ROWS, m), BF16),
        ],
        compiler_params=_params("arbitrary", "arbitrary"),
        name="mem_kv",
    )(mem, mem_norm_g.reshape(1, d), wk, wvT)


def _moba_projection(h, wT_ref, w_ref, qT_ref, kaug_ref, vT_ref, kstat_ref, qmT_ref, z_ref):
    tm = h.shape[0]
    nb = tm // MOBA_BLOCK
    outT = lax.dot_general(wT_ref[...], h, _NT, preferred_element_type=F32)
    out = jnp.dot(h, w_ref[...], preferred_element_type=F32)
    ones = jnp.ones((ONES_ROWS, MOBA_BLOCK), BF16)
    for p in range(N_PAIRS):
        for bb in range(nb):
            cols = slice(bb * MOBA_BLOCK, (bb + 1) * MOBA_BLOCK)
            qT_ref[0, p, bb] = outT[p * LANES:(p + 1) * LANES, cols].astype(BF16)
            for hh in range(HEAD_PAIR):
                v0 = ATTN_WIDTH + (p * HEAD_PAIR + hh) * HEAD_DIM
                vT_ref[0, p, bb, hh * V_ROWS:hh * V_ROWS + HEAD_DIM] = (
                    outT[v0:v0 + HEAD_DIM, cols].astype(BF16))
                vT_ref[0, p, bb, hh * V_ROWS + HEAD_DIM:(hh + 1) * V_ROWS] = ones
    qmT_ref[0] = outT[2 * ATTN_WIDTH:, :].astype(BF16)
    k = out[:, :ATTN_WIDTH]
    z_ref[0] = out[:, ATTN_WIDTH:].astype(BF16)
    row = lax.broadcasted_iota(jnp.int32, (tm, LANES), 0)
    lane = lax.broadcasted_iota(jnp.int32, (tm, LANES), 1)
    blk = pl.program_id(1) * nb + row // MOBA_BLOCK
    s_rel = (row % MOBA_BLOCK).astype(F32)
    ext = jnp.where(lane < SEL_COLS, (lane == blk).astype(F32),
                    jnp.where(lane < SEL_COLS + ALIBI_COLS, s_rel, 0.0)).astype(BF16)
    k_bf16 = k.astype(BF16)
    for p in range(N_PAIRS):
        kaug_ref[0, p, :, 0:LANES] = k_bf16[:, p * LANES:(p + 1) * LANES]
        kaug_ref[0, p, :, LANES:AUG] = ext
    k_abs = jnp.abs(k_bf16.astype(F32))
    for bb in range(nb):
        rows = slice(bb * MOBA_BLOCK, (bb + 1) * MOBA_BLOCK)
        kstat_ref[0, 0, 0, bb:bb + 1, :] = jnp.mean(k[rows], axis=0, keepdims=True)
        kstat_ref[0, 0, 1, bb:bb + 1, :] = jnp.max(k_abs[rows], axis=0, keepdims=True)


def _moba_projection_specs(b, s, tm):
    nblk = s // MOBA_BLOCK
    nb = tm // MOBA_BLOCK
    specs = [
        pl.BlockSpec((1, N_PAIRS, nb, LANES, MOBA_BLOCK), lambda bb, i: (bb, 0, i, 0, 0)),
        pl.BlockSpec((1, N_PAIRS, tm, AUG), lambda bb, i: (bb, 0, i, 0)),
        pl.BlockSpec((1, N_PAIRS, nb, HEAD_PAIR * V_ROWS, MOBA_BLOCK),
                     lambda bb, i: (bb, 0, i, 0, 0)),
        pl.BlockSpec((1, 1, 2, nb, ATTN_WIDTH), lambda bb, i: (bb, i, 0, 0, 0)),
        pl.BlockSpec((1, MEM_WIDTH, tm), lambda bb, i: (bb, 0, i)),
        pl.BlockSpec((1, tm, BRANCH_WIDTH), lambda bb, i: (bb, i, 0)),
    ]
    shapes = [
        jax.ShapeDtypeStruct((b, N_PAIRS, nblk, LANES, MOBA_BLOCK), BF16),
        jax.ShapeDtypeStruct((b, N_PAIRS, s, AUG), BF16),
        jax.ShapeDtypeStruct((b, N_PAIRS, nblk, HEAD_PAIR * V_ROWS, MOBA_BLOCK), BF16),
        jax.ShapeDtypeStruct((b, s // tm, 2, nb, ATTN_WIDTH), F32),
        jax.ShapeDtypeStruct((b, MEM_WIDTH, s), BF16),
        jax.ShapeDtypeStruct((b, s, BRANCH_WIDTH), BF16),
    ]
    return specs, shapes


def _swa_projection(h, wT_ref, w_ref, qT_ref, k_ref, vT_ref, qmT_ref, z_ref):
    outT = lax.dot_general(wT_ref[...], h, _NT, preferred_element_type=F32)
    out = jnp.dot(h, w_ref[...], preferred_element_type=F32)
    tm = h.shape[0]
    qT_ref[0] = outT[:ATTN_WIDTH].astype(BF16)
    for g in range(N_KV_HEADS_B):
        v0 = ATTN_WIDTH + g * HEAD_DIM
        vT_ref[0, g * V_ROWS:g * V_ROWS + HEAD_DIM] = outT[v0:v0 + HEAD_DIM].astype(BF16)
        vT_ref[0, g * V_ROWS + HEAD_DIM:(g + 1) * V_ROWS] = jnp.ones((ONES_ROWS, tm), BF16)
    qmT_ref[0] = outT[ATTN_WIDTH + KV_WIDTH_B:].astype(BF16)
    k_ref[0] = out[:, :KV_WIDTH_B].astype(BF16)
    z_ref[0] = out[:, KV_WIDTH_B:].astype(BF16)


def _swa_projection_specs(b, s, tm):
    specs = [
        pl.BlockSpec((1, ATTN_WIDTH, tm), lambda bb, i: (bb, 0, i)),
        pl.BlockSpec((1, tm, KV_WIDTH_B), lambda bb, i: (bb, i, 0)),
        pl.BlockSpec((1, N_KV_HEADS_B * V_ROWS, tm), lambda bb, i: (bb, 0, i)),
        pl.BlockSpec((1, MEM_WIDTH, tm), lambda bb, i: (bb, 0, i)),
        pl.BlockSpec((1, tm, BRANCH_WIDTH), lambda bb, i: (bb, i, 0)),
    ]
    shapes = [
        jax.ShapeDtypeStruct((b, ATTN_WIDTH, s), BF16),
        jax.ShapeDtypeStruct((b, s, KV_WIDTH_B), BF16),
        jax.ShapeDtypeStruct((b, N_KV_HEADS_B * V_ROWS, s), BF16),
        jax.ShapeDtypeStruct((b, MEM_WIDTH, s), BF16),
        jax.ShapeDtypeStruct((b, s, BRANCH_WIDTH), BF16),
    ]
    return specs, shapes


_PROJECTIONS = {
    "moba": (_moba_projection, _moba_projection_specs),
    "swa": (_swa_projection, _swa_projection_specs),
}


def _split_in_weights(w_in, mixer):
    scale = HEAD_DIM ** -0.5 * LOG2E
    kvw = ATTN_WIDTH if mixer == "moba" else KV_WIDTH_B
    o = np.cumsum([0, ATTN_WIDTH, kvw, kvw, MEM_WIDTH, BRANCH_WIDTH])
    q, k, v, qm, z = (w_in[:, o[i]:o[i + 1]] for i in range(5))
    wT = jnp.concatenate([q * scale, v, qm * scale], axis=1).T.astype(BF16)
    w = jnp.concatenate([k, z], axis=1).astype(BF16)
    return wT, w


def _first_projection_kernel(x_ref, g_ref, wT_ref, w_ref, *out_refs, mixer):
    h = _rms_bf16(x_ref[0], g_ref[...])
    _PROJECTIONS[mixer][0](h, wT_ref, w_ref, *out_refs)


def _first_projection(x, g, w_in, mixer):
    b, s, d = x.shape
    tm = min(ROW_TILE, s)
    wT, w = _split_in_weights(w_in, mixer)
    specs, shapes = _PROJECTIONS[mixer][1](b, s, tm)
    return pl.pallas_call(
        functools.partial(_first_projection_kernel, mixer=mixer),
        grid=(b, s // tm),
        in_specs=[
            pl.BlockSpec((1, tm, d), lambda bb, i: (bb, i, 0)),
            pl.BlockSpec((1, d), lambda bb, i: (0, 0)),
            pl.BlockSpec(wT.shape, lambda bb, i: (0, 0)),
            pl.BlockSpec(w.shape, lambda bb, i: (0, 0)),
        ],
        out_specs=specs,
        out_shape=shapes,
        compiler_params=_params("parallel", "arbitrary"),
        name=f"in_proj_{mixer}",
    )(x, g.reshape(1, d), wT, w)


def _memory_attention_T(qmT, mk_ref, mvT_ref):
    sT = [jnp.dot(mk_ref[0, hh], qmT, preferred_element_type=F32)
          for hh in range(N_MEM_HEADS)]
    outs = []
    for hh in range(N_MEM_HEADS):
        p = jnp.exp2(sT[hh] - jnp.max(sT[hh], axis=0, keepdims=True))
        acc = jnp.dot(mvT_ref[0, hh * V_ROWS:(hh + 1) * V_ROWS, :], p.astype(BF16),
                      preferred_element_type=F32)
        outs.append(acc[:HEAD_DIM] * (1.0 / acc[HEAD_DIM:HEAD_DIM + 1]))
    return jnp.concatenate(outs, axis=0)


def _epilogue_kernel(y_ref, qmT_ref, z_ref, x_ref, mk_ref, mvT_ref, wout_ref, g_ref, *rest, mixer):
    ymem = _memory_attention_T(qmT_ref[0], mk_ref, mvT_ref).T
    y = jnp.concatenate([y_ref[0].astype(F32), ymem], axis=-1)
    z = z_ref[0].astype(F32)
    gated = (y * (z / (1.0 + jnp.exp(-z)))).astype(BF16)
    xn = x_ref[0] + jnp.dot(gated, wout_ref[...], preferred_element_type=F32)
    if mixer is None:
        (o_ref,) = rest
        ms = jnp.mean(xn * xn, axis=-1, keepdims=True)
        o_ref[0] = xn * lax.rsqrt(ms + RMS_EPS) * g_ref[...]
    else:
        wT_ref, w_ref, xo_ref = rest[:3]
        xo_ref[0] = xn
        _PROJECTIONS[mixer][0](_rms_bf16(xn, g_ref[...]), wT_ref, w_ref, *rest[3:])


def _epilogue(y_self, qmT, z, x, mem_k, mem_vT, layer, w_out, g, w_in_next=None, mixer=None):
    b, s, d = x.shape
    tm = min(ROW_TILE, s)
    m = mem_k.shape[3]
    in_specs = [
        pl.BlockSpec((1, tm, ATTN_WIDTH), lambda bb, i: (bb, i, 0)),
        pl.BlockSpec((1, MEM_WIDTH, tm), lambda bb, i: (bb, 0, i)),
        pl.BlockSpec((1, tm, BRANCH_WIDTH), lambda bb, i: (bb, i, 0)),
        pl.BlockSpec((1, tm, d), lambda bb, i: (bb, i, 0)),
        pl.BlockSpec((1, N_MEM_HEADS, m, MEM_WIDTH), lambda bb, i: (bb, 0, 0, 0)),
        pl.BlockSpec((1, N_MEM_HEADS * V_ROWS, m), lambda bb, i: (bb, 0, 0)),
        pl.BlockSpec((BRANCH_WIDTH, d), lambda bb, i: (0, 0)),
        pl.BlockSpec((1, d), lambda bb, i: (0, 0)),
    ]
    args = [y_self, qmT, z, x, mem_k[layer], mem_vT[layer], w_out.astype(BF16), g.reshape(1, d)]
    x_spec = pl.BlockSpec((1, tm, d), lambda bb, i: (bb, i, 0))
    x_shape = jax.ShapeDtypeStruct((b, s, d), F32)
    if mixer is None:
        out_specs, out_shape = [x_spec], [x_shape]
    else:
        wT, w = _split_in_weights(w_in_next, mixer)
        in_specs += [pl.BlockSpec(wT.shape, lambda bb, i: (0, 0)),
                     pl.BlockSpec(w.shape, lambda bb, i: (0, 0))]
        args += [wT, w]
        specs, shapes = _PROJECTIONS[mixer][1](b, s, tm)
        out_specs, out_shape = [x_spec] + specs, [x_shape] + shapes
    return pl.pallas_call(
        functools.partial(_epilogue_kernel, mixer=mixer),
        grid=(b, s // tm),
        in_specs=in_specs,
        out_specs=out_specs,
        out_shape=out_shape,
        compiler_params=_params("parallel", "arbitrary"),
        name=f"epilogue_{mixer}",
    )(*args)


def _moba_kernel(qT_ref, kaug_ref, vT_ref, kmean_ref, kabs_ref, srows_ref, slope_ref, o_ref,
                 qaug_ref):
    j = pl.program_id(2)
    nblk = kmean_ref.shape[1]
    tq = qT_ref.shape[-1]
    qT = qT_ref[0, 0, 0]
    kmean = kmean_ref[0]
    drow = lax.broadcasted_iota(jnp.int32, qT.shape, 0)
    klane = lax.broadcasted_iota(jnp.int32, kmean.shape, 1)
    blk = lax.broadcasted_iota(jnp.int32, (nblk, tq), 0)
    krow = lax.broadcasted_iota(jnp.int32, (MOBA_BLOCK, tq), 0)
    qlane = lax.broadcasted_iota(jnp.int32, (MOBA_BLOCK, tq), 1)
    causal = krow <= qlane

    def weighted_values(n, hh, p):
        return jnp.dot(vT_ref[0, 0, n, hh * V_ROWS:(hh + 1) * V_ROWS, :], p.astype(BF16),
                       preferred_element_type=F32)

    q_heads, gates, own_sT, qk_bound = [], [], [], []
    no_choice = jnp.zeros((SEL_COLS, tq), BF16)
    q_abs = jnp.abs(qT)
    for hh in range(HEAD_PAIR):
        lo, hi = hh * HEAD_DIM, (hh + 1) * HEAD_DIM
        mine = (klane >= lo) & (klane < hi)
        q_heads.append(jnp.where((drow >= lo) & (drow < hi), qT, jnp.zeros_like(qT)))
        km_h = jnp.where(mine, kmean, 0.0).astype(BF16)
        gates.append(jnp.dot(km_h, qT, preferred_element_type=F32))
        q_own = jnp.concatenate([q_heads[hh], no_choice, srows_ref[0, hh]], axis=0)
        own_sT.append(jnp.dot(kaug_ref[0, 0, j], q_own, preferred_element_type=F32))
        ka_h = jnp.where(mine, kabs_ref[0] * BF16_ROUND_UP, 0.0).astype(BF16)
        qk_bound.append(jnp.dot(ka_h, q_abs, preferred_element_type=F32))

    own_max, own_acc = [], []
    for hh in range(HEAD_PAIR):
        sT = jnp.where(causal, own_sT[hh], MASKED)
        m = jnp.max(sT, axis=0, keepdims=True)
        own_max.append(m)
        own_acc.append(weighted_values(j, hh, jnp.exp2(sT - m)))

    blocks_back = (j - blk).astype(F32)
    needed = blk < 0
    for hh in range(HEAD_PAIR):
        alibi = slope_ref[0, hh] * ((MOBA_BLOCK - 1) - MOBA_BLOCK * blocks_back)
        log2_weight_bound = qk_bound[hh] - own_max[hh] + alibi + BOUND_SLACK
        needed = needed | ((log2_weight_bound >= EXP2_IS_ZERO_BELOW) & (blk < j))
    first_needed = jnp.min(jnp.where(needed, blk, j).astype(F32)).astype(jnp.int32)

    for hh in range(HEAD_PAIR):
        gate = jnp.where(blk < j, gates[hh], -jnp.inf)
        bias = jnp.full((nblk, tq), MASKED, F32)
        for _ in range(MOBA_TOPK):
            best = jnp.max(gate, axis=0, keepdims=True)
            first = jnp.min(jnp.where(gate == best, blk, nblk), axis=0, keepdims=True)
            pick = blk == first
            chosen = jnp.where(best > -jnp.inf, 0.0, MASKED)
            bias = jnp.where(pick, jnp.maximum(bias, chosen), bias)
            gate = jnp.where(pick, -jnp.inf, gate)
        bias = bias.astype(BF16)
        if nblk < SEL_COLS:
            bias = jnp.concatenate([bias, jnp.zeros((SEL_COLS - nblk, tq), BF16)], axis=0)
        qaug_ref[hh] = jnp.concatenate([q_heads[hh], bias, srows_ref[0, hh]], axis=0)

    def scores(n, hh):
        return jnp.dot(kaug_ref[0, 0, n], qaug_ref[hh], preferred_element_type=F32)

    def block_offset(n, hh):
        return slope_ref[0, hh] * ((n - j) * MOBA_BLOCK).astype(F32)

    def finish(accs):
        oT = jnp.concatenate([acc[:HEAD_DIM] * (1.0 / acc[HEAD_DIM:HEAD_DIM + 1]) for acc in accs],
                             axis=0)
        o_ref[0] = oT.T.astype(o_ref.dtype)

    def fixed_shift_blocks(first, count, accs):
        slots = [first + u for u in range(count)]
        blocks = [jnp.minimum(n, j) for n in slots]
        sT = [[scores(n, hh) for hh in range(HEAD_PAIR)] for n in blocks]
        accs = list(accs)
        for u, n in enumerate(blocks):
            for hh in range(HEAD_PAIR):
                shift = jnp.where(slots[u] < j, own_max[hh] - block_offset(n, hh), jnp.inf)
                accs[hh] += weighted_values(n, hh, jnp.exp2(sT[u][hh] - shift))
        return accs

    long_trips = (j - first_needed) // PAST_UNROLL
    done = first_needed + long_trips * PAST_UNROLL
    fast = lax.fori_loop(
        0, long_trips,
        lambda g, a: fixed_shift_blocks(first_needed + g * PAST_UNROLL, PAST_UNROLL, a), own_acc)
    fast = lax.fori_loop(
        0, (j - done + TAIL_UNROLL - 1) // TAIL_UNROLL,
        lambda g, a: fixed_shift_blocks(done + g * TAIL_UNROLL, TAIL_UNROLL, a), fast)
    finish(fast)
    denominators = jnp.maximum(fast[0][HEAD_DIM:HEAD_DIM + 1], fast[1][HEAD_DIM:HEAD_DIM + 1])
    in_range = jnp.max(denominators) < OVERFLOW_GUARD

    @pl.when(jnp.logical_not(in_range))
    def _():
        def running_max_block(n, carry):
            out = []
            for hh in range(HEAD_PAIR):
                m, acc = carry[2 * hh:2 * hh + 2]
                sT = scores(n, hh)
                c = block_offset(n, hh)
                m_new = jnp.maximum(m, jnp.max(sT, axis=0, keepdims=True) + c)
                acc = jnp.exp2(m - m_new) * acc + weighted_values(n, hh, jnp.exp2(sT - (m_new - c)))
                out += [m_new, acc]
            return out

        slow = lax.fori_loop(0, j, running_max_block,
                             [t for hh in range(HEAD_PAIR) for t in (own_max[hh], own_acc[hh])])
        finish(slow[1::2])


def _moba_tables(tq):
    slopes = _alibi_slopes(N_HEADS).astype(np.float64) * LOG2E
    pieces = _bf16_pieces(slopes)
    rows = np.zeros((N_HEADS, AUG - LANES - SEL_COLS, tq), np.float32)
    for c, piece in enumerate(pieces):
        rows[:, c, :] = piece[:, None]
    srows = jnp.asarray(rows.reshape(N_PAIRS, HEAD_PAIR, AUG - LANES - SEL_COLS, tq), BF16)
    svec = jnp.asarray(np.broadcast_to(slopes[:, None, None], (N_HEADS, 1, tq))
                       .reshape(N_PAIRS, HEAD_PAIR, 1, tq), F32)
    return srows, svec


def _moba_attention(qT, kaug, vT, kstat):
    b, _, nblk, _, tq = qT.shape
    assert nblk <= SEL_COLS and tq == MOBA_BLOCK
    s = nblk * MOBA_BLOCK
    kaug = kaug.reshape(b, N_PAIRS, nblk, MOBA_BLOCK, AUG)
    kmean = kstat[:, :, 0].reshape(b, nblk, ATTN_WIDTH)
    kabs = kstat[:, :, 1].reshape(b, nblk, ATTN_WIDTH)
    srows, svec = _moba_tables(tq)
    return pl.pallas_call(
        _moba_kernel,
        grid=(b, N_PAIRS, nblk),
        in_specs=[
            pl.BlockSpec((1, 1, 1, LANES, tq), lambda bb, p, j: (bb, p, j, 0, 0)),
            pl.BlockSpec((1, 1, nblk, MOBA_BLOCK, AUG), lambda bb, p, j: (bb, p, 0, 0, 0)),
            pl.BlockSpec((1, 1, nblk, HEAD_PAIR * V_ROWS, MOBA_BLOCK),
                         lambda bb, p, j: (bb, p, 0, 0, 0)),
            pl.BlockSpec((1, nblk, LANES), lambda bb, p, j: (bb, 0, p)),
            pl.BlockSpec((1, nblk, LANES), lambda bb, p, j: (bb, 0, p)),
            pl.BlockSpec((1, HEAD_PAIR, AUG - LANES - SEL_COLS, tq), lambda bb, p, j: (p, 0, 0, 0)),
            pl.BlockSpec((1, HEAD_PAIR, 1, tq), lambda bb, p, j: (p, 0, 0, 0)),
        ],
        out_specs=pl.BlockSpec((1, tq, LANES), lambda bb, p, j: (bb, j, p)),
        out_shape=jax.ShapeDtypeStruct((b, s, ATTN_WIDTH), BF16),
        scratch_shapes=[pltpu.VMEM((HEAD_PAIR, AUG, tq), BF16)],
        compiler_params=_params("parallel", "parallel", "arbitrary"),
        name="moba_attention",
    )(qT, kaug, vT, kmean, kabs, srows, svec)


def _swa_kernel(qT_ref, kp_ref, kc_ref, vTp_ref, vTc_ref, bias_ref, sink_ref, o_ref, yT_ref):
    tq = qT_ref.shape[-1]
    kwin = jnp.concatenate([kp_ref[0], kc_ref[0]], axis=0)
    vTwin = jnp.concatenate([vTp_ref[0], vTc_ref[0]], axis=1)
    zeros = jnp.zeros((HEAD_DIM, tq), BF16)
    group = N_HEADS // N_KV_HEADS_B

    def all_heads(sink_is_shift):
        worst = jnp.zeros((1, tq), F32)
        for h in range(N_HEADS):
            g = h // group
            q_h = qT_ref[0, h * HEAD_DIM:(h + 1) * HEAD_DIM, :]
            q_pad = jnp.concatenate([q_h, zeros] if g == 0 else [zeros, q_h], axis=0)
            s_h = jnp.dot(kwin, q_pad, preferred_element_type=F32) + bias_ref[0, h]
            sink = sink_ref[h] * LOG2E
            m = sink if sink_is_shift else jnp.maximum(jnp.max(s_h, axis=0, keepdims=True), sink)
            acc = jnp.dot(vTwin[g * V_ROWS:(g + 1) * V_ROWS, :], jnp.exp2(s_h - m).astype(BF16),
                          preferred_element_type=F32)
            denom = acc[HEAD_DIM:HEAD_DIM + 1] + jnp.exp2(sink - m)
            worst = jnp.maximum(worst, denom)
            yT_ref[h * HEAD_DIM:(h + 1) * HEAD_DIM, :] = acc[:HEAD_DIM] * (1.0 / denom)
        o_ref[0] = yT_ref[...].T.astype(o_ref.dtype)
        return worst

    in_range = jnp.max(all_heads(sink_is_shift=True)) < OVERFLOW_GUARD

    @pl.when(jnp.logical_not(in_range))
    def _():
        all_heads(sink_is_shift=False)


def _swa_bias_table(tq):
    slopes = _alibi_slopes(N_HEADS).astype(np.float64) * LOG2E
    kw = np.arange(WINDOW + tq)[:, None]
    q = np.arange(tq)[None, :]
    dist = q + WINDOW - kw
    ok = (dist >= 0) & (dist < WINDOW)
    first_tile = ok & (kw >= WINDOW)
    table = np.stack([np.where(valid[None], -slopes[:, None, None] * dist[None], MASKED)
                      for valid in (first_tile, ok)])
    return jnp.asarray(table, F32)


def _swa_attention(qT, k, vT, sinks):
    b, _, s = qT.shape
    tq = min(SWA_TILE, s)
    r = tq // WINDOW
    bias = _swa_bias_table(tq)
    sink_rows = jnp.broadcast_to(sinks.astype(F32)[:, None, None], (N_HEADS, 1, tq))
    prev = lambda j: jnp.maximum(j * r - 1, 0)
    return pl.pallas_call(
        _swa_kernel,
        grid=(b, s // tq),
        in_specs=[
            pl.BlockSpec((1, ATTN_WIDTH, tq), lambda bb, j: (bb, 0, j)),
            pl.BlockSpec((1, WINDOW, KV_WIDTH_B), lambda bb, j: (bb, prev(j), 0)),
            pl.BlockSpec((1, tq, KV_WIDTH_B), lambda bb, j: (bb, j, 0)),
            pl.BlockSpec((1, N_KV_HEADS_B * V_ROWS, WINDOW), lambda bb, j: (bb, 0, prev(j))),
            pl.BlockSpec((1, N_KV_HEADS_B * V_ROWS, tq), lambda bb, j: (bb, 0, j)),
            pl.BlockSpec((1,) + bias.shape[1:], lambda bb, j: (jnp.minimum(j, 1), 0, 0, 0)),
            pl.BlockSpec(sink_rows.shape, lambda bb, j: (0, 0, 0)),
        ],
        out_specs=pl.BlockSpec((1, tq, ATTN_WIDTH), lambda bb, j: (bb, j, 0)),
        out_shape=jax.ShapeDtypeStruct((b, s, ATTN_WIDTH), BF16),
        scratch_shapes=[pltpu.VMEM((ATTN_WIDTH, tq), F32)],
        compiler_params=_params("parallel", "arbitrary"),
        name="swa_attention",
    )(qT, k, k, vT, vT, bias, sink_rows)


def kernel(x, mem, norm_g, w_in_a, w_in_b, sinks_b, w_mem_kv, w_out, mem_norm_g, final_norm_g):
    depth = norm_g.shape[0]
    b, s, _ = x.shape
    assert s % ROW_TILE == 0 or s < ROW_TILE
    mem_k, mem_vT = _mem_kv(mem, mem_norm_g, w_mem_kv)
    mixers = ["moba" if i % 2 == 0 else "swa" for i in range(depth)]
    w_in = [w_in_a[i // 2] if i % 2 == 0 else w_in_b[i // 2] for i in range(depth)]
    proj = _first_projection(x, norm_g[0], w_in[0], mixers[0])
    for i in range(depth):
        if mixers[i] == "moba":
            qT, kaug, vT, kstat, qmT, z = proj
            y_self = _moba_attention(qT, kaug, vT, kstat)
        else:
            qT, k, vT, qmT, z = proj
            y_self = _swa_attention(qT, k, vT, sinks_b[i // 2])
        if i + 1 < depth:
            x, *proj = _epilogue(y_self, qmT, z, x, mem_k, mem_vT, i, w_out[i], norm_g[i + 1],
                                 w_in[i + 1], mixers[i + 1])
        else:
            (x,) = _epilogue(y_self, qmT, z, x, mem_k, mem_vT, i, w_out[i], final_norm_g)
    return x
```

```python
import functools
import math

import jax
import jax.numpy as jnp
import numpy as np
from jax import lax
from jax.experimental import pallas as pl
from jax.experimental.pallas import tpu as pltpu

HEAD_DIM = 64
N_HEADS = 12
N_KV_HEADS_B = 2
N_MEM_HEADS = 4
ATTN_WIDTH = N_HEADS * HEAD_DIM
MEM_WIDTH = N_MEM_HEADS * HEAD_DIM
BRANCH_WIDTH = ATTN_WIDTH + MEM_WIDTH
KV_WIDTH_B = N_KV_HEADS_B * HEAD_DIM
MOBA_BLOCK = 256
MOBA_TOPK = 3
WINDOW = 128
RMS_EPS = 1e-6

LANES = 128
HEAD_PAIR = LANES // HEAD_DIM
N_PAIRS = N_HEADS // HEAD_PAIR
SEL_COLS = 64
ALIBI_COLS = 4
ONES_ROWS = 16
V_ROWS = HEAD_DIM + ONES_ROWS
LOG2E = math.log2(math.e)
AUG = 2 * LANES
MASKED = -(2.0 ** 30)
OVERFLOW_GUARD = 2.0 ** 100
EXP2_IS_ZERO_BELOW = -150.0
BOUND_SLACK = 1.0
BF16_ROUND_UP = 1.0 + 2.0 ** -7
TRIP_SIZES = (16, 4, 2, 1)
MASKED_TRIP = 4
ROW_TILE = 512
SWA_TILE = 256
VMEM_LIMIT = 56 * 1024 * 1024

F32 = jnp.float32
BF16 = jnp.bfloat16
_NT = (((1,), (1,)), ((), ()))


def _alibi_slopes(n_heads):
    def pow2_slopes(n):
        start = 2.0 ** (-8.0 / n)
        return [start ** (i + 1) for i in range(n)]
    if math.log2(n_heads).is_integer():
        vals = pow2_slopes(n_heads)
    else:
        c = 2 ** math.floor(math.log2(n_heads))
        vals = pow2_slopes(c) + pow2_slopes(2 * c)[0::2][: n_heads - c]
    return np.array(vals, dtype=np.float32)


def _bf16_pieces(v):
    rest = np.asarray(v, np.float64)
    pieces = []
    for _ in range(ALIBI_COLS):
        p = rest.astype(BF16).astype(np.float64)
        pieces.append(p.astype(np.float32))
        rest = rest - p
    assert np.all(np.abs(rest) <= np.abs(v) * 2.0 ** -30)
    return pieces


def _rms_bf16(x, g):
    ms = jnp.mean(x * x, axis=-1, keepdims=True)
    return (x * lax.rsqrt(ms + RMS_EPS) * g).astype(BF16)


def _params(*sem):
    return pltpu.CompilerParams(dimension_semantics=sem, vmem_limit_bytes=VMEM_LIMIT)


def _mem_kv_kernel(mem_ref, g_ref, wk_ref, wvT_ref, k_ref, vT_ref):
    h = _rms_bf16(mem_ref[0], g_ref[...])
    k = jnp.dot(h, wk_ref[0], preferred_element_type=F32)
    vT = lax.dot_general(wvT_ref[0], h, _NT, preferred_element_type=F32)
    lane = lax.broadcasted_iota(jnp.int32, k.shape, 1)
    for hh in range(N_MEM_HEADS):
        mine = (lane >= hh * HEAD_DIM) & (lane < (hh + 1) * HEAD_DIM)
        k_ref[0, 0, hh] = jnp.where(mine, k, 0.0).astype(BF16)
        vT_ref[0, 0, hh * V_ROWS:hh * V_ROWS + HEAD_DIM] = (
            vT[hh * HEAD_DIM:(hh + 1) * HEAD_DIM].astype(BF16))
        vT_ref[0, 0, hh * V_ROWS + HEAD_DIM:(hh + 1) * V_ROWS] = jnp.ones(
            (ONES_ROWS, vT.shape[1]), BF16)


def _mem_kv(mem, mem_norm_g, w_mem_kv):
    b, m, d = mem.shape
    depth = w_mem_kv.shape[0]
    wk = w_mem_kv[:, :, :MEM_WIDTH].astype(BF16)
    wvT = jnp.swapaxes(w_mem_kv[:, :, MEM_WIDTH:], 1, 2).astype(BF16)
    return pl.pallas_call(
        _mem_kv_kernel,
        grid=(depth, b),
        in_specs=[
            pl.BlockSpec((1, m, d), lambda i, bb: (bb, 0, 0)),
            pl.BlockSpec((1, d), lambda i, bb: (0, 0)),
            pl.BlockSpec((1, d, MEM_WIDTH), lambda i, bb: (i, 0, 0)),
            pl.BlockSpec((1, MEM_WIDTH, d), lambda i, bb: (i, 0, 0)),
        ],
        out_specs=[
            pl.BlockSpec((1, 1, N_MEM_HEADS, m, MEM_WIDTH), lambda i, bb: (i, bb, 0, 0, 0)),
            pl.BlockSpec((1, 1, N_MEM_HEADS * V_ROWS, m), lambda i, bb: (i, bb, 0, 0)),
        ],
        out_shape=[
            jax.ShapeDtypeStruct((depth, b, N_MEM_HEADS, m, MEM_WIDTH), BF16),
            jax.ShapeDtypeStruct((depth, b, N_MEM_HEADS * V_ROWS, m), BF16),
        ],
        compiler_params=_params("arbitrary", "arbitrary"),
        name="mem_kv",
    )(mem, mem_norm_g.reshape(1, d), wk, wvT)


def _moba_projection(h, wT_ref, w_ref, qT_ref, kaug_ref, vT_ref, kstat_ref, qmT_ref, z_ref):
    tm = h.shape[0]
    nb = tm // MOBA_BLOCK
    outT = lax.dot_general(wT_ref[...], h, _NT, preferred_element_type=F32)
    out = jnp.dot(h, w_ref[...], preferred_element_type=F32)
    ones = jnp.ones((ONES_ROWS, MOBA_BLOCK), BF16)
    for p in range(N_PAIRS):
        for bb in range(nb):
            cols = slice(bb * MOBA_BLOCK, (bb + 1) * MOBA_BLOCK)
            qT_ref[0, p, bb] = outT[p * LANES:(p + 1) * LANES, cols].astype(BF16)
            for hh in range(HEAD_PAIR):
                v0 = ATTN_WIDTH + (p * HEAD_PAIR + hh) * HEAD_DIM
                vT_ref[0, p, bb, hh * V_ROWS:hh * V_ROWS + HEAD_DIM] = (
                    outT[v0:v0 + HEAD_DIM, cols].astype(BF16))
                vT_ref[0, p, bb, hh * V_ROWS + HEAD_DIM:(hh + 1) * V_ROWS] = ones
    qmT_ref[0] = outT[2 * ATTN_WIDTH:, :].astype(BF16)
    k = out[:, :ATTN_WIDTH]
    z_ref[0] = out[:, ATTN_WIDTH:].astype(BF16)
    row = lax.broadcasted_iota(jnp.int32, (tm, LANES), 0)
    lane = lax.broadcasted_iota(jnp.int32, (tm, LANES), 1)
    blk = pl.program_id(1) * nb + row // MOBA_BLOCK
    s_rel = (row % MOBA_BLOCK).astype(F32)
    ext = jnp.where(lane < SEL_COLS, (lane == blk).astype(F32),
                    jnp.where(lane < SEL_COLS + ALIBI_COLS, s_rel, 0.0)).astype(BF16)
    k_bf16 = k.astype(BF16)
    for p in range(N_PAIRS):
        kaug_ref[0, p, :, 0:LANES] = k_bf16[:, p * LANES:(p + 1) * LANES]
        kaug_ref[0, p, :, LANES:AUG] = ext
    k_abs = jnp.abs(k_bf16.astype(F32))
    for bb in range(nb):
        rows = slice(bb * MOBA_BLOCK, (bb + 1) * MOBA_BLOCK)
        kstat_ref[0, 0, 0, bb:bb + 1, :] = jnp.mean(k[rows], axis=0, keepdims=True)
        kstat_ref[0, 0, 1, bb:bb + 1, :] = jnp.max(k_abs[rows], axis=0, keepdims=True)


def _moba_projection_specs(b, s, tm):
    nblk = s // MOBA_BLOCK
    nb = tm // MOBA_BLOCK
    specs = [
        pl.BlockSpec((1, N_PAIRS, nb, LANES, MOBA_BLOCK), lambda bb, i: (bb, 0, i, 0, 0)),
        pl.BlockSpec((1, N_PAIRS, tm, AUG), lambda bb, i: (bb, 0, i, 0)),
        pl.BlockSpec((1, N_PAIRS, nb, HEAD_PAIR * V_ROWS, MOBA_BLOCK),
                     lambda bb, i: (bb, 0, i, 0, 0)),
        pl.BlockSpec((1, 1, 2, nb, ATTN_WIDTH), lambda bb, i: (bb, i, 0, 0, 0)),
        pl.BlockSpec((1, MEM_WIDTH, tm), lambda bb, i: (bb, 0, i)),
        pl.BlockSpec((1, tm, BRANCH_WIDTH), lambda bb, i: (bb, i, 0)),
    ]
    shapes = [
        jax.ShapeDtypeStruct((b, N_PAIRS, nblk, LANES, MOBA_BLOCK), BF16),
        jax.ShapeDtypeStruct((b, N_PAIRS, s, AUG), BF16),
        jax.ShapeDtypeStruct((b, N_PAIRS, nblk, HEAD_PAIR * V_ROWS, MOBA_BLOCK), BF16),
        jax.ShapeDtypeStruct((b, s // tm, 2, nb, ATTN_WIDTH), F32),
        jax.ShapeDtypeStruct((b, MEM_WIDTH, s), BF16),
        jax.ShapeDtypeStruct((b, s, BRANCH_WIDTH), BF16),
    ]
    return specs, shapes


def _swa_projection(h, wT_ref, w_ref, qT_ref, k_ref, vT_ref, qmT_ref, z_ref):
    outT = lax.dot_general(wT_ref[...], h, _NT, preferred_element_type=F32)
    out = jnp.dot(h, w_ref[...], preferred_element_type=F32)
    tm = h.shape[0]
    qT_ref[0] = outT[:ATTN_WIDTH].astype(BF16)
    for g in range(N_KV_HEADS_B):
        v0 = ATTN_WIDTH + g * HEAD_DIM
        vT_ref[0, g * V_ROWS:g * V_ROWS + HEAD_DIM] = outT[v0:v0 + HEAD_DIM].astype(BF16)
        vT_ref[0, g * V_ROWS + HEAD_DIM:(g + 1) * V_ROWS] = jnp.ones((ONES_ROWS, tm), BF16)
    qmT_ref[0] = outT[ATTN_WIDTH + KV_WIDTH_B:].astype(BF16)
    k_ref[0] = out[:, :KV_WIDTH_B].astype(BF16)
    z_ref[0] = out[:, KV_WIDTH_B:].astype(BF16)


def _swa_projection_specs(b, s, tm):
    specs = [
        pl.BlockSpec((1, ATTN_WIDTH, tm), lambda bb, i: (bb, 0, i)),
        pl.BlockSpec((1, tm, KV_WIDTH_B), lambda bb, i: (bb, i, 0)),
        pl.BlockSpec((1, N_KV_HEADS_B * V_ROWS, tm), lambda bb, i: (bb, 0, i)),
        pl.BlockSpec((1, MEM_WIDTH, tm), lambda bb, i: (bb, 0, i)),
        pl.BlockSpec((1, tm, BRANCH_WIDTH), lambda bb, i: (bb, i, 0)),
    ]
    shapes = [
        jax.ShapeDtypeStruct((b, ATTN_WIDTH, s), BF16),
        jax.ShapeDtypeStruct((b, s, KV_WIDTH_B), BF16),
        jax.ShapeDtypeStruct((b, N_KV_HEADS_B * V_ROWS, s), BF16),
        jax.ShapeDtypeStruct((b, MEM_WIDTH, s), BF16),
        jax.ShapeDtypeStruct((b, s, BRANCH_WIDTH), BF16),
    ]
    return specs, shapes


_PROJECTIONS = {
    "moba": (_moba_projection, _moba_projection_specs),
    "swa": (_swa_projection, _swa_projection_specs),
}


def _split_in_weights(w_in, mixer):
    scale = HEAD_DIM ** -0.5 * LOG2E
    kvw = ATTN_WIDTH if mixer == "moba" else KV_WIDTH_B
    o = np.cumsum([0, ATTN_WIDTH, kvw, kvw, MEM_WIDTH, BRANCH_WIDTH])
    q, k, v, qm, z = (w_in[:, o[i]:o[i + 1]] for i in range(5))
    wT = jnp.concatenate([q * scale, v, qm * scale], axis=1).T.astype(BF16)
    w = jnp.concatenate([k, z], axis=1).astype(BF16)
    return wT, w


def _first_projection_kernel(x_ref, g_ref, wT_ref, w_ref, *out_refs, mixer):
    h = _rms_bf16(x_ref[0], g_ref[...])
    _PROJECTIONS[mixer][0](h, wT_ref, w_ref, *out_refs)


def _first_projection(x, g, w_in, mixer):
    b, s, d = x.shape
    tm = min(ROW_TILE, s)
    wT, w = _split_in_weights(w_in, mixer)
    specs, shapes = _PROJECTIONS[mixer][1](b, s, tm)
    return pl.pallas_call(
        functools.partial(_first_projection_kernel, mixer=mixer),
        grid=(b, s // tm),
        in_specs=[
            pl.BlockSpec((1, tm, d), lambda bb, i: (bb, i, 0)),
            pl.BlockSpec((1, d), lambda bb, i: (0, 0)),
            pl.BlockSpec(wT.shape, lambda bb, i: (0, 0)),
            pl.BlockSpec(w.shape, lambda bb, i: (0, 0)),
        ],
        out_specs=specs,
        out_shape=shapes,
        compiler_params=_params("parallel", "arbitrary"),
        name=f"in_proj_{mixer}",
    )(x, g.reshape(1, d), wT, w)


def _memory_attention_T(qmT, mk_ref, mvT_ref):
    sT = [jnp.dot(mk_ref[0, hh], qmT, preferred_element_type=F32)
          for hh in range(N_MEM_HEADS)]
    outs = []
    for hh in range(N_MEM_HEADS):
        p = jnp.exp2(sT[hh] - jnp.max(sT[hh], axis=0, keepdims=True))
        acc = jnp.dot(mvT_ref[0, hh * V_ROWS:(hh + 1) * V_ROWS, :], p.astype(BF16),
                      preferred_element_type=F32)
        outs.append(acc[:HEAD_DIM] * (1.0 / acc[HEAD_DIM:HEAD_DIM + 1]))
    return jnp.concatenate(outs, axis=0)


def _epilogue_kernel(y_ref, qmT_ref, z_ref, x_ref, mk_ref, mvT_ref, wout_ref, g_ref, *rest, mixer):
    ymem = _memory_attention_T(qmT_ref[0], mk_ref, mvT_ref).T
    y = jnp.concatenate([y_ref[0].astype(F32), ymem], axis=-1)
    z = z_ref[0].astype(F32)
    gated = (y * (z / (1.0 + jnp.exp(-z)))).astype(BF16)
    xn = x_ref[0] + jnp.dot(gated, wout_ref[...], preferred_element_type=F32)
    if mixer is None:
        (o_ref,) = rest
        ms = jnp.mean(xn * xn, axis=-1, keepdims=True)
        o_ref[0] = xn * lax.rsqrt(ms + RMS_EPS) * g_ref[...]
    else:
        wT_ref, w_ref, xo_ref = rest[:3]
        xo_ref[0] = xn
        _PROJECTIONS[mixer][0](_rms_bf16(xn, g_ref[...]), wT_ref, w_ref, *rest[3:])


def _epilogue(y_self, qmT, z, x, mem_k, mem_vT, layer, w_out, g, w_in_next=None, mixer=None):
    b, s, d = x.shape
    tm = min(ROW_TILE, s)
    m = mem_k.shape[3]
    in_specs = [
        pl.BlockSpec((1, tm, ATTN_WIDTH), lambda bb, i: (bb, i, 0)),
        pl.BlockSpec((1, MEM_WIDTH, tm), lambda bb, i: (bb, 0, i)),
        pl.BlockSpec((1, tm, BRANCH_WIDTH), lambda bb, i: (bb, i, 0)),
        pl.BlockSpec((1, tm, d), lambda bb, i: (bb, i, 0)),
        pl.BlockSpec((1, N_MEM_HEADS, m, MEM_WIDTH), lambda bb, i: (bb, 0, 0, 0)),
        pl.BlockSpec((1, N_MEM_HEADS * V_ROWS, m), lambda bb, i: (bb, 0, 0)),
        pl.BlockSpec((BRANCH_WIDTH, d), lambda bb, i: (0, 0)),
        pl.BlockSpec((1, d), lambda bb, i: (0, 0)),
    ]
    args = [y_self, qmT, z, x, mem_k[layer], mem_vT[layer], w_out.astype(BF16), g.reshape(1, d)]
    x_spec = pl.BlockSpec((1, tm, d), lambda bb, i: (bb, i, 0))
    x_shape = jax.ShapeDtypeStruct((b, s, d), F32)
    if mixer is None:
        out_specs, out_shape = [x_spec], [x_shape]
    else:
        wT, w = _split_in_weights(w_in_next, mixer)
        in_specs += [pl.BlockSpec(wT.shape, lambda bb, i: (0, 0)),
                     pl.BlockSpec(w.shape, lambda bb, i: (0, 0))]
        args += [wT, w]
        specs, shapes = _PROJECTIONS[mixer][1](b, s, tm)
        out_specs, out_shape = [x_spec] + specs, [x_shape] + shapes
    return pl.pallas_call(
        functools.partial(_epilogue_kernel, mixer=mixer),
        grid=(b, s // tm),
        in_specs=in_specs,
        out_specs=out_specs,
        out_shape=out_shape,
        compiler_params=_params("parallel", "arbitrary"),
        name=f"epilogue_{mixer}",
    )(*args)


def _moba_kernel(qT_ref, kaug_ref, vT_ref, kmean_ref, kabs_ref, srows_ref, slope_ref, o_ref,
                 qaug_ref):
    j = pl.program_id(2)
    nblk = kmean_ref.shape[1]
    tq = qT_ref.shape[-1]
    qT = qT_ref[0, 0, 0]
    kmean = kmean_ref[0]
    drow = lax.broadcasted_iota(jnp.int32, qT.shape, 0)
    klane = lax.broadcasted_iota(jnp.int32, kmean.shape, 1)
    blk = lax.broadcasted_iota(jnp.int32, (nblk, tq), 0)
    krow = lax.broadcasted_iota(jnp.int32, (MOBA_BLOCK, tq), 0)
    qlane = lax.broadcasted_iota(jnp.int32, (MOBA_BLOCK, tq), 1)
    causal = krow <= qlane

    def weighted_values(n, hh, p):
        return jnp.dot(vT_ref[0, 0, n, hh * V_ROWS:(hh + 1) * V_ROWS, :], p.astype(BF16),
                       preferred_element_type=F32)

    no_choice = jnp.zeros((SEL_COLS, tq), BF16)
    q_abs = jnp.abs(qT)
    head_lanes = [(klane >= hh * HEAD_DIM) & (klane < (hh + 1) * HEAD_DIM) for hh in range(HEAD_PAIR)]
    q_heads = [jnp.where((drow >= hh * HEAD_DIM) & (drow < (hh + 1) * HEAD_DIM), qT, jnp.zeros_like(qT))
               for hh in range(HEAD_PAIR)]
    gates = [jnp.dot(jnp.where(head_lanes[hh], kmean, 0.0).astype(BF16), qT,
                     preferred_element_type=F32) for hh in range(HEAD_PAIR)]
    own_sT = [jnp.dot(kaug_ref[0, 0, j],
                      jnp.concatenate([q_heads[hh], no_choice, srows_ref[0, hh]], axis=0),
                      preferred_element_type=F32) for hh in range(HEAD_PAIR)]
    qk_bound = [jnp.dot(jnp.where(head_lanes[hh], kabs_ref[0] * BF16_ROUND_UP, 0.0).astype(BF16),
                        q_abs, preferred_element_type=F32) for hh in range(HEAD_PAIR)]

    own_max, own_acc = [], []
    for hh in range(HEAD_PAIR):
        sT = jnp.where(causal, own_sT[hh], MASKED)
        m = jnp.max(sT, axis=0, keepdims=True)
        own_max.append(m)
        own_acc.append(weighted_values(j, hh, jnp.exp2(sT - m)))

    blocks_back = (j - blk).astype(F32)
    needed = blk < 0
    for hh in range(HEAD_PAIR):
        alibi = slope_ref[0, hh] * ((MOBA_BLOCK - 1) - MOBA_BLOCK * blocks_back)
        log2_weight_bound = qk_bound[hh] - own_max[hh] + alibi + BOUND_SLACK
        needed = needed | ((log2_weight_bound >= EXP2_IS_ZERO_BELOW) & (blk < j))
    first_needed = jnp.min(jnp.where(needed, blk, j).astype(F32)).astype(jnp.int32)

    for hh in range(HEAD_PAIR):
        gate = jnp.where(blk < j, gates[hh], -jnp.inf)
        bias = jnp.full((nblk, tq), MASKED, F32)
        for _ in range(MOBA_TOPK):
            best = jnp.max(gate, axis=0, keepdims=True)
            first = jnp.min(jnp.where(gate == best, blk, nblk), axis=0, keepdims=True)
            pick = blk == first
            chosen = jnp.where(best > -jnp.inf, 0.0, MASKED)
            bias = jnp.where(pick, jnp.maximum(bias, chosen), bias)
            gate = jnp.where(pick, -jnp.inf, gate)
        bias = bias.astype(BF16)
        if nblk < SEL_COLS:
            bias = jnp.concatenate([bias, jnp.zeros((SEL_COLS - nblk, tq), BF16)], axis=0)
        qaug_ref[hh] = jnp.concatenate([q_heads[hh], bias, srows_ref[0, hh]], axis=0)

    def scores(n, hh):
        return jnp.dot(kaug_ref[0, 0, n], qaug_ref[hh], preferred_element_type=F32)

    def block_offset(n, hh):
        return slope_ref[0, hh] * ((n - j) * MOBA_BLOCK).astype(F32)

    def finish(accs):
        oT = jnp.concatenate([acc[:HEAD_DIM] * (1.0 / acc[HEAD_DIM:HEAD_DIM + 1]) for acc in accs],
                             axis=0)
        o_ref[0] = oT.T.astype(o_ref.dtype)

    def fixed_shift_blocks(first, count, accs):
        slots = [first + u for u in range(count)]
        blocks = [jnp.minimum(n, j) for n in slots]
        sT = [[scores(n, hh) for hh in range(HEAD_PAIR)] for n in blocks]
        accs = list(accs)
        for u, n in enumerate(blocks):
            for hh in range(HEAD_PAIR):
                shift = jnp.where(slots[u] < j, own_max[hh] - block_offset(n, hh), jnp.inf)
                accs[hh] += weighted_values(n, hh, jnp.exp2(sT[u][hh] - shift))
        return accs

    fast, start = own_acc, first_needed
    for unroll in TRIP_SIZES:
        left = j - start
        trips = (left + 1) // unroll if unroll == MASKED_TRIP else left // unroll
        fast = lax.fori_loop(
            0, trips,
            lambda g, a, start=start, unroll=unroll: fixed_shift_blocks(start + g * unroll, unroll, a),
            fast)
        start = jnp.minimum(start + trips * unroll, j)
    finish(fast)
    denominators = jnp.maximum(fast[0][HEAD_DIM:HEAD_DIM + 1], fast[1][HEAD_DIM:HEAD_DIM + 1])
    in_range = jnp.max(denominators) < OVERFLOW_GUARD

    @pl.when(jnp.logical_not(in_range))
    def _():
        def running_max_block(n, carry):
            out = []
            for hh in range(HEAD_PAIR):
                m, acc = carry[2 * hh:2 * hh + 2]
                sT = scores(n, hh)
                c = block_offset(n, hh)
                m_new = jnp.maximum(m, jnp.max(sT, axis=0, keepdims=True) + c)
                acc = jnp.exp2(m - m_new) * acc + weighted_values(n, hh, jnp.exp2(sT - (m_new - c)))
                out += [m_new, acc]
            return out

        slow = lax.fori_loop(0, j, running_max_block,
                             [t for hh in range(HEAD_PAIR) for t in (own_max[hh], own_acc[hh])])
        finish(slow[1::2])


def _moba_tables(tq):
    slopes = _alibi_slopes(N_HEADS).astype(np.float64) * LOG2E
    pieces = _bf16_pieces(slopes)
    rows = np.zeros((N_HEADS, AUG - LANES - SEL_COLS, tq), np.float32)
    for c, piece in enumerate(pieces):
        rows[:, c, :] = piece[:, None]
    srows = jnp.asarray(rows.reshape(N_PAIRS, HEAD_PAIR, AUG - LANES - SEL_COLS, tq), BF16)
    svec = jnp.asarray(np.broadcast_to(slopes[:, None, None], (N_HEADS, 1, tq))
                       .reshape(N_PAIRS, HEAD_PAIR, 1, tq), F32)
    return srows, svec


def _moba_attention(qT, kaug, vT, kstat):
    b, _, nblk, _, tq = qT.shape
    assert nblk <= SEL_COLS and tq == MOBA_BLOCK
    s = nblk * MOBA_BLOCK
    kaug = kaug.reshape(b, N_PAIRS, nblk, MOBA_BLOCK, AUG)
    kmean = kstat[:, :, 0].reshape(b, nblk, ATTN_WIDTH)
    kabs = kstat[:, :, 1].reshape(b, nblk, ATTN_WIDTH)
    srows, svec = _moba_tables(tq)
    return pl.pallas_call(
        _moba_kernel,
        grid=(b, N_PAIRS, nblk),
        in_specs=[
            pl.BlockSpec((1, 1, 1, LANES, tq), lambda bb, p, j: (bb, p, j, 0, 0)),
            pl.BlockSpec((1, 1, nblk, MOBA_BLOCK, AUG), lambda bb, p, j: (bb, p, 0, 0, 0)),
            pl.BlockSpec((1, 1, nblk, HEAD_PAIR * V_ROWS, MOBA_BLOCK),
                         lambda bb, p, j: (bb, p, 0, 0, 0)),
            pl.BlockSpec((1, nblk, LANES), lambda bb, p, j: (bb, 0, p)),
            pl.BlockSpec((1, nblk, LANES), lambda bb, p, j: (bb, 0, p)),
            pl.BlockSpec((1, HEAD_PAIR, AUG - LANES - SEL_COLS, tq), lambda bb, p, j: (p, 0, 0, 0)),
            pl.BlockSpec((1, HEAD_PAIR, 1, tq), lambda bb, p, j: (p, 0, 0, 0)),
        ],
        out_specs=pl.BlockSpec((1, tq, LANES), lambda bb, p, j: (bb, j, p)),
        out_shape=jax.ShapeDtypeStruct((b, s, ATTN_WIDTH), BF16),
        scratch_shapes=[pltpu.VMEM((HEAD_PAIR, AUG, tq), BF16)],
        compiler_params=_params("parallel", "parallel", "arbitrary"),
        name="moba_attention",
    )(qT, kaug, vT, kmean, kabs, srows, svec)


def _swa_kernel(qT_ref, kp_ref, kc_ref, vTp_ref, vTc_ref, bias_ref, sink_ref, o_ref, yT_ref):
    tq = qT_ref.shape[-1]
    kwin = jnp.concatenate([kp_ref[0], kc_ref[0]], axis=0)
    vTwin = jnp.concatenate([vTp_ref[0], vTc_ref[0]], axis=1)
    zeros = jnp.zeros((HEAD_DIM, tq), BF16)
    group = N_HEADS // N_KV_HEADS_B

    def padded_q(h):
        q_h = qT_ref[0, h * HEAD_DIM:(h + 1) * HEAD_DIM, :]
        return jnp.concatenate([q_h, zeros] if h // group == 0 else [zeros, q_h], axis=0)

    worst = jnp.zeros((1, group * tq), F32)
    for g in range(N_KV_HEADS_B):
        heads = range(g * group, (g + 1) * group)
        sink = jnp.concatenate([sink_ref[h] * LOG2E for h in heads], axis=1)
        s_g = jnp.dot(kwin, jnp.concatenate([padded_q(h) for h in heads], axis=1),
                      preferred_element_type=F32)
        p = jnp.exp2(s_g + bias_ref[0, g] - sink).astype(BF16)
        acc = jnp.dot(vTwin[g * V_ROWS:(g + 1) * V_ROWS, :], p, preferred_element_type=F32)
        denom = acc[HEAD_DIM:HEAD_DIM + 1] + 1.0
        worst = jnp.maximum(worst, denom)
        out = acc[:HEAD_DIM] * (1.0 / denom)
        for i, h in enumerate(heads):
            yT_ref[h * HEAD_DIM:(h + 1) * HEAD_DIM, :] = out[:, i * tq:(i + 1) * tq]
    o_ref[0] = yT_ref[...].T.astype(o_ref.dtype)
    in_range = jnp.max(worst) < OVERFLOW_GUARD

    @pl.when(jnp.logical_not(in_range))
    def _():
        for h in range(N_HEADS):
            g, i = divmod(h, group)
            s_h = (jnp.dot(kwin, padded_q(h), preferred_element_type=F32)
                   + bias_ref[0, g, :, i * tq:(i + 1) * tq])
            sink = sink_ref[h] * LOG2E
            m = jnp.maximum(jnp.max(s_h, axis=0, keepdims=True), sink)
            acc = jnp.dot(vTwin[g * V_ROWS:(g + 1) * V_ROWS, :], jnp.exp2(s_h - m).astype(BF16),
                          preferred_element_type=F32)
            denom = acc[HEAD_DIM:HEAD_DIM + 1] + jnp.exp2(sink - m)
            yT_ref[h * HEAD_DIM:(h + 1) * HEAD_DIM, :] = acc[:HEAD_DIM] * (1.0 / denom)
        o_ref[0] = yT_ref[...].T.astype(o_ref.dtype)


def _swa_bias_table(tq):
    slopes = _alibi_slopes(N_HEADS).astype(np.float64) * LOG2E
    kw = np.arange(WINDOW + tq)[:, None]
    q = np.arange(tq)[None, :]
    dist = q + WINDOW - kw
    ok = (dist >= 0) & (dist < WINDOW)
    first_tile = ok & (kw >= WINDOW)
    table = np.stack([np.where(valid[None], -slopes[:, None, None] * dist[None], MASKED)
                      for valid in (first_tile, ok)])
    table = table.reshape(2, N_KV_HEADS_B, N_HEADS // N_KV_HEADS_B, WINDOW + tq, tq)
    table = np.moveaxis(table, 2, 3).reshape(2, N_KV_HEADS_B, WINDOW + tq, -1)
    return jnp.asarray(table, F32)


def _swa_attention(qT, k, vT, sinks):
    b, _, s = qT.shape
    tq = min(SWA_TILE, s)
    r = tq // WINDOW
    bias = _swa_bias_table(tq)
    sink_rows = jnp.broadcast_to(sinks.astype(F32)[:, None, None], (N_HEADS, 1, tq))
    prev = lambda j: jnp.maximum(j * r - 1, 0)
    return pl.pallas_call(
        _swa_kernel,
        grid=(b, s // tq),
        in_specs=[
            pl.BlockSpec((1, ATTN_WIDTH, tq), lambda bb, j: (bb, 0, j)),
            pl.BlockSpec((1, WINDOW, KV_WIDTH_B), lambda bb, j: (bb, prev(j), 0)),
            pl.BlockSpec((1, tq, KV_WIDTH_B), lambda bb, j: (bb, j, 0)),
            pl.BlockSpec((1, N_KV_HEADS_B * V_ROWS, WINDOW), lambda bb, j: (bb, 0, prev(j))),
            pl.BlockSpec((1, N_KV_HEADS_B * V_ROWS, tq), lambda bb, j: (bb, 0, j)),
            pl.BlockSpec((1,) + bias.shape[1:], lambda bb, j: (jnp.minimum(j, 1), 0, 0, 0)),
            pl.BlockSpec(sink_rows.shape, lambda bb, j: (0, 0, 0)),
        ],
        out_specs=pl.BlockSpec((1, tq, ATTN_WIDTH), lambda bb, j: (bb, j, 0)),
        out_shape=jax.ShapeDtypeStruct((b, s, ATTN_WIDTH), BF16),
        scratch_shapes=[pltpu.VMEM((ATTN_WIDTH, tq), F32)],
        compiler_params=_params("parallel", "arbitrary"),
        name="swa_attention",
    )(qT, k, k, vT, vT, bias, sink_rows)


def kernel(x, mem, norm_g, w_in_a, w_in_b, sinks_b, w_mem_kv, w_out, mem_norm_g, final_norm_g):
    depth = norm_g.shape[0]
    b, s, _ = x.shape
    assert s % ROW_TILE == 0 or s < ROW_TILE
    mem_k, mem_vT = _mem_kv(mem, mem_norm_g, w_mem_kv)
    mixers = ["moba" if i % 2 == 0 else "swa" for i in range(depth)]
    w_in = [w_in_a[i // 2] if i % 2 == 0 else w_in_b[i // 2] for i in range(depth)]
    proj = _first_projection(x, norm_g[0], w_in[0], mixers[0])
    for i in range(depth):
        if mixers[i] == "moba":
            qT, kaug, vT, kstat, qmT, z = proj
            y_self = _moba_attention(qT, kaug, vT, kstat)
        else:
            qT, k, vT, qmT, z = proj
            y_self = _swa_attention(qT, k, vT, sinks_b[i // 2])
        if i + 1 < depth:
            x, *proj = _epilogue(y_self, qmT, z, x, mem_k, mem_vT, i, w_out[i], norm_g[i + 1],
                                 w_in[i + 1], mixers[i + 1])
        else:
            (x,) = _epilogue(y_self, qmT, z, x, mem_k, mem_vT, i, w_out[i], final_norm_g)
    return x
```

```python
import functools
import math

import jax
import jax.numpy as jnp
import numpy as np
from jax import lax
from jax.experimental import pallas as pl
from jax.experimental.pallas import tpu as pltpu

HEAD_DIM = 64
N_HEADS = 12
N_KV_HEADS_B = 2
N_MEM_HEADS = 4
ATTN_WIDTH = N_HEADS * HEAD_DIM
MEM_WIDTH = N_MEM_HEADS * HEAD_DIM
BRANCH_WIDTH = ATTN_WIDTH + MEM_WIDTH
KV_WIDTH_B = N_KV_HEADS_B * HEAD_DIM
MOBA_BLOCK = 256
MOBA_TOPK = 3
WINDOW = 128
RMS_EPS = 1e-6

LANES = 128
HEAD_PAIR = LANES // HEAD_DIM
N_PAIRS = N_HEADS // HEAD_PAIR
SEL_COLS = 64
ALIBI_COLS = 4
ONES_ROWS = 16
V_ROWS = HEAD_DIM + ONES_ROWS
LOG2E = math.log2(math.e)
AUG = 2 * LANES
MASKED = -(2.0 ** 30)
OVERFLOW_GUARD = 2.0 ** 100
EXP2_IS_ZERO_BELOW = -150.0
BOUND_SLACK = 1.0
BF16_ROUND_UP = 1.0 + 2.0 ** -7
TRIP_SIZES = (16, 4, 2, 1)
MASKED_TRIP = 4
Q_BLOCKS_PER_STEP = 2
ROW_TILE = 512
SWA_TILE = 256
VMEM_LIMIT = 56 * 1024 * 1024

F32 = jnp.float32
BF16 = jnp.bfloat16
_NT = (((1,), (1,)), ((), ()))


def _alibi_slopes(n_heads):
    def pow2_slopes(n):
        start = 2.0 ** (-8.0 / n)
        return [start ** (i + 1) for i in range(n)]
    if math.log2(n_heads).is_integer():
        vals = pow2_slopes(n_heads)
    else:
        c = 2 ** math.floor(math.log2(n_heads))
        vals = pow2_slopes(c) + pow2_slopes(2 * c)[0::2][: n_heads - c]
    return np.array(vals, dtype=np.float32)


def _bf16_pieces(v):
    rest = np.asarray(v, np.float64)
    pieces = []
    for _ in range(ALIBI_COLS):
        p = rest.astype(BF16).astype(np.float64)
        pieces.append(p.astype(np.float32))
        rest = rest - p
    assert np.all(np.abs(rest) <= np.abs(v) * 2.0 ** -30)
    return pieces


def _rms_bf16(x, g):
    ms = jnp.mean(x * x, axis=-1, keepdims=True)
    return (x * lax.rsqrt(ms + RMS_EPS) * g).astype(BF16)


def _params(*sem):
    return pltpu.CompilerParams(dimension_semantics=sem, vmem_limit_bytes=VMEM_LIMIT)


def _mem_kv_kernel(mem_ref, g_ref, wk_ref, wvT_ref, k_ref, vT_ref):
    h = _rms_bf16(mem_ref[0], g_ref[...])
    k = jnp.dot(h, wk_ref[0], preferred_element_type=F32)
    vT = lax.dot_general(wvT_ref[0], h, _NT, preferred_element_type=F32)
    lane = lax.broadcasted_iota(jnp.int32, k.shape, 1)
    for hh in range(N_MEM_HEADS):
        mine = (lane >= hh * HEAD_DIM) & (lane < (hh + 1) * HEAD_DIM)
        k_ref[0, 0, hh] = jnp.where(mine, k, 0.0).astype(BF16)
        vT_ref[0, 0, hh * V_ROWS:hh * V_ROWS + HEAD_DIM] = (
            vT[hh * HEAD_DIM:(hh + 1) * HEAD_DIM].astype(BF16))
        vT_ref[0, 0, hh * V_ROWS + HEAD_DIM:(hh + 1) * V_ROWS] = jnp.ones(
            (ONES_ROWS, vT.shape[1]), BF16)


def _mem_kv(mem, mem_norm_g, w_mem_kv):
    b, m, d = mem.shape
    depth = w_mem_kv.shape[0]
    wk = w_mem_kv[:, :, :MEM_WIDTH].astype(BF16)
    wvT = jnp.swapaxes(w_mem_kv[:, :, MEM_WIDTH:], 1, 2).astype(BF16)
    return pl.pallas_call(
        _mem_kv_kernel,
        grid=(depth, b),
        in_specs=[
            pl.BlockSpec((1, m, d), lambda i, bb: (bb, 0, 0)),
            pl.BlockSpec((1, d), lambda i, bb: (0, 0)),
            pl.BlockSpec((1, d, MEM_WIDTH), lambda i, bb: (i, 0, 0)),
            pl.BlockSpec((1, MEM_WIDTH, d), lambda i, bb: (i, 0, 0)),
        ],
        out_specs=[
            pl.BlockSpec((1, 1, N_MEM_HEADS, m, MEM_WIDTH), lambda i, bb: (i, bb, 0, 0, 0)),
            pl.BlockSpec((1, 1, N_MEM_HEADS * V_ROWS, m), lambda i, bb: (i, bb, 0, 0)),
        ],
        out_shape=[
            jax.ShapeDtypeStruct((depth, b, N_MEM_HEADS, m, MEM_WIDTH), BF16),
            jax.ShapeDtypeStruct((depth, b, N_MEM_HEADS * V_ROWS, m), BF16),
        ],
        compiler_params=_params("arbitrary", "arbitrary"),
        name="mem_kv",
    )(mem, mem_norm_g.reshape(1, d), wk, wvT)


def _moba_projection(h, wT_ref, w_ref, qT_ref, kaug_ref, vT_ref, kstat_ref, qmT_ref, z_ref):
    tm = h.shape[0]
    nb = tm // MOBA_BLOCK
    outT = lax.dot_general(wT_ref[...], h, _NT, preferred_element_type=F32)
    out = jnp.dot(h, w_ref[...], preferred_element_type=F32)
    ones = jnp.ones((ONES_ROWS, MOBA_BLOCK), BF16)
    for p in range(N_PAIRS):
        for bb in range(nb):
            cols = slice(bb * MOBA_BLOCK, (bb + 1) * MOBA_BLOCK)
            qT_ref[0, p, bb] = outT[p * LANES:(p + 1) * LANES, cols].astype(BF16)
            for hh in range(HEAD_PAIR):
                v0 = ATTN_WIDTH + (p * HEAD_PAIR + hh) * HEAD_DIM
                vT_ref[0, p, bb, hh * V_ROWS:hh * V_ROWS + HEAD_DIM] = (
                    outT[v0:v0 + HEAD_DIM, cols].astype(BF16))
                vT_ref[0, p, bb, hh * V_ROWS + HEAD_DIM:(hh + 1) * V_ROWS] = ones
    qmT_ref[0] = outT[2 * ATTN_WIDTH:, :].astype(BF16)
    k = out[:, :ATTN_WIDTH]
    z_ref[0] = out[:, ATTN_WIDTH:].astype(BF16)
    row = lax.broadcasted_iota(jnp.int32, (tm, LANES), 0)
    lane = lax.broadcasted_iota(jnp.int32, (tm, LANES), 1)
    blk = pl.program_id(1) * nb + row // MOBA_BLOCK
    s_rel = (row % MOBA_BLOCK).astype(F32)
    ext = jnp.where(lane < SEL_COLS, (lane == blk).astype(F32),
                    jnp.where(lane < SEL_COLS + ALIBI_COLS, s_rel, 0.0)).astype(BF16)
    k_bf16 = k.astype(BF16)
    for p in range(N_PAIRS):
        kaug_ref[0, p, :, 0:LANES] = k_bf16[:, p * LANES:(p + 1) * LANES]
        kaug_ref[0, p, :, LANES:AUG] = ext
    k_abs = jnp.abs(k_bf16.astype(F32))
    for bb in range(nb):
        rows = slice(bb * MOBA_BLOCK, (bb + 1) * MOBA_BLOCK)
        kstat_ref[0, 0, 0, bb:bb + 1, :] = jnp.mean(k[rows], axis=0, keepdims=True)
        kstat_ref[0, 0, 1, bb:bb + 1, :] = jnp.max(k_abs[rows], axis=0, keepdims=True)


def _moba_projection_specs(b, s, tm):
    nblk = s // MOBA_BLOCK
    nb = tm // MOBA_BLOCK
    specs = [
        pl.BlockSpec((1, N_PAIRS, nb, LANES, MOBA_BLOCK), lambda bb, i: (bb, 0, i, 0, 0)),
        pl.BlockSpec((1, N_PAIRS, tm, AUG), lambda bb, i: (bb, 0, i, 0)),
        pl.BlockSpec((1, N_PAIRS, nb, HEAD_PAIR * V_ROWS, MOBA_BLOCK),
                     lambda bb, i: (bb, 0, i, 0, 0)),
        pl.BlockSpec((1, 1, 2, nb, ATTN_WIDTH), lambda bb, i: (bb, i, 0, 0, 0)),
        pl.BlockSpec((1, MEM_WIDTH, tm), lambda bb, i: (bb, 0, i)),
        pl.BlockSpec((1, tm, BRANCH_WIDTH), lambda bb, i: (bb, i, 0)),
    ]
    shapes = [
        jax.ShapeDtypeStruct((b, N_PAIRS, nblk, LANES, MOBA_BLOCK), BF16),
        jax.ShapeDtypeStruct((b, N_PAIRS, s, AUG), BF16),
        jax.ShapeDtypeStruct((b, N_PAIRS, nblk, HEAD_PAIR * V_ROWS, MOBA_BLOCK), BF16),
        jax.ShapeDtypeStruct((b, s // tm, 2, nb, ATTN_WIDTH), F32),
        jax.ShapeDtypeStruct((b, MEM_WIDTH, s), BF16),
        jax.ShapeDtypeStruct((b, s, BRANCH_WIDTH), BF16),
    ]
    return specs, shapes


def _swa_projection(h, wT_ref, w_ref, qT_ref, k_ref, vT_ref, qmT_ref, z_ref):
    outT = lax.dot_general(wT_ref[...], h, _NT, preferred_element_type=F32)
    out = jnp.dot(h, w_ref[...], preferred_element_type=F32)
    tm = h.shape[0]
    qT_ref[0] = outT[:ATTN_WIDTH].astype(BF16)
    for g in range(N_KV_HEADS_B):
        v0 = ATTN_WIDTH + g * HEAD_DIM
        vT_ref[0, g * V_ROWS:g * V_ROWS + HEAD_DIM] = outT[v0:v0 + HEAD_DIM].astype(BF16)
        vT_ref[0, g * V_ROWS + HEAD_DIM:(g + 1) * V_ROWS] = jnp.ones((ONES_ROWS, tm), BF16)
    qmT_ref[0] = outT[ATTN_WIDTH + KV_WIDTH_B:].astype(BF16)
    k_ref[0] = out[:, :KV_WIDTH_B].astype(BF16)
    z_ref[0] = out[:, KV_WIDTH_B:].astype(BF16)


def _swa_projection_specs(b, s, tm):
    specs = [
        pl.BlockSpec((1, ATTN_WIDTH, tm), lambda bb, i: (bb, 0, i)),
        pl.BlockSpec((1, tm, KV_WIDTH_B), lambda bb, i: (bb, i, 0)),
        pl.BlockSpec((1, N_KV_HEADS_B * V_ROWS, tm), lambda bb, i: (bb, 0, i)),
        pl.BlockSpec((1, MEM_WIDTH, tm), lambda bb, i: (bb, 0, i)),
        pl.BlockSpec((1, tm, BRANCH_WIDTH), lambda bb, i: (bb, i, 0)),
    ]
    shapes = [
        jax.ShapeDtypeStruct((b, ATTN_WIDTH, s), BF16),
        jax.ShapeDtypeStruct((b, s, KV_WIDTH_B), BF16),
        jax.ShapeDtypeStruct((b, N_KV_HEADS_B * V_ROWS, s), BF16),
        jax.ShapeDtypeStruct((b, MEM_WIDTH, s), BF16),
        jax.ShapeDtypeStruct((b, s, BRANCH_WIDTH), BF16),
    ]
    return specs, shapes


_PROJECTIONS = {
    "moba": (_moba_projection, _moba_projection_specs),
    "swa": (_swa_projection, _swa_projection_specs),
}


def _split_in_weights(w_in, mixer):
    scale = HEAD_DIM ** -0.5 * LOG2E
    kvw = ATTN_WIDTH if mixer == "moba" else KV_WIDTH_B
    o = np.cumsum([0, ATTN_WIDTH, kvw, kvw, MEM_WIDTH, BRANCH_WIDTH])
    q, k, v, qm, z = (w_in[:, o[i]:o[i + 1]] for i in range(5))
    wT = jnp.concatenate([q * scale, v, qm * scale], axis=1).T.astype(BF16)
    w = jnp.concatenate([k, z], axis=1).astype(BF16)
    return wT, w


def _first_projection_kernel(x_ref, g_ref, wT_ref, w_ref, *out_refs, mixer):
    h = _rms_bf16(x_ref[0], g_ref[...])
    _PROJECTIONS[mixer][0](h, wT_ref, w_ref, *out_refs)


def _first_projection(x, g, w_in, mixer):
    b, s, d = x.shape
    tm = min(ROW_TILE, s)
    wT, w = _split_in_weights(w_in, mixer)
    specs, shapes = _PROJECTIONS[mixer][1](b, s, tm)
    return pl.pallas_call(
        functools.partial(_first_projection_kernel, mixer=mixer),
        grid=(b, s // tm),
        in_specs=[
            pl.BlockSpec((1, tm, d), lambda bb, i: (bb, i, 0)),
            pl.BlockSpec((1, d), lambda bb, i: (0, 0)),
            pl.BlockSpec(wT.shape, lambda bb, i: (0, 0)),
            pl.BlockSpec(w.shape, lambda bb, i: (0, 0)),
        ],
        out_specs=specs,
        out_shape=shapes,
        compiler_params=_params("parallel", "arbitrary"),
        name=f"in_proj_{mixer}",
    )(x, g.reshape(1, d), wT, w)


def _memory_attention_T(qmT, mk_ref, mvT_ref):
    sT = [jnp.dot(mk_ref[0, hh], qmT, preferred_element_type=F32)
          for hh in range(N_MEM_HEADS)]
    outs = []
    for hh in range(N_MEM_HEADS):
        p = jnp.exp2(sT[hh] - jnp.max(sT[hh], axis=0, keepdims=True))
        acc = jnp.dot(mvT_ref[0, hh * V_ROWS:(hh + 1) * V_ROWS, :], p.astype(BF16),
                      preferred_element_type=F32)
        outs.append(acc[:HEAD_DIM] * (1.0 / acc[HEAD_DIM:HEAD_DIM + 1]))
    return jnp.concatenate(outs, axis=0)


def _epilogue_kernel(y_ref, qmT_ref, z_ref, x_ref, mk_ref, mvT_ref, wout_ref, g_ref, *rest, mixer):
    ymem = _memory_attention_T(qmT_ref[0], mk_ref, mvT_ref).T
    y = jnp.concatenate([y_ref[0].astype(F32), ymem], axis=-1)
    z = z_ref[0].astype(F32)
    gated = (y * (z / (1.0 + jnp.exp(-z)))).astype(BF16)
    xn = x_ref[0] + jnp.dot(gated, wout_ref[...], preferred_element_type=F32)
    if mixer is None:
        (o_ref,) = rest
        ms = jnp.mean(xn * xn, axis=-1, keepdims=True)
        o_ref[0] = xn * lax.rsqrt(ms + RMS_EPS) * g_ref[...]
    else:
        wT_ref, w_ref, xo_ref = rest[:3]
        xo_ref[0] = xn
        _PROJECTIONS[mixer][0](_rms_bf16(xn, g_ref[...]), wT_ref, w_ref, *rest[3:])


def _epilogue(y_self, qmT, z, x, mem_k, mem_vT, layer, w_out, g, w_in_next=None, mixer=None):
    b, s, d = x.shape
    tm = min(ROW_TILE, s)
    m = mem_k.shape[3]
    in_specs = [
        pl.BlockSpec((1, tm, ATTN_WIDTH), lambda bb, i: (bb, i, 0)),
        pl.BlockSpec((1, MEM_WIDTH, tm), lambda bb, i: (bb, 0, i)),
        pl.BlockSpec((1, tm, BRANCH_WIDTH), lambda bb, i: (bb, i, 0)),
        pl.BlockSpec((1, tm, d), lambda bb, i: (bb, i, 0)),
        pl.BlockSpec((1, N_MEM_HEADS, m, MEM_WIDTH), lambda bb, i: (bb, 0, 0, 0)),
        pl.BlockSpec((1, N_MEM_HEADS * V_ROWS, m), lambda bb, i: (bb, 0, 0)),
        pl.BlockSpec((BRANCH_WIDTH, d), lambda bb, i: (0, 0)),
        pl.BlockSpec((1, d), lambda bb, i: (0, 0)),
    ]
    args = [y_self, qmT, z, x, mem_k[layer], mem_vT[layer], w_out.astype(BF16), g.reshape(1, d)]
    x_spec = pl.BlockSpec((1, tm, d), lambda bb, i: (bb, i, 0))
    x_shape = jax.ShapeDtypeStruct((b, s, d), F32)
    if mixer is None:
        out_specs, out_shape = [x_spec], [x_shape]
    else:
        wT, w = _split_in_weights(w_in_next, mixer)
        in_specs += [pl.BlockSpec(wT.shape, lambda bb, i: (0, 0)),
                     pl.BlockSpec(w.shape, lambda bb, i: (0, 0))]
        args += [wT, w]
        specs, shapes = _PROJECTIONS[mixer][1](b, s, tm)
        out_specs, out_shape = [x_spec] + specs, [x_shape] + shapes
    return pl.pallas_call(
        functools.partial(_epilogue_kernel, mixer=mixer),
        grid=(b, s // tm),
        in_specs=in_specs,
        out_specs=out_specs,
        out_shape=out_shape,
        compiler_params=_params("parallel", "arbitrary"),
        name=f"epilogue_{mixer}",
    )(*args)


def _moba_kernel(qT_ref, kaug_ref, vT_ref, kmean_ref, kabs_ref, srows_ref, slope_ref, o_ref,
                 qaug_ref):
    nblk = kmean_ref.shape[1]
    tq = qT_ref.shape[-1]
    kmean = kmean_ref[0]
    drow = lax.broadcasted_iota(jnp.int32, (LANES, tq), 0)
    klane = lax.broadcasted_iota(jnp.int32, kmean.shape, 1)
    blk = lax.broadcasted_iota(jnp.int32, (nblk, tq), 0)
    krow = lax.broadcasted_iota(jnp.int32, (MOBA_BLOCK, tq), 0)
    qlane = lax.broadcasted_iota(jnp.int32, (MOBA_BLOCK, tq), 1)
    causal = krow <= qlane
    head_lanes = [(klane >= hh * HEAD_DIM) & (klane < (hh + 1) * HEAD_DIM) for hh in range(HEAD_PAIR)]
    head_rows = [(drow >= hh * HEAD_DIM) & (drow < (hh + 1) * HEAD_DIM) for hh in range(HEAD_PAIR)]
    no_choice = jnp.zeros((SEL_COLS, tq), BF16)

    def weighted_values(n, hh, p):
        return jnp.dot(vT_ref[0, 0, n, hh * V_ROWS:(hh + 1) * V_ROWS, :], p.astype(BF16),
                       preferred_element_type=F32)

    def set_up(t):
        j = pl.program_id(2) * Q_BLOCKS_PER_STEP + t
        qT = qT_ref[0, 0, t]
        q_heads = [jnp.where(head_rows[hh], qT, jnp.zeros_like(qT)) for hh in range(HEAD_PAIR)]
        gates = [jnp.dot(jnp.where(head_lanes[hh], kmean, 0.0).astype(BF16), qT,
                         preferred_element_type=F32) for hh in range(HEAD_PAIR)]
        own_sT = [jnp.dot(kaug_ref[0, 0, j],
                          jnp.concatenate([q_heads[hh], no_choice, srows_ref[0, hh]], axis=0),
                          preferred_element_type=F32) for hh in range(HEAD_PAIR)]
        q_abs = jnp.abs(qT)
        qk_bound = [jnp.dot(jnp.where(head_lanes[hh], kabs_ref[0] * BF16_ROUND_UP, 0.0).astype(BF16),
                            q_abs, preferred_element_type=F32) for hh in range(HEAD_PAIR)]

        own_max, own_acc = [], []
        for hh in range(HEAD_PAIR):
            sT = jnp.where(causal, own_sT[hh], MASKED)
            m = jnp.max(sT, axis=0, keepdims=True)
            own_max.append(m)
            own_acc.append(weighted_values(j, hh, jnp.exp2(sT - m)))

        blocks_back = (j - blk).astype(F32)
        needed = blk < 0
        for hh in range(HEAD_PAIR):
            alibi = slope_ref[0, hh] * ((MOBA_BLOCK - 1) - MOBA_BLOCK * blocks_back)
            log2_weight_bound = qk_bound[hh] - own_max[hh] + alibi + BOUND_SLACK
            needed = needed | ((log2_weight_bound >= EXP2_IS_ZERO_BELOW) & (blk < j))
        first_needed = jnp.min(jnp.where(needed, blk, j).astype(F32)).astype(jnp.int32)

        for hh in range(HEAD_PAIR):
            gate = jnp.where(blk < j, gates[hh], -jnp.inf)
            bias = jnp.full((nblk, tq), MASKED, F32)
            for _ in range(MOBA_TOPK):
                best = jnp.max(gate, axis=0, keepdims=True)
                first = jnp.min(jnp.where(gate == best, blk, nblk), axis=0, keepdims=True)
                pick = blk == first
                chosen = jnp.where(best > -jnp.inf, 0.0, MASKED)
                bias = jnp.where(pick, jnp.maximum(bias, chosen), bias)
                gate = jnp.where(pick, -jnp.inf, gate)
            bias = bias.astype(BF16)
            if nblk < SEL_COLS:
                bias = jnp.concatenate([bias, jnp.zeros((SEL_COLS - nblk, tq), BF16)], axis=0)
            qaug_ref[t, hh] = jnp.concatenate([q_heads[hh], bias, srows_ref[0, hh]], axis=0)
        return j, own_max, own_acc, first_needed

    def attend_past(t, j, own_max, own_acc, first_needed):
        def scores(n, hh):
            return jnp.dot(kaug_ref[0, 0, n], qaug_ref[t, hh], preferred_element_type=F32)

        def block_offset(n, hh):
            return slope_ref[0, hh] * ((n - j) * MOBA_BLOCK).astype(F32)

        def finish(accs):
            oT = jnp.concatenate(
                [acc[:HEAD_DIM] * (1.0 / acc[HEAD_DIM:HEAD_DIM + 1]) for acc in accs], axis=0)
            o_ref[0, t * tq:(t + 1) * tq, :] = oT.T.astype(o_ref.dtype)

        def fixed_shift_blocks(first, count, accs):
            slots = [first + u for u in range(count)]
            blocks = [jnp.minimum(n, j) for n in slots]
            sT = [[scores(n, hh) for hh in range(HEAD_PAIR)] for n in blocks]
            accs = list(accs)
            for u, n in enumerate(blocks):
                for hh in range(HEAD_PAIR):
                    shift = jnp.where(slots[u] < j, own_max[hh] - block_offset(n, hh), jnp.inf)
                    accs[hh] += weighted_values(n, hh, jnp.exp2(sT[u][hh] - shift))
            return accs

        fast, start = own_acc, first_needed
        for unroll in TRIP_SIZES:
            left = j - start
            trips = (left + 1) // unroll if unroll == MASKED_TRIP else left // unroll
            fast = lax.fori_loop(
                0, trips,
                lambda g, a, start=start, unroll=unroll: fixed_shift_blocks(start + g * unroll, unroll, a),
                fast)
            start = jnp.minimum(start + trips * unroll, j)
        finish(fast)
        denominators = jnp.maximum(fast[0][HEAD_DIM:HEAD_DIM + 1], fast[1][HEAD_DIM:HEAD_DIM + 1])
        in_range = jnp.max(denominators) < OVERFLOW_GUARD

        @pl.when(jnp.logical_not(in_range))
        def _():
            def running_max_block(n, carry):
                out = []
                for hh in range(HEAD_PAIR):
                    m, acc = carry[2 * hh:2 * hh + 2]
                    sT = scores(n, hh)
                    c = block_offset(n, hh)
                    m_new = jnp.maximum(m, jnp.max(sT, axis=0, keepdims=True) + c)
                    acc = (jnp.exp2(m - m_new) * acc
                           + weighted_values(n, hh, jnp.exp2(sT - (m_new - c))))
                    out += [m_new, acc]
                return out

            slow = lax.fori_loop(0, j, running_max_block,
                                 [x for hh in range(HEAD_PAIR) for x in (own_max[hh], own_acc[hh])])
            finish(slow[1::2])

    prepared = [set_up(t) for t in range(Q_BLOCKS_PER_STEP)]
    for t in range(Q_BLOCKS_PER_STEP):
        attend_past(t, *prepared[t])


def _moba_tables(tq):
    slopes = _alibi_slopes(N_HEADS).astype(np.float64) * LOG2E
    pieces = _bf16_pieces(slopes)
    rows = np.zeros((N_HEADS, AUG - LANES - SEL_COLS, tq), np.float32)
    for c, piece in enumerate(pieces):
        rows[:, c, :] = piece[:, None]
    srows = jnp.asarray(rows.reshape(N_PAIRS, HEAD_PAIR, AUG - LANES - SEL_COLS, tq), BF16)
    svec = jnp.asarray(np.broadcast_to(slopes[:, None, None], (N_HEADS, 1, tq))
                       .reshape(N_PAIRS, HEAD_PAIR, 1, tq), F32)
    return srows, svec


def _moba_attention(qT, kaug, vT, kstat):
    b, _, nblk, _, tq = qT.shape
    assert nblk <= SEL_COLS and tq == MOBA_BLOCK and nblk % Q_BLOCKS_PER_STEP == 0
    s = nblk * MOBA_BLOCK
    kaug = kaug.reshape(b, N_PAIRS, nblk, MOBA_BLOCK, AUG)
    kmean = kstat[:, :, 0].reshape(b, nblk, ATTN_WIDTH)
    kabs = kstat[:, :, 1].reshape(b, nblk, ATTN_WIDTH)
    srows, svec = _moba_tables(tq)
    return pl.pallas_call(
        _moba_kernel,
        grid=(b, N_PAIRS, nblk // Q_BLOCKS_PER_STEP),
        in_specs=[
            pl.BlockSpec((1, 1, Q_BLOCKS_PER_STEP, LANES, tq), lambda bb, p, j: (bb, p, j, 0, 0)),
            pl.BlockSpec((1, 1, nblk, MOBA_BLOCK, AUG), lambda bb, p, j: (bb, p, 0, 0, 0)),
            pl.BlockSpec((1, 1, nblk, HEAD_PAIR * V_ROWS, MOBA_BLOCK),
                         lambda bb, p, j: (bb, p, 0, 0, 0)),
            pl.BlockSpec((1, nblk, LANES), lambda bb, p, j: (bb, 0, p)),
            pl.BlockSpec((1, nblk, LANES), lambda bb, p, j: (bb, 0, p)),
            pl.BlockSpec((1, HEAD_PAIR, AUG - LANES - SEL_COLS, tq), lambda bb, p, j: (p, 0, 0, 0)),
            pl.BlockSpec((1, HEAD_PAIR, 1, tq), lambda bb, p, j: (p, 0, 0, 0)),
        ],
        out_specs=pl.BlockSpec((1, Q_BLOCKS_PER_STEP * tq, LANES), lambda bb, p, j: (bb, j, p)),
        out_shape=jax.ShapeDtypeStruct((b, s, ATTN_WIDTH), BF16),
        scratch_shapes=[pltpu.VMEM((Q_BLOCKS_PER_STEP, HEAD_PAIR, AUG, tq), BF16)],
        compiler_params=_params("parallel", "parallel", "arbitrary"),
        name="moba_attention",
    )(qT, kaug, vT, kmean, kabs, srows, svec)


def _swa_kernel(qT_ref, kp_ref, kc_ref, vTp_ref, vTc_ref, bias_ref, sink_ref, o_ref, yT_ref):
    tq = qT_ref.shape[-1]
    kwin = jnp.concatenate([kp_ref[0], kc_ref[0]], axis=0)
    vTwin = jnp.concatenate([vTp_ref[0], vTc_ref[0]], axis=1)
    zeros = jnp.zeros((HEAD_DIM, tq), BF16)
    group = N_HEADS // N_KV_HEADS_B

    def padded_q(h):
        q_h = qT_ref[0, h * HEAD_DIM:(h + 1) * HEAD_DIM, :]
        return jnp.concatenate([q_h, zeros] if h // group == 0 else [zeros, q_h], axis=0)

    worst = jnp.zeros((1, group * tq), F32)
    for g in range(N_KV_HEADS_B):
        heads = range(g * group, (g + 1) * group)
        sink = jnp.concatenate([sink_ref[h] * LOG2E for h in heads], axis=1)
        s_g = jnp.dot(kwin, jnp.concatenate([padded_q(h) for h in heads], axis=1),
                      preferred_element_type=F32)
        p = jnp.exp2(s_g + bias_ref[0, g] - sink).astype(BF16)
        acc = jnp.dot(vTwin[g * V_ROWS:(g + 1) * V_ROWS, :], p, preferred_element_type=F32)
        denom = acc[HEAD_DIM:HEAD_DIM + 1] + 1.0
        worst = jnp.maximum(worst, denom)
        out = acc[:HEAD_DIM] * (1.0 / denom)
        for i, h in enumerate(heads):
            yT_ref[h * HEAD_DIM:(h + 1) * HEAD_DIM, :] = out[:, i * tq:(i + 1) * tq]
    o_ref[0] = yT_ref[...].T.astype(o_ref.dtype)
    in_range = jnp.max(worst) < OVERFLOW_GUARD

    @pl.when(jnp.logical_not(in_range))
    def _():
        for h in range(N_HEADS):
            g, i = divmod(h, group)
            s_h = (jnp.dot(kwin, padded_q(h), preferred_element_type=F32)
                   + bias_ref[0, g, :, i * tq:(i + 1) * tq])
            sink = sink_ref[h] * LOG2E
            m = jnp.maximum(jnp.max(s_h, axis=0, keepdims=True), sink)
            acc = jnp.dot(vTwin[g * V_ROWS:(g + 1) * V_ROWS, :], jnp.exp2(s_h - m).astype(BF16),
                          preferred_element_type=F32)
            denom = acc[HEAD_DIM:HEAD_DIM + 1] + jnp.exp2(sink - m)
            yT_ref[h * HEAD_DIM:(h + 1) * HEAD_DIM, :] = acc[:HEAD_DIM] * (1.0 / denom)
        o_ref[0] = yT_ref[...].T.astype(o_ref.dtype)


def _swa_bias_table(tq):
    slopes = _alibi_slopes(N_HEADS).astype(np.float64) * LOG2E
    kw = np.arange(WINDOW + tq)[:, None]
    q = np.arange(tq)[None, :]
    dist = q + WINDOW - kw
    ok = (dist >= 0) & (dist < WINDOW)
    first_tile = ok & (kw >= WINDOW)
    table = np.stack([np.where(valid[None], -slopes[:, None, None] * dist[None], MASKED)
                      for valid in (first_tile, ok)])
    table = table.reshape(2, N_KV_HEADS_B, N_HEADS // N_KV_HEADS_B, WINDOW + tq, tq)
    table = np.moveaxis(table, 2, 3).reshape(2, N_KV_HEADS_B, WINDOW + tq, -1)
    return jnp.asarray(table, F32)


def _swa_attention(qT, k, vT, sinks):
    b, _, s = qT.shape
    tq = min(SWA_TILE, s)
    r = tq // WINDOW
    bias = _swa_bias_table(tq)
    sink_rows = jnp.broadcast_to(sinks.astype(F32)[:, None, None], (N_HEADS, 1, tq))
    prev = lambda j: jnp.maximum(j * r - 1, 0)
    return pl.pallas_call(
        _swa_kernel,
        grid=(b, s // tq),
        in_specs=[
            pl.BlockSpec((1, ATTN_WIDTH, tq), lambda bb, j: (bb, 0, j)),
            pl.BlockSpec((1, WINDOW, KV_WIDTH_B), lambda bb, j: (bb, prev(j), 0)),
            pl.BlockSpec((1, tq, KV_WIDTH_B), lambda bb, j: (bb, j, 0)),
            pl.BlockSpec((1, N_KV_HEADS_B * V_ROWS, WINDOW), lambda bb, j: (bb, 0, prev(j))),
            pl.BlockSpec((1, N_KV_HEADS_B * V_ROWS, tq), lambda bb, j: (bb, 0, j)),
            pl.BlockSpec((1,) + bias.shape[1:], lambda bb, j: (jnp.minimum(j, 1), 0, 0, 0)),
            pl.BlockSpec(sink_rows.shape, lambda bb, j: (0, 0, 0)),
        ],
        out_specs=pl.BlockSpec((1, tq, ATTN_WIDTH), lambda bb, j: (bb, j, 0)),
        out_shape=jax.ShapeDtypeStruct((b, s, ATTN_WIDTH), BF16),
        scratch_shapes=[pltpu.VMEM((ATTN_WIDTH, tq), F32)],
        compiler_params=_params("parallel", "arbitrary"),
        name="swa_attention",
    )(qT, k, k, vT, vT, bias, sink_rows)


def kernel(x, mem, norm_g, w_in_a, w_in_b, sinks_b, w_mem_kv, w_out, mem_norm_g, final_norm_g):
    depth = norm_g.shape[0]
    b, s, _ = x.shape
    assert s % ROW_TILE == 0 or s < ROW_TILE
    mem_k, mem_vT = _mem_kv(mem, mem_norm_g, w_mem_kv)
    mixers = ["moba" if i % 2 == 0 else "swa" for i in range(depth)]
    w_in = [w_in_a[i // 2] if i % 2 == 0 else w_in_b[i // 2] for i in range(depth)]
    proj = _first_projection(x, norm_g[0], w_in[0], mixers[0])
    for i in range(depth):
        if mixers[i] == "moba":
            qT, kaug, vT, kstat, qmT, z = proj
            y_self = _moba_attention(qT, kaug, vT, kstat)
        else:
            qT, k, vT, qmT, z = proj
            y_self = _swa_attention(qT, k, vT, sinks_b[i // 2])
        if i + 1 < depth:
            x, *proj = _epilogue(y_self, qmT, z, x, mem_k, mem_vT, i, w_out[i], norm_g[i + 1],
                                 w_in[i + 1], mixers[i + 1])
        else:
            (x,) = _epilogue(y_self, qmT, z, x, mem_k, mem_vT, i, w_out[i], final_norm_g)
    return x
```

```python
import functools
import math

import jax
import jax.numpy as jnp
import numpy as np
from jax import lax
from jax.experimental import pallas as pl
from jax.experimental.pallas import tpu as pltpu

HEAD_DIM = 64
N_HEADS = 12
N_KV_HEADS_B = 2
N_MEM_HEADS = 4
ATTN_WIDTH = N_HEADS * HEAD_DIM
MEM_WIDTH = N_MEM_HEADS * HEAD_DIM
BRANCH_WIDTH = ATTN_WIDTH + MEM_WIDTH
KV_WIDTH_B = N_KV_HEADS_B * HEAD_DIM
MOBA_BLOCK = 256
MOBA_TOPK = 3
WINDOW = 128
RMS_EPS = 1e-6

LANES = 128
HEAD_PAIR = LANES // HEAD_DIM
N_PAIRS = N_HEADS // HEAD_PAIR
SEL_COLS = 64
ALIBI_COLS = 4
ONES_ROWS = 16
V_ROWS = HEAD_DIM + ONES_ROWS
LOG2E = math.log2(math.e)
AUG = 2 * LANES
MASKED = -(2.0 ** 30)
OVERFLOW_GUARD = 2.0 ** 100
EXP2_IS_ZERO_BELOW = -150.0
BOUND_SLACK = 1.0
BF16_ROUND_UP = 1.0 + 2.0 ** -7
TRIP_SIZES = (16, 4, 2, 1)
MASKED_TRIP = 4
Q_BLOCKS_PER_STEP = 2
ROW_TILE = 512
SWA_TILE = 256
VMEM_LIMIT = 56 * 1024 * 1024

F32 = jnp.float32
BF16 = jnp.bfloat16
_NT = (((1,), (1,)), ((), ()))


def _alibi_slopes(n_heads):
    def pow2_slopes(n):
        start = 2.0 ** (-8.0 / n)
        return [start ** (i + 1) for i in range(n)]
    if math.log2(n_heads).is_integer():
        vals = pow2_slopes(n_heads)
    else:
        c = 2 ** math.floor(math.log2(n_heads))
        vals = pow2_slopes(c) + pow2_slopes(2 * c)[0::2][: n_heads - c]
    return np.array(vals, dtype=np.float32)


def _bf16_pieces(v):
    rest = np.asarray(v, np.float64)
    pieces = []
    for _ in range(ALIBI_COLS):
        p = rest.astype(BF16).astype(np.float64)
        pieces.append(p.astype(np.float32))
        rest = rest - p
    assert np.all(np.abs(rest) <= np.abs(v) * 2.0 ** -30)
    return pieces


def _rms_bf16(x, g):
    ms = jnp.mean(x * x, axis=-1, keepdims=True)
    return (x * lax.rsqrt(ms + RMS_EPS) * g).astype(BF16)


def _params(*sem):
    return pltpu.CompilerParams(dimension_semantics=sem, vmem_limit_bytes=VMEM_LIMIT)


def _mem_kv_kernel(mem_ref, g_ref, wk_ref, wvT_ref, k_ref, vT_ref):
    h = _rms_bf16(mem_ref[0], g_ref[...])
    k = jnp.dot(h, wk_ref[0], preferred_element_type=F32)
    vT = lax.dot_general(wvT_ref[0], h, _NT, preferred_element_type=F32)
    lane = lax.broadcasted_iota(jnp.int32, k.shape, 1)
    for hh in range(N_MEM_HEADS):
        mine = (lane >= hh * HEAD_DIM) & (lane < (hh + 1) * HEAD_DIM)
        k_ref[0, 0, hh] = jnp.where(mine, k, 0.0).astype(BF16)
        vT_ref[0, 0, hh * V_ROWS:hh * V_ROWS + HEAD_DIM] = (
            vT[hh * HEAD_DIM:(hh + 1) * HEAD_DIM].astype(BF16))
        vT_ref[0, 0, hh * V_ROWS + HEAD_DIM:(hh + 1) * V_ROWS] = jnp.ones(
            (ONES_ROWS, vT.shape[1]), BF16)


def _mem_kv(mem, mem_norm_g, w_mem_kv):
    b, m, d = mem.shape
    depth = w_mem_kv.shape[0]
    wk = w_mem_kv[:, :, :MEM_WIDTH].astype(BF16)
    wvT = jnp.swapaxes(w_mem_kv[:, :, MEM_WIDTH:], 1, 2).astype(BF16)
    return pl.pallas_call(
        _mem_kv_kernel,
        grid=(depth, b),
        in_specs=[
            pl.BlockSpec((1, m, d), lambda i, bb: (bb, 0, 0)),
            pl.BlockSpec((1, d), lambda i, bb: (0, 0)),
            pl.BlockSpec((1, d, MEM_WIDTH), lambda i, bb: (i, 0, 0)),
            pl.BlockSpec((1, MEM_WIDTH, d), lambda i, bb: (i, 0, 0)),
        ],
        out_specs=[
            pl.BlockSpec((1, 1, N_MEM_HEADS, m, MEM_WIDTH), lambda i, bb: (i, bb, 0, 0, 0)),
            pl.BlockSpec((1, 1, N_MEM_HEADS * V_ROWS, m), lambda i, bb: (i, bb, 0, 0)),
        ],
        out_shape=[
            jax.ShapeDtypeStruct((depth, b, N_MEM_HEADS, m, MEM_WIDTH), BF16),
            jax.ShapeDtypeStruct((depth, b, N_MEM_HEADS * V_ROWS, m), BF16),
        ],
        compiler_params=_params("arbitrary", "arbitrary"),
        name="mem_kv",
    )(mem, mem_norm_g.reshape(1, d), wk, wvT)


def _moba_projection(h, wT_ref, w_ref, qT_ref, kaug_ref, vT_ref, kstat_ref, qmT_ref, z_ref):
    tm = h.shape[0]
    nb = tm // MOBA_BLOCK
    outT = lax.dot_general(wT_ref[...], h, _NT, preferred_element_type=F32)
    out = jnp.dot(h, w_ref[...], preferred_element_type=F32)
    ones = jnp.ones((ONES_ROWS, MOBA_BLOCK), BF16)
    for p in range(N_PAIRS):
        for bb in range(nb):
            cols = slice(bb * MOBA_BLOCK, (bb + 1) * MOBA_BLOCK)
            qT_ref[0, p, bb] = outT[p * LANES:(p + 1) * LANES, cols].astype(BF16)
            for hh in range(HEAD_PAIR):
                v0 = ATTN_WIDTH + (p * HEAD_PAIR + hh) * HEAD_DIM
                vT_ref[0, p, bb, hh * V_ROWS:hh * V_ROWS + HEAD_DIM] = (
                    outT[v0:v0 + HEAD_DIM, cols].astype(BF16))
                vT_ref[0, p, bb, hh * V_ROWS + HEAD_DIM:(hh + 1) * V_ROWS] = ones
    qmT_ref[0] = outT[2 * ATTN_WIDTH:, :].astype(BF16)
    k = out[:, :ATTN_WIDTH]
    z_ref[0] = out[:, ATTN_WIDTH:].astype(BF16)
    row = lax.broadcasted_iota(jnp.int32, (tm, LANES), 0)
    lane = lax.broadcasted_iota(jnp.int32, (tm, LANES), 1)
    blk = pl.program_id(1) * nb + row // MOBA_BLOCK
    s_rel = (row % MOBA_BLOCK).astype(F32)
    ext = jnp.where(lane < SEL_COLS, (lane == blk).astype(F32),
                    jnp.where(lane < SEL_COLS + ALIBI_COLS, s_rel, 0.0)).astype(BF16)
    k_bf16 = k.astype(BF16)
    for p in range(N_PAIRS):
        kaug_ref[0, p, :, 0:LANES] = k_bf16[:, p * LANES:(p + 1) * LANES]
        kaug_ref[0, p, :, LANES:AUG] = ext
    k_abs = jnp.abs(k_bf16.astype(F32))
    for bb in range(nb):
        rows = slice(bb * MOBA_BLOCK, (bb + 1) * MOBA_BLOCK)
        kstat_ref[0, 0, 0, bb:bb + 1, :] = jnp.mean(k[rows], axis=0, keepdims=True)
        kstat_ref[0, 0, 1, bb:bb + 1, :] = jnp.max(k_abs[rows], axis=0, keepdims=True)


def _moba_projection_specs(b, s, tm):
    nblk = s // MOBA_BLOCK
    nb = tm // MOBA_BLOCK
    specs = [
        pl.BlockSpec((1, N_PAIRS, nb, LANES, MOBA_BLOCK), lambda bb, i: (bb, 0, i, 0, 0)),
        pl.BlockSpec((1, N_PAIRS, tm, AUG), lambda bb, i: (bb, 0, i, 0)),
        pl.BlockSpec((1, N_PAIRS, nb, HEAD_PAIR * V_ROWS, MOBA_BLOCK),
                     lambda bb, i: (bb, 0, i, 0, 0)),
        pl.BlockSpec((1, 1, 2, nb, ATTN_WIDTH), lambda bb, i: (bb, i, 0, 0, 0)),
        pl.BlockSpec((1, MEM_WIDTH, tm), lambda bb, i: (bb, 0, i)),
        pl.BlockSpec((1, tm, BRANCH_WIDTH), lambda bb, i: (bb, i, 0)),
    ]
    shapes = [
        jax.ShapeDtypeStruct((b, N_PAIRS, nblk, LANES, MOBA_BLOCK), BF16),
        jax.ShapeDtypeStruct((b, N_PAIRS, s, AUG), BF16),
        jax.ShapeDtypeStruct((b, N_PAIRS, nblk, HEAD_PAIR * V_ROWS, MOBA_BLOCK), BF16),
        jax.ShapeDtypeStruct((b, s // tm, 2, nb, ATTN_WIDTH), F32),
        jax.ShapeDtypeStruct((b, MEM_WIDTH, s), BF16),
        jax.ShapeDtypeStruct((b, s, BRANCH_WIDTH), BF16),
    ]
    return specs, shapes


def _swa_projection(h, wT_ref, w_ref, qT_ref, k_ref, vT_ref, qmT_ref, z_ref):
    outT = lax.dot_general(wT_ref[...], h, _NT, preferred_element_type=F32)
    out = jnp.dot(h, w_ref[...], preferred_element_type=F32)
    tm = h.shape[0]
    qT_ref[0] = outT[:ATTN_WIDTH].astype(BF16)
    for g in range(N_KV_HEADS_B):
        v0 = ATTN_WIDTH + g * HEAD_DIM
        vT_ref[0, g * V_ROWS:g * V_ROWS + HEAD_DIM] = outT[v0:v0 + HEAD_DIM].astype(BF16)
        vT_ref[0, g * V_ROWS + HEAD_DIM:(g + 1) * V_ROWS] = jnp.ones((ONES_ROWS, tm), BF16)
    qmT_ref[0] = outT[ATTN_WIDTH + KV_WIDTH_B:].astype(BF16)
    k_ref[0] = out[:, :KV_WIDTH_B].astype(BF16)
    z_ref[0] = out[:, KV_WIDTH_B:].astype(BF16)


def _swa_projection_specs(b, s, tm):
    specs = [
        pl.BlockSpec((1, ATTN_WIDTH, tm), lambda bb, i: (bb, 0, i)),
        pl.BlockSpec((1, tm, KV_WIDTH_B), lambda bb, i: (bb, i, 0)),
        pl.BlockSpec((1, N_KV_HEADS_B * V_ROWS, tm), lambda bb, i: (bb, 0, i)),
        pl.BlockSpec((1, MEM_WIDTH, tm), lambda bb, i: (bb, 0, i)),
        pl.BlockSpec((1, tm, BRANCH_WIDTH), lambda bb, i: (bb, i, 0)),
    ]
    shapes = [
        jax.ShapeDtypeStruct((b, ATTN_WIDTH, s), BF16),
        jax.ShapeDtypeStruct((b, s, KV_WIDTH_B), BF16),
        jax.ShapeDtypeStruct((b, N_KV_HEADS_B * V_ROWS, s), BF16),
        jax.ShapeDtypeStruct((b, MEM_WIDTH, s), BF16),
        jax.ShapeDtypeStruct((b, s, BRANCH_WIDTH), BF16),
    ]
    return specs, shapes


_PROJECTIONS = {
    "moba": (_moba_projection, _moba_projection_specs),
    "swa": (_swa_projection, _swa_projection_specs),
}


def _split_in_weights(w_in, mixer):
    scale = HEAD_DIM ** -0.5 * LOG2E
    kvw = ATTN_WIDTH if mixer == "moba" else KV_WIDTH_B
    o = np.cumsum([0, ATTN_WIDTH, kvw, kvw, MEM_WIDTH, BRANCH_WIDTH])
    q, k, v, qm, z = (w_in[:, o[i]:o[i + 1]] for i in range(5))
    wT = jnp.concatenate([q * scale, v, qm * scale], axis=1).T.astype(BF16)
    w = jnp.concatenate([k, z], axis=1).astype(BF16)
    return wT, w


def _first_projection_kernel(x_ref, g_ref, wT_ref, w_ref, *out_refs, mixer):
    h = _rms_bf16(x_ref[0], g_ref[...])
    _PROJECTIONS[mixer][0](h, wT_ref, w_ref, *out_refs)


def _first_projection(x, g, w_in, mixer):
    b, s, d = x.shape
    tm = min(ROW_TILE, s)
    wT, w = _split_in_weights(w_in, mixer)
    specs, shapes = _PROJECTIONS[mixer][1](b, s, tm)
    return pl.pallas_call(
        functools.partial(_first_projection_kernel, mixer=mixer),
        grid=(b, s // tm),
        in_specs=[
            pl.BlockSpec((1, tm, d), lambda bb, i: (bb, i, 0)),
            pl.BlockSpec((1, d), lambda bb, i: (0, 0)),
            pl.BlockSpec(wT.shape, lambda bb, i: (0, 0)),
            pl.BlockSpec(w.shape, lambda bb, i: (0, 0)),
        ],
        out_specs=specs,
        out_shape=shapes,
        compiler_params=_params("parallel", "arbitrary"),
        name=f"in_proj_{mixer}",
    )(x, g.reshape(1, d), wT, w)


def _memory_attention_T(qmT, mk_ref, mvT_ref):
    sT = [jnp.dot(mk_ref[0, hh], qmT, preferred_element_type=F32)
          for hh in range(N_MEM_HEADS)]
    outs = []
    for hh in range(N_MEM_HEADS):
        p = jnp.exp2(sT[hh] - jnp.max(sT[hh], axis=0, keepdims=True))
        acc = jnp.dot(mvT_ref[0, hh * V_ROWS:(hh + 1) * V_ROWS, :], p.astype(BF16),
                      preferred_element_type=F32)
        outs.append(acc[:HEAD_DIM] * (1.0 / acc[HEAD_DIM:HEAD_DIM + 1]))
    return jnp.concatenate(outs, axis=0)


def _epilogue_kernel(y_ref, qmT_ref, z_ref, x_ref, mk_ref, mvT_ref, wout_ref, g_ref, *rest, mixer):
    ymem = _memory_attention_T(qmT_ref[0], mk_ref, mvT_ref).T
    y = jnp.concatenate([y_ref[0].astype(F32), ymem], axis=-1)
    z = z_ref[0].astype(F32)
    gated = (y * (z / (1.0 + jnp.exp(-z)))).astype(BF16)
    xn = x_ref[0] + jnp.dot(gated, wout_ref[...], preferred_element_type=F32)
    if mixer is None:
        (o_ref,) = rest
        ms = jnp.mean(xn * xn, axis=-1, keepdims=True)
        o_ref[0] = xn * lax.rsqrt(ms + RMS_EPS) * g_ref[...]
    else:
        wT_ref, w_ref, xo_ref = rest[:3]
        xo_ref[0] = xn
        _PROJECTIONS[mixer][0](_rms_bf16(xn, g_ref[...]), wT_ref, w_ref, *rest[3:])


def _epilogue(y_self, qmT, z, x, mem_k, mem_vT, layer, w_out, g, w_in_next=None, mixer=None):
    b, s, d = x.shape
    tm = min(ROW_TILE, s)
    m = mem_k.shape[3]
    in_specs = [
        pl.BlockSpec((1, tm, ATTN_WIDTH), lambda bb, i: (bb, i, 0)),
        pl.BlockSpec((1, MEM_WIDTH, tm), lambda bb, i: (bb, 0, i)),
        pl.BlockSpec((1, tm, BRANCH_WIDTH), lambda bb, i: (bb, i, 0)),
        pl.BlockSpec((1, tm, d), lambda bb, i: (bb, i, 0)),
        pl.BlockSpec((1, N_MEM_HEADS, m, MEM_WIDTH), lambda bb, i: (bb, 0, 0, 0)),
        pl.BlockSpec((1, N_MEM_HEADS * V_ROWS, m), lambda bb, i: (bb, 0, 0)),
        pl.BlockSpec((BRANCH_WIDTH, d), lambda bb, i: (0, 0)),
        pl.BlockSpec((1, d), lambda bb, i: (0, 0)),
    ]
    args = [y_self, qmT, z, x, mem_k[layer], mem_vT[layer], w_out.astype(BF16), g.reshape(1, d)]
    x_spec = pl.BlockSpec((1, tm, d), lambda bb, i: (bb, i, 0))
    x_shape = jax.ShapeDtypeStruct((b, s, d), F32)
    if mixer is None:
        out_specs, out_shape = [x_spec], [x_shape]
    else:
        wT, w = _split_in_weights(w_in_next, mixer)
        in_specs += [pl.BlockSpec(wT.shape, lambda bb, i: (0, 0)),
                     pl.BlockSpec(w.shape, lambda bb, i: (0, 0))]
        args += [wT, w]
        specs, shapes = _PROJECTIONS[mixer][1](b, s, tm)
        out_specs, out_shape = [x_spec] + specs, [x_shape] + shapes
    return pl.pallas_call(
        functools.partial(_epilogue_kernel, mixer=mixer),
        grid=(b, s // tm),
        in_specs=in_specs,
        out_specs=out_specs,
        out_shape=out_shape,
        compiler_params=_params("parallel", "arbitrary"),
        name=f"epilogue_{mixer}",
    )(*args)


def _moba_kernel(qT_ref, kaug_ref, vT_ref, kmean_ref, kabs_ref, srows_ref, slope_ref, o_ref,
                 qaug_ref):
    nblk = kmean_ref.shape[1]
    tq = qT_ref.shape[-1]
    kmean = kmean_ref[0]
    drow = lax.broadcasted_iota(jnp.int32, (LANES, tq), 0)
    klane = lax.broadcasted_iota(jnp.int32, kmean.shape, 1)
    blk = lax.broadcasted_iota(jnp.int32, (nblk, tq), 0)
    blk_f32 = blk.astype(F32)
    blk_col = lax.broadcasted_iota(jnp.int32, (nblk, 1), 0)
    krow = lax.broadcasted_iota(jnp.int32, (MOBA_BLOCK, tq), 0)
    qlane = lax.broadcasted_iota(jnp.int32, (MOBA_BLOCK, tq), 1)
    causal = krow <= qlane
    head_lanes = [(klane >= hh * HEAD_DIM) & (klane < (hh + 1) * HEAD_DIM) for hh in range(HEAD_PAIR)]
    head_rows = [(drow >= hh * HEAD_DIM) & (drow < (hh + 1) * HEAD_DIM) for hh in range(HEAD_PAIR)]
    no_choice = jnp.zeros((SEL_COLS, tq), BF16)

    def weighted_values(n, hh, p):
        return jnp.dot(vT_ref[0, 0, n, hh * V_ROWS:(hh + 1) * V_ROWS, :], p.astype(BF16),
                       preferred_element_type=F32)

    def set_up(t):
        j = pl.program_id(2) * Q_BLOCKS_PER_STEP + t
        qT = qT_ref[0, 0, t]
        q_heads = [jnp.where(head_rows[hh], qT, jnp.zeros_like(qT)) for hh in range(HEAD_PAIR)]
        gates = [jnp.dot(jnp.where(head_lanes[hh], kmean, 0.0).astype(BF16), qT,
                         preferred_element_type=F32) for hh in range(HEAD_PAIR)]
        own_sT = [jnp.dot(kaug_ref[0, 0, j],
                          jnp.concatenate([q_heads[hh], no_choice, srows_ref[0, hh]], axis=0),
                          preferred_element_type=F32) for hh in range(HEAD_PAIR)]
        q_abs = jnp.abs(qT)
        qk_bound = [jnp.dot(jnp.where(head_lanes[hh], kabs_ref[0] * BF16_ROUND_UP, 0.0).astype(BF16),
                            q_abs, preferred_element_type=F32) for hh in range(HEAD_PAIR)]

        own_max, own_acc = [], []
        for hh in range(HEAD_PAIR):
            sT = jnp.where(causal, own_sT[hh], MASKED)
            m = jnp.max(sT, axis=0, keepdims=True)
            own_max.append(m)
            own_acc.append(weighted_values(j, hh, jnp.exp2(sT - m)))

        needed = blk_col < 0
        for hh in range(HEAD_PAIR):
            qk_over_shift = jnp.max(qk_bound[hh] - own_max[hh], axis=1, keepdims=True)
            alibi = slope_ref[0, hh][:, :1] * ((MOBA_BLOCK - 1) - MOBA_BLOCK * (j - blk_col)).astype(F32)
            needed = needed | (qk_over_shift + alibi + BOUND_SLACK >= EXP2_IS_ZERO_BELOW)
        first_needed = jnp.min(
            jnp.where(needed & (blk_col < j), blk_col, j).astype(F32)).astype(jnp.int32)

        for hh in range(HEAD_PAIR):
            gate = jnp.where(blk < j, gates[hh], -jnp.inf)
            bias = jnp.full((nblk, tq), MASKED, F32)
            for _ in range(MOBA_TOPK):
                best = jnp.max(gate, axis=0, keepdims=True)
                first = jnp.min(jnp.where(gate == best, blk_f32, float(nblk)), axis=0, keepdims=True)
                pick = blk_f32 == first
                chosen = jnp.where(best > -jnp.inf, 0.0, MASKED)
                bias = jnp.where(pick, jnp.maximum(bias, chosen), bias)
                gate = jnp.where(pick, -jnp.inf, gate)
            bias = bias.astype(BF16)
            if nblk < SEL_COLS:
                bias = jnp.concatenate([bias, jnp.zeros((SEL_COLS - nblk, tq), BF16)], axis=0)
            qaug_ref[t, hh] = jnp.concatenate([q_heads[hh], bias, srows_ref[0, hh]], axis=0)
        return j, own_max, own_acc, first_needed

    prepared = [set_up(t) for t in range(Q_BLOCKS_PER_STEP)]
    chains = [(t, hh) for t in range(Q_BLOCKS_PER_STEP) for hh in range(HEAD_PAIR)]
    block_j = [p[0] for p in prepared]
    own_max = {(t, hh): prepared[t][1][hh] for t, hh in chains}
    own_acc = {(t, hh): prepared[t][2][hh] for t, hh in chains}
    last_j = block_j[-1]

    def scores(t, n, hh):
        return jnp.dot(kaug_ref[0, 0, n], qaug_ref[t, hh], preferred_element_type=F32)

    def block_offset(t, n, hh):
        return slope_ref[0, hh] * ((n - block_j[t]) * MOBA_BLOCK).astype(F32)

    def finish(t, accs):
        oT = jnp.concatenate(
            [acc[:HEAD_DIM] * (1.0 / acc[HEAD_DIM:HEAD_DIM + 1]) for acc in accs], axis=0)
        o_ref[0, t * tq:(t + 1) * tq, :] = oT.T.astype(o_ref.dtype)

    def fixed_shift_blocks(first, count, accs):
        slots = [first + u for u in range(count)]
        blocks = [jnp.minimum(n, last_j) for n in slots]
        sT = [{c: scores(c[0], n, c[1]) for c in chains} for n in blocks]
        accs = list(accs)
        for u, n in enumerate(blocks):
            for i, (t, hh) in enumerate(chains):
                shift = jnp.where(slots[u] < block_j[t],
                                  own_max[t, hh] - block_offset(t, n, hh), jnp.inf)
                accs[i] += weighted_values(n, hh, jnp.exp2(sT[u][t, hh] - shift))
        return accs

    fast = [own_acc[c] for c in chains]
    start = functools.reduce(jnp.minimum, [p[3] for p in prepared])
    for unroll in TRIP_SIZES:
        left = last_j - start
        trips = (left + 1) // unroll if unroll == MASKED_TRIP else left // unroll
        fast = lax.fori_loop(
            0, trips,
            lambda g, a, start=start, unroll=unroll: fixed_shift_blocks(start + g * unroll, unroll, a),
            fast)
        start = jnp.minimum(start + trips * unroll, last_j)

    for t in range(Q_BLOCKS_PER_STEP):
        mine = fast[t * HEAD_PAIR:(t + 1) * HEAD_PAIR]
        finish(t, mine)
        denominators = functools.reduce(jnp.maximum, [a[HEAD_DIM:HEAD_DIM + 1] for a in mine])
        in_range = jnp.max(denominators) < OVERFLOW_GUARD

        @pl.when(jnp.logical_not(in_range))
        def _(t=t):
            def running_max_block(n, carry):
                out = []
                for hh in range(HEAD_PAIR):
                    m, acc = carry[2 * hh:2 * hh + 2]
                    sT = scores(t, n, hh)
                    c = block_offset(t, n, hh)
                    m_new = jnp.maximum(m, jnp.max(sT, axis=0, keepdims=True) + c)
                    acc = (jnp.exp2(m - m_new) * acc
                           + weighted_values(n, hh, jnp.exp2(sT - (m_new - c))))
                    out += [m_new, acc]
                return out

            slow = lax.fori_loop(
                0, block_j[t], running_max_block,
                [x for hh in range(HEAD_PAIR) for x in (own_max[t, hh], own_acc[t, hh])])
            finish(t, slow[1::2])


def _moba_tables(tq):
    slopes = _alibi_slopes(N_HEADS).astype(np.float64) * LOG2E
    pieces = _bf16_pieces(slopes)
    rows = np.zeros((N_HEADS, AUG - LANES - SEL_COLS, tq), np.float32)
    for c, piece in enumerate(pieces):
        rows[:, c, :] = piece[:, None]
    srows = jnp.asarray(rows.reshape(N_PAIRS, HEAD_PAIR, AUG - LANES - SEL_COLS, tq), BF16)
    svec = jnp.asarray(np.broadcast_to(slopes[:, None, None], (N_HEADS, 1, tq))
                       .reshape(N_PAIRS, HEAD_PAIR, 1, tq), F32)
    return srows, svec


def _moba_attention(qT, kaug, vT, kstat):
    b, _, nblk, _, tq = qT.shape
    assert nblk <= SEL_COLS and tq == MOBA_BLOCK and nblk % Q_BLOCKS_PER_STEP == 0
    s = nblk * MOBA_BLOCK
    kaug = kaug.reshape(b, N_PAIRS, nblk, MOBA_BLOCK, AUG)
    kmean = kstat[:, :, 0].reshape(b, nblk, ATTN_WIDTH)
    kabs = kstat[:, :, 1].reshape(b, nblk, ATTN_WIDTH)
    srows, svec = _moba_tables(tq)
    return pl.pallas_call(
        _moba_kernel,
        grid=(b, N_PAIRS, nblk // Q_BLOCKS_PER_STEP),
        in_specs=[
            pl.BlockSpec((1, 1, Q_BLOCKS_PER_STEP, LANES, tq), lambda bb, p, j: (bb, p, j, 0, 0)),
            pl.BlockSpec((1, 1, nblk, MOBA_BLOCK, AUG), lambda bb, p, j: (bb, p, 0, 0, 0)),
            pl.BlockSpec((1, 1, nblk, HEAD_PAIR * V_ROWS, MOBA_BLOCK),
                         lambda bb, p, j: (bb, p, 0, 0, 0)),
            pl.BlockSpec((1, nblk, LANES), lambda bb, p, j: (bb, 0, p)),
            pl.BlockSpec((1, nblk, LANES), lambda bb, p, j: (bb, 0, p)),
            pl.BlockSpec((1, HEAD_PAIR, AUG - LANES - SEL_COLS, tq), lambda bb, p, j: (p, 0, 0, 0)),
            pl.BlockSpec((1, HEAD_PAIR, 1, tq), lambda bb, p, j: (p, 0, 0, 0)),
        ],
        out_specs=pl.BlockSpec((1, Q_BLOCKS_PER_STEP * tq, LANES), lambda bb, p, j: (bb, j, p)),
        out_shape=jax.ShapeDtypeStruct((b, s, ATTN_WIDTH), BF16),
        scratch_shapes=[pltpu.VMEM((Q_BLOCKS_PER_STEP, HEAD_PAIR, AUG, tq), BF16)],
        compiler_params=_params("parallel", "parallel", "arbitrary"),
        name="moba_attention",
    )(qT, kaug, vT, kmean, kabs, srows, svec)


def _swa_kernel(qT_ref, kp_ref, kc_ref, vTp_ref, vTc_ref, bias_ref, sink_ref, o_ref, yT_ref):
    tq = qT_ref.shape[-1]
    kwin = jnp.concatenate([kp_ref[0], kc_ref[0]], axis=0)
    vTwin = jnp.concatenate([vTp_ref[0], vTc_ref[0]], axis=1)
    zeros = jnp.zeros((HEAD_DIM, tq), BF16)
    group = N_HEADS // N_KV_HEADS_B

    def padded_q(h):
        q_h = qT_ref[0, h * HEAD_DIM:(h + 1) * HEAD_DIM, :]
        return jnp.concatenate([q_h, zeros] if h // group == 0 else [zeros, q_h], axis=0)

    worst = jnp.zeros((1, group * tq), F32)
    for g in range(N_KV_HEADS_B):
        heads = range(g * group, (g + 1) * group)
        sink = jnp.concatenate([sink_ref[h] * LOG2E for h in heads], axis=1)
        s_g = jnp.dot(kwin, jnp.concatenate([padded_q(h) for h in heads], axis=1),
                      preferred_element_type=F32)
        p = jnp.exp2(s_g + bias_ref[0, g] - sink).astype(BF16)
        acc = jnp.dot(vTwin[g * V_ROWS:(g + 1) * V_ROWS, :], p, preferred_element_type=F32)
        denom = acc[HEAD_DIM:HEAD_DIM + 1] + 1.0
        worst = jnp.maximum(worst, denom)
        out = acc[:HEAD_DIM] * (1.0 / denom)
        for i, h in enumerate(heads):
            yT_ref[h * HEAD_DIM:(h + 1) * HEAD_DIM, :] = out[:, i * tq:(i + 1) * tq]
    o_ref[0] = yT_ref[...].T.astype(o_ref.dtype)
    in_range = jnp.max(worst) < OVERFLOW_GUARD

    @pl.when(jnp.logical_not(in_range))
    def _():
        for h in range(N_HEADS):
            g, i = divmod(h, group)
            s_h = (jnp.dot(kwin, padded_q(h), preferred_element_type=F32)
                   + bias_ref[0, g, :, i * tq:(i + 1) * tq])
            sink = sink_ref[h] * LOG2E
            m = jnp.maximum(jnp.max(s_h, axis=0, keepdims=True), sink)
            acc = jnp.dot(vTwin[g * V_ROWS:(g + 1) * V_ROWS, :], jnp.exp2(s_h - m).astype(BF16),
                          preferred_element_type=F32)
            denom = acc[HEAD_DIM:HEAD_DIM + 1] + jnp.exp2(sink - m)
            yT_ref[h * HEAD_DIM:(h + 1) * HEAD_DIM, :] = acc[:HEAD_DIM] * (1.0 / denom)
        o_ref[0] = yT_ref[...].T.astype(o_ref.dtype)


def _swa_bias_table(tq):
    slopes = _alibi_slopes(N_HEADS).astype(np.float64) * LOG2E
    kw = np.arange(WINDOW + tq)[:, None]
    q = np.arange(tq)[None, :]
    dist = q + WINDOW - kw
    ok = (dist >= 0) & (dist < WINDOW)
    first_tile = ok & (kw >= WINDOW)
    table = np.stack([np.where(valid[None], -slopes[:, None, None] * dist[None], MASKED)
                      for valid in (first_tile, ok)])
    table = table.reshape(2, N_KV_HEADS_B, N_HEADS // N_KV_HEADS_B, WINDOW + tq, tq)
    table = np.moveaxis(table, 2, 3).reshape(2, N_KV_HEADS_B, WINDOW + tq, -1)
    return jnp.asarray(table, F32)


def _swa_attention(qT, k, vT, sinks):
    b, _, s = qT.shape
    tq = min(SWA_TILE, s)
    r = tq // WINDOW
    bias = _swa_bias_table(tq)
    sink_rows = jnp.broadcast_to(sinks.astype(F32)[:, None, None], (N_HEADS, 1, tq))
    prev = lambda j: jnp.maximum(j * r - 1, 0)
    return pl.pallas_call(
        _swa_kernel,
        grid=(b, s // tq),
        in_specs=[
            pl.BlockSpec((1, ATTN_WIDTH, tq), lambda bb, j: (bb, 0, j)),
            pl.BlockSpec((1, WINDOW, KV_WIDTH_B), lambda bb, j: (bb, prev(j), 0)),
            pl.BlockSpec((1, tq, KV_WIDTH_B), lambda bb, j: (bb, j, 0)),
            pl.BlockSpec((1, N_KV_HEADS_B * V_ROWS, WINDOW), lambda bb, j: (bb, 0, prev(j))),
            pl.BlockSpec((1, N_KV_HEADS_B * V_ROWS, tq), lambda bb, j: (bb, 0, j)),
            pl.BlockSpec((1,) + bias.shape[1:], lambda bb, j: (jnp.minimum(j, 1), 0, 0, 0)),
            pl.BlockSpec(sink_rows.shape, lambda bb, j: (0, 0, 0)),
        ],
        out_specs=pl.BlockSpec((1, tq, ATTN_WIDTH), lambda bb, j: (bb, j, 0)),
        out_shape=jax.ShapeDtypeStruct((b, s, ATTN_WIDTH), BF16),
        scratch_shapes=[pltpu.VMEM((ATTN_WIDTH, tq), F32)],
        compiler_params=_params("parallel", "arbitrary"),
        name="swa_attention",
    )(qT, k, k, vT, vT, bias, sink_rows)


def kernel(x, mem, norm_g, w_in_a, w_in_b, sinks_b, w_mem_kv, w_out, mem_norm_g, final_norm_g):
    depth = norm_g.shape[0]
    b, s, _ = x.shape
    assert s % ROW_TILE == 0 or s < ROW_TILE
    mem_k, mem_vT = _mem_kv(mem, mem_norm_g, w_mem_kv)
    mixers = ["moba" if i % 2 == 0 else "swa" for i in range(depth)]
    w_in = [w_in_a[i // 2] if i % 2 == 0 else w_in_b[i // 2] for i in range(depth)]
    proj = _first_projection(x, norm_g[0], w_in[0], mixers[0])
    for i in range(depth):
        if mixers[i] == "moba":
            qT, kaug, vT, kstat, qmT, z = proj
            y_self = _moba_attention(qT, kaug, vT, kstat)
        else:
            qT, k, vT, qmT, z = proj
            y_self = _swa_attention(qT, k, vT, sinks_b[i // 2])
        if i + 1 < depth:
            x, *proj = _epilogue(y_self, qmT, z, x, mem_k, mem_vT, i, w_out[i], norm_g[i + 1],
                                 w_in[i + 1], mixers[i + 1])
        else:
            (x,) = _epilogue(y_self, qmT, z, x, mem_k, mem_vT, i, w_out[i], final_norm_g)
    return x
```

```python
import functools
import math

import jax
import jax.numpy as jnp
import numpy as np
from jax import lax
from jax.experimental import pallas as pl
from jax.experimental.pallas import tpu as pltpu

HEAD_DIM = 64
N_HEADS = 12
N_KV_HEADS_B = 2
N_MEM_HEADS = 4
ATTN_WIDTH = N_HEADS * HEAD_DIM
MEM_WIDTH = N_MEM_HEADS * HEAD_DIM
BRANCH_WIDTH = ATTN_WIDTH + MEM_WIDTH
KV_WIDTH_B = N_KV_HEADS_B * HEAD_DIM
MOBA_BLOCK = 256
MOBA_TOPK = 3
WINDOW = 128
RMS_EPS = 1e-6

LANES = 128
HEAD_PAIR = LANES // HEAD_DIM
N_PAIRS = N_HEADS // HEAD_PAIR
SEL_COLS = 64
ALIBI_COLS = 4
ONES_ROWS = 16
V_ROWS = HEAD_DIM + ONES_ROWS
LOG2E = math.log2(math.e)
AUG = 2 * LANES
MASKED = -(2.0 ** 30)
OVERFLOW_GUARD = 2.0 ** 100
EXP2_IS_ZERO_BELOW = -150.0
BOUND_SLACK = 1.0
BF16_ROUND_UP = 1.0 + 2.0 ** -7
TRIP_SIZES = (16, 4, 2, 1)
MASKED_TRIP = 4
Q_BLOCKS_PER_STEP = 2
ROW_TILE = 512
SWA_TILE = 256
VMEM_LIMIT = 56 * 1024 * 1024

F32 = jnp.float32
BF16 = jnp.bfloat16
_NT = (((1,), (1,)), ((), ()))


def _alibi_slopes(n_heads):
    def pow2_slopes(n):
        start = 2.0 ** (-8.0 / n)
        return [start ** (i + 1) for i in range(n)]
    if math.log2(n_heads).is_integer():
        vals = pow2_slopes(n_heads)
    else:
        c = 2 ** math.floor(math.log2(n_heads))
        vals = pow2_slopes(c) + pow2_slopes(2 * c)[0::2][: n_heads - c]
    return np.array(vals, dtype=np.float32)


def _bf16_pieces(v):
    rest = np.asarray(v, np.float64)
    pieces = []
    for _ in range(ALIBI_COLS):
        p = rest.astype(BF16).astype(np.float64)
        pieces.append(p.astype(np.float32))
        rest = rest - p
    assert np.all(np.abs(rest) <= np.abs(v) * 2.0 ** -30)
    return pieces


def _rms_bf16(x, g):
    ms = jnp.mean(x * x, axis=-1, keepdims=True)
    return (x * lax.rsqrt(ms + RMS_EPS) * g).astype(BF16)


def _params(*sem):
    return pltpu.CompilerParams(dimension_semantics=sem, vmem_limit_bytes=VMEM_LIMIT)


def _mem_kv_kernel(mem_ref, g_ref, wk_ref, wvT_ref, k_ref, vT_ref):
    h = _rms_bf16(mem_ref[0], g_ref[...])
    k = jnp.dot(h, wk_ref[0], preferred_element_type=F32)
    vT = lax.dot_general(wvT_ref[0], h, _NT, preferred_element_type=F32)
    lane = lax.broadcasted_iota(jnp.int32, k.shape, 1)
    for hh in range(N_MEM_HEADS):
        mine = (lane >= hh * HEAD_DIM) & (lane < (hh + 1) * HEAD_DIM)
        k_ref[0, 0, hh] = jnp.where(mine, k, 0.0).astype(BF16)
        vT_ref[0, 0, hh * V_ROWS:hh * V_ROWS + HEAD_DIM] = (
            vT[hh * HEAD_DIM:(hh + 1) * HEAD_DIM].astype(BF16))
        vT_ref[0, 0, hh * V_ROWS + HEAD_DIM:(hh + 1) * V_ROWS] = jnp.ones(
            (ONES_ROWS, vT.shape[1]), BF16)


def _mem_kv(mem, mem_norm_g, w_mem_kv):
    b, m, d = mem.shape
    depth = w_mem_kv.shape[0]
    wk = w_mem_kv[:, :, :MEM_WIDTH].astype(BF16)
    wvT = jnp.swapaxes(w_mem_kv[:, :, MEM_WIDTH:], 1, 2).astype(BF16)
    return pl.pallas_call(
        _mem_kv_kernel,
        grid=(depth, b),
        in_specs=[
            pl.BlockSpec((1, m, d), lambda i, bb: (bb, 0, 0)),
            pl.BlockSpec((1, d), lambda i, bb: (0, 0)),
            pl.BlockSpec((1, d, MEM_WIDTH), lambda i, bb: (i, 0, 0)),
            pl.BlockSpec((1, MEM_WIDTH, d), lambda i, bb: (i, 0, 0)),
        ],
        out_specs=[
            pl.BlockSpec((1, 1, N_MEM_HEADS, m, MEM_WIDTH), lambda i, bb: (i, bb, 0, 0, 0)),
            pl.BlockSpec((1, 1, N_MEM_HEADS * V_ROWS, m), lambda i, bb: (i, bb, 0, 0)),
        ],
        out_shape=[
            jax.ShapeDtypeStruct((depth, b, N_MEM_HEADS, m, MEM_WIDTH), BF16),
            jax.ShapeDtypeStruct((depth, b, N_MEM_HEADS * V_ROWS, m), BF16),
        ],
        compiler_params=_params("arbitrary", "arbitrary"),
        name="mem_kv",
    )(mem, mem_norm_g.reshape(1, d), wk, wvT)


def _moba_projection(h, wT_ref, w_ref, qT_ref, k_ref, kext_ref, vT_ref, kstat_ref, qmT_ref, z_ref):
    tm = h.shape[0]
    nb = tm // MOBA_BLOCK
    outT = lax.dot_general(wT_ref[...], h, _NT, preferred_element_type=F32)
    out = jnp.dot(h, w_ref[...], preferred_element_type=F32)
    ones = jnp.ones((ONES_ROWS, MOBA_BLOCK), BF16)
    for p in range(N_PAIRS):
        for bb in range(nb):
            cols = slice(bb * MOBA_BLOCK, (bb + 1) * MOBA_BLOCK)
            qT_ref[0, p, bb] = outT[p * LANES:(p + 1) * LANES, cols].astype(BF16)
            for hh in range(HEAD_PAIR):
                v0 = ATTN_WIDTH + (p * HEAD_PAIR + hh) * HEAD_DIM
                vT_ref[0, p, bb, hh * V_ROWS:hh * V_ROWS + HEAD_DIM] = (
                    outT[v0:v0 + HEAD_DIM, cols].astype(BF16))
                vT_ref[0, p, bb, hh * V_ROWS + HEAD_DIM:(hh + 1) * V_ROWS] = ones
    qmT_ref[0] = outT[2 * ATTN_WIDTH:, :].astype(BF16)
    k = out[:, :ATTN_WIDTH]
    z_ref[0] = out[:, ATTN_WIDTH:].astype(BF16)
    row = lax.broadcasted_iota(jnp.int32, (tm, LANES), 0)
    lane = lax.broadcasted_iota(jnp.int32, (tm, LANES), 1)
    blk = pl.program_id(1) * nb + row // MOBA_BLOCK
    s_rel = (row % MOBA_BLOCK).astype(F32)
    ext = jnp.where(lane < SEL_COLS, (lane == blk).astype(F32),
                    jnp.where(lane < SEL_COLS + ALIBI_COLS, s_rel, 0.0)).astype(BF16)
    k_bf16 = k.astype(BF16)
    k_ref[0] = k_bf16
    kext_ref[0] = ext
    k_abs = jnp.abs(k_bf16.astype(F32))
    for bb in range(nb):
        rows = slice(bb * MOBA_BLOCK, (bb + 1) * MOBA_BLOCK)
        kstat_ref[0, 0, 0, bb:bb + 1, :] = jnp.mean(k[rows], axis=0, keepdims=True)
        kstat_ref[0, 0, 1, bb:bb + 1, :] = jnp.max(k_abs[rows], axis=0, keepdims=True)


def _moba_projection_specs(b, s, tm):
    nblk = s // MOBA_BLOCK
    nb = tm // MOBA_BLOCK
    specs = [
        pl.BlockSpec((1, N_PAIRS, nb, LANES, MOBA_BLOCK), lambda bb, i: (bb, 0, i, 0, 0)),
        pl.BlockSpec((1, tm, ATTN_WIDTH), lambda bb, i: (bb, i, 0)),
        pl.BlockSpec((1, tm, LANES), lambda bb, i: (bb, i, 0)),
        pl.BlockSpec((1, N_PAIRS, nb, HEAD_PAIR * V_ROWS, MOBA_BLOCK),
                     lambda bb, i: (bb, 0, i, 0, 0)),
        pl.BlockSpec((1, 1, 2, nb, ATTN_WIDTH), lambda bb, i: (bb, i, 0, 0, 0)),
        pl.BlockSpec((1, MEM_WIDTH, tm), lambda bb, i: (bb, 0, i)),
        pl.BlockSpec((1, tm, BRANCH_WIDTH), lambda bb, i: (bb, i, 0)),
    ]
    shapes = [
        jax.ShapeDtypeStruct((b, N_PAIRS, nblk, LANES, MOBA_BLOCK), BF16),
        jax.ShapeDtypeStruct((b, s, ATTN_WIDTH), BF16),
        jax.ShapeDtypeStruct((b, s, LANES), BF16),
        jax.ShapeDtypeStruct((b, N_PAIRS, nblk, HEAD_PAIR * V_ROWS, MOBA_BLOCK), BF16),
        jax.ShapeDtypeStruct((b, s // tm, 2, nb, ATTN_WIDTH), F32),
        jax.ShapeDtypeStruct((b, MEM_WIDTH, s), BF16),
        jax.ShapeDtypeStruct((b, s, BRANCH_WIDTH), BF16),
    ]
    return specs, shapes


def _swa_projection(h, wT_ref, w_ref, qT_ref, k_ref, vT_ref, qmT_ref, z_ref):
    outT = lax.dot_general(wT_ref[...], h, _NT, preferred_element_type=F32)
    out = jnp.dot(h, w_ref[...], preferred_element_type=F32)
    tm = h.shape[0]
    qT_ref[0] = outT[:ATTN_WIDTH].astype(BF16)
    for g in range(N_KV_HEADS_B):
        v0 = ATTN_WIDTH + g * HEAD_DIM
        vT_ref[0, g * V_ROWS:g * V_ROWS + HEAD_DIM] = outT[v0:v0 + HEAD_DIM].astype(BF16)
        vT_ref[0, g * V_ROWS + HEAD_DIM:(g + 1) * V_ROWS] = jnp.ones((ONES_ROWS, tm), BF16)
    qmT_ref[0] = outT[ATTN_WIDTH + KV_WIDTH_B:].astype(BF16)
    k_ref[0] = out[:, :KV_WIDTH_B].astype(BF16)
    z_ref[0] = out[:, KV_WIDTH_B:].astype(BF16)


def _swa_projection_specs(b, s, tm):
    specs = [
        pl.BlockSpec((1, ATTN_WIDTH, tm), lambda bb, i: (bb, 0, i)),
        pl.BlockSpec((1, tm, KV_WIDTH_B), lambda bb, i: (bb, i, 0)),
        pl.BlockSpec((1, N_KV_HEADS_B * V_ROWS, tm), lambda bb, i: (bb, 0, i)),
        pl.BlockSpec((1, MEM_WIDTH, tm), lambda bb, i: (bb, 0, i)),
        pl.BlockSpec((1, tm, BRANCH_WIDTH), lambda bb, i: (bb, i, 0)),
    ]
    shapes = [
        jax.ShapeDtypeStruct((b, ATTN_WIDTH, s), BF16),
        jax.ShapeDtypeStruct((b, s, KV_WIDTH_B), BF16),
        jax.ShapeDtypeStruct((b, N_KV_HEADS_B * V_ROWS, s), BF16),
        jax.ShapeDtypeStruct((b, MEM_WIDTH, s), BF16),
        jax.ShapeDtypeStruct((b, s, BRANCH_WIDTH), BF16),
    ]
    return specs, shapes


_PROJECTIONS = {
    "moba": (_moba_projection, _moba_projection_specs),
    "swa": (_swa_projection, _swa_projection_specs),
}


def _split_in_weights(w_in, mixer):
    scale = HEAD_DIM ** -0.5 * LOG2E
    kvw = ATTN_WIDTH if mixer == "moba" else KV_WIDTH_B
    o = np.cumsum([0, ATTN_WIDTH, kvw, kvw, MEM_WIDTH, BRANCH_WIDTH])
    q, k, v, qm, z = (w_in[:, o[i]:o[i + 1]] for i in range(5))
    wT = jnp.concatenate([q * scale, v, qm * scale], axis=1).T.astype(BF16)
    w = jnp.concatenate([k, z], axis=1).astype(BF16)
    return wT, w


def _first_projection_kernel(x_ref, g_ref, wT_ref, w_ref, *out_refs, mixer):
    h = _rms_bf16(x_ref[0], g_ref[...])
    _PROJECTIONS[mixer][0](h, wT_ref, w_ref, *out_refs)


def _first_projection(x, g, w_in, mixer):
    b, s, d = x.shape
    tm = min(ROW_TILE, s)
    wT, w = _split_in_weights(w_in, mixer)
    specs, shapes = _PROJECTIONS[mixer][1](b, s, tm)
    return pl.pallas_call(
        functools.partial(_first_projection_kernel, mixer=mixer),
        grid=(b, s // tm),
        in_specs=[
            pl.BlockSpec((1, tm, d), lambda bb, i: (bb, i, 0)),
            pl.BlockSpec((1, d), lambda bb, i: (0, 0)),
            pl.BlockSpec(wT.shape, lambda bb, i: (0, 0)),
            pl.BlockSpec(w.shape, lambda bb, i: (0, 0)),
        ],
        out_specs=specs,
        out_shape=shapes,
        compiler_params=_params("parallel", "arbitrary"),
        name=f"in_proj_{mixer}",
    )(x, g.reshape(1, d), wT, w)


def _memory_attention_T(qmT, mk_ref, mvT_ref):
    sT = [jnp.dot(mk_ref[0, hh], qmT, preferred_element_type=F32)
          for hh in range(N_MEM_HEADS)]
    outs = []
    for hh in range(N_MEM_HEADS):
        p = jnp.exp2(sT[hh] - jnp.max(sT[hh], axis=0, keepdims=True))
        acc = jnp.dot(mvT_ref[0, hh * V_ROWS:(hh + 1) * V_ROWS, :], p.astype(BF16),
                      preferred_element_type=F32)
        outs.append(acc[:HEAD_DIM] * (1.0 / acc[HEAD_DIM:HEAD_DIM + 1]))
    return jnp.concatenate(outs, axis=0)


def _epilogue_kernel(y_ref, qmT_ref, z_ref, x_ref, mk_ref, mvT_ref, wout_ref, g_ref, *rest, mixer):
    ymem = _memory_attention_T(qmT_ref[0], mk_ref, mvT_ref).T
    y = jnp.concatenate([y_ref[0].astype(F32), ymem], axis=-1)
    z = z_ref[0].astype(F32)
    gated = (y * (z / (1.0 + jnp.exp(-z)))).astype(BF16)
    xn = x_ref[0] + jnp.dot(gated, wout_ref[...], preferred_element_type=F32)
    if mixer is None:
        (o_ref,) = rest
        ms = jnp.mean(xn * xn, axis=-1, keepdims=True)
        o_ref[0] = xn * lax.rsqrt(ms + RMS_EPS) * g_ref[...]
    else:
        wT_ref, w_ref, xo_ref = rest[:3]
        xo_ref[0] = xn
        _PROJECTIONS[mixer][0](_rms_bf16(xn, g_ref[...]), wT_ref, w_ref, *rest[3:])


def _epilogue(y_self, qmT, z, x, mem_k, mem_vT, layer, w_out, g, w_in_next=None, mixer=None):
    b, s, d = x.shape
    tm = min(ROW_TILE, s)
    m = mem_k.shape[3]
    in_specs = [
        pl.BlockSpec((1, tm, ATTN_WIDTH), lambda bb, i: (bb, i, 0)),
        pl.BlockSpec((1, MEM_WIDTH, tm), lambda bb, i: (bb, 0, i)),
        pl.BlockSpec((1, tm, BRANCH_WIDTH), lambda bb, i: (bb, i, 0)),
        pl.BlockSpec((1, tm, d), lambda bb, i: (bb, i, 0)),
        pl.BlockSpec((1, N_MEM_HEADS, m, MEM_WIDTH), lambda bb, i: (bb, 0, 0, 0)),
        pl.BlockSpec((1, N_MEM_HEADS * V_ROWS, m), lambda bb, i: (bb, 0, 0)),
        pl.BlockSpec((BRANCH_WIDTH, d), lambda bb, i: (0, 0)),
        pl.BlockSpec((1, d), lambda bb, i: (0, 0)),
    ]
    args = [y_self, qmT, z, x, mem_k[layer], mem_vT[layer], w_out.astype(BF16), g.reshape(1, d)]
    x_spec = pl.BlockSpec((1, tm, d), lambda bb, i: (bb, i, 0))
    x_shape = jax.ShapeDtypeStruct((b, s, d), F32)
    if mixer is None:
        out_specs, out_shape = [x_spec], [x_shape]
    else:
        wT, w = _split_in_weights(w_in_next, mixer)
        in_specs += [pl.BlockSpec(wT.shape, lambda bb, i: (0, 0)),
                     pl.BlockSpec(w.shape, lambda bb, i: (0, 0))]
        args += [wT, w]
        specs, shapes = _PROJECTIONS[mixer][1](b, s, tm)
        out_specs, out_shape = [x_spec] + specs, [x_shape] + shapes
    return pl.pallas_call(
        functools.partial(_epilogue_kernel, mixer=mixer),
        grid=(b, s // tm),
        in_specs=in_specs,
        out_specs=out_specs,
        out_shape=out_shape,
        compiler_params=_params("parallel", "arbitrary"),
        name=f"epilogue_{mixer}",
    )(*args)


def _moba_kernel(qT_ref, k_ref, kext_ref, vT_ref, kmean_ref, kabs_ref, srows_ref, slope_ref, o_ref,
                 qaug_ref):
    nblk = kmean_ref.shape[1]
    tq = qT_ref.shape[-1]
    kmean = kmean_ref[0]
    drow = lax.broadcasted_iota(jnp.int32, (LANES, tq), 0)
    klane = lax.broadcasted_iota(jnp.int32, kmean.shape, 1)
    blk = lax.broadcasted_iota(jnp.int32, (nblk, tq), 0)
    blk_f32 = blk.astype(F32)
    blk_col = lax.broadcasted_iota(jnp.int32, (nblk, 1), 0)
    krow = lax.broadcasted_iota(jnp.int32, (MOBA_BLOCK, tq), 0)
    qlane = lax.broadcasted_iota(jnp.int32, (MOBA_BLOCK, tq), 1)
    causal = krow <= qlane
    head_lanes = [(klane >= hh * HEAD_DIM) & (klane < (hh + 1) * HEAD_DIM) for hh in range(HEAD_PAIR)]
    head_rows = [(drow >= hh * HEAD_DIM) & (drow < (hh + 1) * HEAD_DIM) for hh in range(HEAD_PAIR)]
    no_choice = jnp.zeros((SEL_COLS, tq), BF16)

    def k_aug(n):
        return jnp.concatenate([k_ref[0, n], kext_ref[0, n]], axis=1)

    def weighted_values(n, hh, p):
        return jnp.dot(vT_ref[0, 0, n, hh * V_ROWS:(hh + 1) * V_ROWS, :], p.astype(BF16),
                       preferred_element_type=F32)

    def set_up(t):
        j = pl.program_id(2) * Q_BLOCKS_PER_STEP + t
        qT = qT_ref[0, 0, t]
        q_heads = [jnp.where(head_rows[hh], qT, jnp.zeros_like(qT)) for hh in range(HEAD_PAIR)]
        gates = [jnp.dot(jnp.where(head_lanes[hh], kmean, 0.0).astype(BF16), qT,
                         preferred_element_type=F32) for hh in range(HEAD_PAIR)]
        own_sT = [jnp.dot(k_aug(j),
                          jnp.concatenate([q_heads[hh], no_choice, srows_ref[0, hh]], axis=0),
                          preferred_element_type=F32) for hh in range(HEAD_PAIR)]
        q_abs = jnp.abs(qT)
        qk_bound = [jnp.dot(jnp.where(head_lanes[hh], kabs_ref[0] * BF16_ROUND_UP, 0.0).astype(BF16),
                            q_abs, preferred_element_type=F32) for hh in range(HEAD_PAIR)]

        own_max, own_acc = [], []
        for hh in range(HEAD_PAIR):
            sT = jnp.where(causal, own_sT[hh], MASKED)
            m = jnp.max(sT, axis=0, keepdims=True)
            own_max.append(m)
            own_acc.append(weighted_values(j, hh, jnp.exp2(sT - m)))

        needed = blk_col < 0
        for hh in range(HEAD_PAIR):
            qk_over_shift = jnp.max(qk_bound[hh] - own_max[hh], axis=1, keepdims=True)
            alibi = slope_ref[0, hh][:, :1] * ((MOBA_BLOCK - 1) - MOBA_BLOCK * (j - blk_col)).astype(F32)
            needed = needed | (qk_over_shift + alibi + BOUND_SLACK >= EXP2_IS_ZERO_BELOW)
        first_needed = jnp.min(
            jnp.where(needed & (blk_col < j), blk_col, j).astype(F32)).astype(jnp.int32)

        for hh in range(HEAD_PAIR):
            gate = jnp.where(blk < j, gates[hh], -jnp.inf)
            bias = jnp.full((nblk, tq), MASKED, F32)
            for _ in range(MOBA_TOPK):
                best = jnp.max(gate, axis=0, keepdims=True)
                first = jnp.min(jnp.where(gate == best, blk_f32, float(nblk)), axis=0, keepdims=True)
                pick = blk_f32 == first
                chosen = jnp.where(best > -jnp.inf, 0.0, MASKED)
                bias = jnp.where(pick, jnp.maximum(bias, chosen), bias)
                gate = jnp.where(pick, -jnp.inf, gate)
            bias = bias.astype(BF16)
            if nblk < SEL_COLS:
                bias = jnp.concatenate([bias, jnp.zeros((SEL_COLS - nblk, tq), BF16)], axis=0)
            qaug_ref[t, hh] = jnp.concatenate([q_heads[hh], bias, srows_ref[0, hh]], axis=0)
        return j, own_max, own_acc, first_needed

    prepared = [set_up(t) for t in range(Q_BLOCKS_PER_STEP)]
    chains = [(t, hh) for t in range(Q_BLOCKS_PER_STEP) for hh in range(HEAD_PAIR)]
    block_j = [p[0] for p in prepared]
    own_max = {(t, hh): prepared[t][1][hh] for t, hh in chains}
    own_acc = {(t, hh): prepared[t][2][hh] for t, hh in chains}
    last_j = block_j[-1]

    def scores(t, n, hh):
        return jnp.dot(k_aug(n), qaug_ref[t, hh], preferred_element_type=F32)

    def block_offset(t, n, hh):
        return slope_ref[0, hh] * ((n - block_j[t]) * MOBA_BLOCK).astype(F32)

    def finish(t, accs):
        oT = jnp.concatenate(
            [acc[:HEAD_DIM] * (1.0 / acc[HEAD_DIM:HEAD_DIM + 1]) for acc in accs], axis=0)
        o_ref[0, t * tq:(t + 1) * tq, :] = oT.T.astype(o_ref.dtype)

    def fixed_shift_blocks(first, count, accs):
        slots = [first + u for u in range(count)]
        blocks = [jnp.minimum(n, last_j) for n in slots]
        sT = [{c: scores(c[0], n, c[1]) for c in chains} for n in blocks]
        accs = list(accs)
        for u, n in enumerate(blocks):
            for i, (t, hh) in enumerate(chains):
                shift = jnp.where(slots[u] < block_j[t],
                                  own_max[t, hh] - block_offset(t, n, hh), jnp.inf)
                accs[i] += weighted_values(n, hh, jnp.exp2(sT[u][t, hh] - shift))
        return accs

    fast = [own_acc[c] for c in chains]
    start = functools.reduce(jnp.minimum, [p[3] for p in prepared])
    for unroll in TRIP_SIZES:
        left = last_j - start
        trips = (left + 1) // unroll if unroll == MASKED_TRIP else left // unroll
        fast = lax.fori_loop(
            0, trips,
            lambda g, a, start=start, unroll=unroll: fixed_shift_blocks(start + g * unroll, unroll, a),
            fast)
        start = jnp.minimum(start + trips * unroll, last_j)

    in_range = []
    for t in range(Q_BLOCKS_PER_STEP):
        mine = fast[t * HEAD_PAIR:(t + 1) * HEAD_PAIR]
        finish(t, mine)
        denominators = functools.reduce(jnp.maximum, [a[HEAD_DIM:HEAD_DIM + 1] for a in mine])
        in_range.append(jnp.max(denominators) < OVERFLOW_GUARD)

    for t in range(Q_BLOCKS_PER_STEP):
        @pl.when(jnp.logical_not(in_range[t]))
        def _(t=t):
            def running_max_block(n, carry):
                out = []
                for hh in range(HEAD_PAIR):
                    m, acc = carry[2 * hh:2 * hh + 2]
                    sT = scores(t, n, hh)
                    c = block_offset(t, n, hh)
                    m_new = jnp.maximum(m, jnp.max(sT, axis=0, keepdims=True) + c)
                    acc = (jnp.exp2(m - m_new) * acc
                           + weighted_values(n, hh, jnp.exp2(sT - (m_new - c))))
                    out += [m_new, acc]
                return out

            slow = lax.fori_loop(
                0, block_j[t], running_max_block,
                [x for hh in range(HEAD_PAIR) for x in (own_max[t, hh], own_acc[t, hh])])
            finish(t, slow[1::2])


def _moba_tables(tq):
    slopes = _alibi_slopes(N_HEADS).astype(np.float64) * LOG2E
    pieces = _bf16_pieces(slopes)
    rows = np.zeros((N_HEADS, AUG - LANES - SEL_COLS, tq), np.float32)
    for c, piece in enumerate(pieces):
        rows[:, c, :] = piece[:, None]
    srows = jnp.asarray(rows.reshape(N_PAIRS, HEAD_PAIR, AUG - LANES - SEL_COLS, tq), BF16)
    svec = jnp.asarray(np.broadcast_to(slopes[:, None, None], (N_HEADS, 1, tq))
                       .reshape(N_PAIRS, HEAD_PAIR, 1, tq), F32)
    return srows, svec


def _moba_attention(qT, k, kext, vT, kstat):
    b, _, nblk, _, tq = qT.shape
    assert nblk <= SEL_COLS and tq == MOBA_BLOCK and nblk % Q_BLOCKS_PER_STEP == 0
    s = nblk * MOBA_BLOCK
    k = k.reshape(b, nblk, MOBA_BLOCK, ATTN_WIDTH)
    kext = kext.reshape(b, nblk, MOBA_BLOCK, LANES)
    kmean = kstat[:, :, 0].reshape(b, nblk, ATTN_WIDTH)
    kabs = kstat[:, :, 1].reshape(b, nblk, ATTN_WIDTH)
    srows, svec = _moba_tables(tq)
    return pl.pallas_call(
        _moba_kernel,
        grid=(b, N_PAIRS, nblk // Q_BLOCKS_PER_STEP),
        in_specs=[
            pl.BlockSpec((1, 1, Q_BLOCKS_PER_STEP, LANES, tq), lambda bb, p, j: (bb, p, j, 0, 0)),
            pl.BlockSpec((1, nblk, MOBA_BLOCK, LANES), lambda bb, p, j: (bb, 0, 0, p)),
            pl.BlockSpec((1, nblk, MOBA_BLOCK, LANES), lambda bb, p, j: (bb, 0, 0, 0)),
            pl.BlockSpec((1, 1, nblk, HEAD_PAIR * V_ROWS, MOBA_BLOCK),
                         lambda bb, p, j: (bb, p, 0, 0, 0)),
            pl.BlockSpec((1, nblk, LANES), lambda bb, p, j: (bb, 0, p)),
            pl.BlockSpec((1, nblk, LANES), lambda bb, p, j: (bb, 0, p)),
            pl.BlockSpec((1, HEAD_PAIR, AUG - LANES - SEL_COLS, tq), lambda bb, p, j: (p, 0, 0, 0)),
            pl.BlockSpec((1, HEAD_PAIR, 1, tq), lambda bb, p, j: (p, 0, 0, 0)),
        ],
        out_specs=pl.BlockSpec((1, Q_BLOCKS_PER_STEP * tq, LANES), lambda bb, p, j: (bb, j, p)),
        out_shape=jax.ShapeDtypeStruct((b, s, ATTN_WIDTH), BF16),
        scratch_shapes=[pltpu.VMEM((Q_BLOCKS_PER_STEP, HEAD_PAIR, AUG, tq), BF16)],
        compiler_params=_params("parallel", "parallel", "arbitrary"),
        name="moba_attention",
    )(qT, k, kext, vT, kmean, kabs, srows, svec)


def _swa_kernel(qT_ref, kp_ref, kc_ref, vTp_ref, vTc_ref, bias_ref, sink_ref, o_ref, yT_ref):
    tq = qT_ref.shape[-1]
    kwin = jnp.concatenate([kp_ref[0], kc_ref[0]], axis=0)
    vTwin = jnp.concatenate([vTp_ref[0], vTc_ref[0]], axis=1)
    zeros = jnp.zeros((HEAD_DIM, tq), BF16)
    group = N_HEADS // N_KV_HEADS_B

    def padded_q(h):
        q_h = qT_ref[0, h * HEAD_DIM:(h + 1) * HEAD_DIM, :]
        return jnp.concatenate([q_h, zeros] if h // group == 0 else [zeros, q_h], axis=0)

    worst = jnp.zeros((1, group * tq), F32)
    for g in range(N_KV_HEADS_B):
        heads = range(g * group, (g + 1) * group)
        sink = jnp.concatenate([sink_ref[h] * LOG2E for h in heads], axis=1)
        s_g = jnp.dot(kwin, jnp.concatenate([padded_q(h) for h in heads], axis=1),
                      preferred_element_type=F32)
        p = jnp.exp2(s_g + bias_ref[0, g] - sink).astype(BF16)
        acc = jnp.dot(vTwin[g * V_ROWS:(g + 1) * V_ROWS, :], p, preferred_element_type=F32)
        denom = acc[HEAD_DIM:HEAD_DIM + 1] + 1.0
        worst = jnp.maximum(worst, denom)
        out = acc[:HEAD_DIM] * (1.0 / denom)
        for i, h in enumerate(heads):
            yT_ref[h * HEAD_DIM:(h + 1) * HEAD_DIM, :] = out[:, i * tq:(i + 1) * tq]
    o_ref[0] = yT_ref[...].T.astype(o_ref.dtype)
    in_range = jnp.max(worst) < OVERFLOW_GUARD

    @pl.when(jnp.logical_not(in_range))
    def _():
        for h in range(N_HEADS):
            g, i = divmod(h, group)
            s_h = (jnp.dot(kwin, padded_q(h), preferred_element_type=F32)
                   + bias_ref[0, g, :, i * tq:(i + 1) * tq])
            sink = sink_ref[h] * LOG2E
            m = jnp.maximum(jnp.max(s_h, axis=0, keepdims=True), sink)
            acc = jnp.dot(vTwin[g * V_ROWS:(g + 1) * V_ROWS, :], jnp.exp2(s_h - m).astype(BF16),
                          preferred_element_type=F32)
            denom = acc[HEAD_DIM:HEAD_DIM + 1] + jnp.exp2(sink - m)
            yT_ref[h * HEAD_DIM:(h + 1) * HEAD_DIM, :] = acc[:HEAD_DIM] * (1.0 / denom)
        o_ref[0] = yT_ref[...].T.astype(o_ref.dtype)


def _swa_bias_table(tq):
    slopes = _alibi_slopes(N_HEADS).astype(np.float64) * LOG2E
    kw = np.arange(WINDOW + tq)[:, None]
    q = np.arange(tq)[None, :]
    dist = q + WINDOW - kw
    ok = (dist >= 0) & (dist < WINDOW)
    first_tile = ok & (kw >= WINDOW)
    table = np.stack([np.where(valid[None], -slopes[:, None, None] * dist[None], MASKED)
                      for valid in (first_tile, ok)])
    table = table.reshape(2, N_KV_HEADS_B, N_HEADS // N_KV_HEADS_B, WINDOW + tq, tq)
    table = np.moveaxis(table, 2, 3).reshape(2, N_KV_HEADS_B, WINDOW + tq, -1)
    return jnp.asarray(table, F32)


def _swa_attention(qT, k, vT, sinks):
    b, _, s = qT.shape
    tq = min(SWA_TILE, s)
    r = tq // WINDOW
    bias = _swa_bias_table(tq)
    sink_rows = jnp.broadcast_to(sinks.astype(F32)[:, None, None], (N_HEADS, 1, tq))
    prev = lambda j: jnp.maximum(j * r - 1, 0)
    return pl.pallas_call(
        _swa_kernel,
        grid=(b, s // tq),
        in_specs=[
            pl.BlockSpec((1, ATTN_WIDTH, tq), lambda bb, j: (bb, 0, j)),
            pl.BlockSpec((1, WINDOW, KV_WIDTH_B), lambda bb, j: (bb, prev(j), 0)),
            pl.BlockSpec((1, tq, KV_WIDTH_B), lambda bb, j: (bb, j, 0)),
            pl.BlockSpec((1, N_KV_HEADS_B * V_ROWS, WINDOW), lambda bb, j: (bb, 0, prev(j))),
            pl.BlockSpec((1, N_KV_HEADS_B * V_ROWS, tq), lambda bb, j: (bb, 0, j)),
            pl.BlockSpec((1,) + bias.shape[1:], lambda bb, j: (jnp.minimum(j, 1), 0, 0, 0)),
            pl.BlockSpec(sink_rows.shape, lambda bb, j: (0, 0, 0)),
        ],
        out_specs=pl.BlockSpec((1, tq, ATTN_WIDTH), lambda bb, j: (bb, j, 0)),
        out_shape=jax.ShapeDtypeStruct((b, s, ATTN_WIDTH), BF16),
        scratch_shapes=[pltpu.VMEM((ATTN_WIDTH, tq), F32)],
        compiler_params=_params("parallel", "arbitrary"),
        name="swa_attention",
    )(qT, k, k, vT, vT, bias, sink_rows)


def kernel(x, mem, norm_g, w_in_a, w_in_b, sinks_b, w_mem_kv, w_out, mem_norm_g, final_norm_g):
    depth = norm_g.shape[0]
    b, s, _ = x.shape
    assert s % ROW_TILE == 0 or s < ROW_TILE
    mem_k, mem_vT = _mem_kv(mem, mem_norm_g, w_mem_kv)
    mixers = ["moba" if i % 2 == 0 else "swa" for i in range(depth)]
    w_in = [w_in_a[i // 2] if i % 2 == 0 else w_in_b[i // 2] for i in range(depth)]
    proj = _first_projection(x, norm_g[0], w_in[0], mixers[0])
    for i in range(depth):
        if mixers[i] == "moba":
            qT, k, kext, vT, kstat, qmT, z = proj
            y_self = _moba_attention(qT, k, kext, vT, kstat)
        else:
            qT, k, vT, qmT, z = proj
            y_self = _swa_attention(qT, k, vT, sinks_b[i // 2])
        if i + 1 < depth:
            x, *proj = _epilogue(y_self, qmT, z, x, mem_k, mem_vT, i, w_out[i], norm_g[i + 1],
                                 w_in[i + 1], mixers[i + 1])
        else:
            (x,) = _epilogue(y_self, qmT, z, x, mem_k, mem_vT, i, w_out[i], final_norm_g)
    return x
```

```python
import functools
import math

import jax
import jax.numpy as jnp
import numpy as np
from jax import lax
from jax.experimental import pallas as pl
from jax.experimental.pallas import tpu as pltpu

HEAD_DIM = 64
N_HEADS = 12
N_KV_HEADS_B = 2
N_MEM_HEADS = 4
ATTN_WIDTH = N_HEADS * HEAD_DIM
MEM_WIDTH = N_MEM_HEADS * HEAD_DIM
BRANCH_WIDTH = ATTN_WIDTH + MEM_WIDTH
KV_WIDTH_B = N_KV_HEADS_B * HEAD_DIM
MOBA_BLOCK = 256
MOBA_TOPK = 3
WINDOW = 128
RMS_EPS = 1e-6

LANES = 128
HEAD_PAIR = LANES // HEAD_DIM
N_PAIRS = N_HEADS // HEAD_PAIR
SEL_COLS = 64
ALIBI_COLS = 4
ONES_ROWS = 16
V_ROWS = HEAD_DIM + ONES_ROWS
LOG2E = math.log2(math.e)
AUG = 2 * LANES
MASKED = -(2.0 ** 30)
OVERFLOW_GUARD = 2.0 ** 100
EXP2_IS_ZERO_BELOW = -135.0
BOUND_SLACK = 1.0
BF16_ROUND_UP = 1.0 + 2.0 ** -7
TRIP_SIZES = (16, 4, 2, 1)
MASKED_TRIP = 4
Q_BLOCKS_PER_STEP = 2
ROW_TILE = 512
SWA_TILE = 256
VMEM_LIMIT = 56 * 1024 * 1024

F32 = jnp.float32
BF16 = jnp.bfloat16
_NT = (((1,), (1,)), ((), ()))


def _alibi_slopes(n_heads):
    def pow2_slopes(n):
        start = 2.0 ** (-8.0 / n)
        return [start ** (i + 1) for i in range(n)]
    if math.log2(n_heads).is_integer():
        vals = pow2_slopes(n_heads)
    else:
        c = 2 ** math.floor(math.log2(n_heads))
        vals = pow2_slopes(c) + pow2_slopes(2 * c)[0::2][: n_heads - c]
    return np.array(vals, dtype=np.float32)


def _bf16_pieces(v):
    rest = np.asarray(v, np.float64)
    pieces = []
    for _ in range(ALIBI_COLS):
        p = rest.astype(BF16).astype(np.float64)
        pieces.append(p.astype(np.float32))
        rest = rest - p
    assert np.all(np.abs(rest) <= np.abs(v) * 2.0 ** -30)
    return pieces


def _rms_bf16(x, g):
    ms = jnp.mean(x * x, axis=-1, keepdims=True)
    return (x * lax.rsqrt(ms + RMS_EPS) * g).astype(BF16)


def _params(*sem):
    return pltpu.CompilerParams(dimension_semantics=sem, vmem_limit_bytes=VMEM_LIMIT)


def _mem_kv_kernel(mem_ref, g_ref, wk_ref, wvT_ref, k_ref, vT_ref):
    h = _rms_bf16(mem_ref[0], g_ref[...])
    k = jnp.dot(h, wk_ref[0], preferred_element_type=F32)
    vT = lax.dot_general(wvT_ref[0], h, _NT, preferred_element_type=F32)
    lane = lax.broadcasted_iota(jnp.int32, k.shape, 1)
    for hh in range(N_MEM_HEADS):
        mine = (lane >= hh * HEAD_DIM) & (lane < (hh + 1) * HEAD_DIM)
        k_ref[0, 0, hh] = jnp.where(mine, k, 0.0).astype(BF16)
        vT_ref[0, 0, hh * V_ROWS:hh * V_ROWS + HEAD_DIM] = (
            vT[hh * HEAD_DIM:(hh + 1) * HEAD_DIM].astype(BF16))
        vT_ref[0, 0, hh * V_ROWS + HEAD_DIM:(hh + 1) * V_ROWS] = jnp.ones(
            (ONES_ROWS, vT.shape[1]), BF16)


def _mem_kv(mem, mem_norm_g, w_mem_kv):
    b, m, d = mem.shape
    depth = w_mem_kv.shape[0]
    wk = w_mem_kv[:, :, :MEM_WIDTH].astype(BF16)
    wvT = jnp.swapaxes(w_mem_kv[:, :, MEM_WIDTH:], 1, 2).astype(BF16)
    return pl.pallas_call(
        _mem_kv_kernel,
        grid=(depth, b),
        in_specs=[
            pl.BlockSpec((1, m, d), lambda i, bb: (bb, 0, 0)),
            pl.BlockSpec((1, d), lambda i, bb: (0, 0)),
            pl.BlockSpec((1, d, MEM_WIDTH), lambda i, bb: (i, 0, 0)),
            pl.BlockSpec((1, MEM_WIDTH, d), lambda i, bb: (i, 0, 0)),
        ],
        out_specs=[
            pl.BlockSpec((1, 1, N_MEM_HEADS, m, MEM_WIDTH), lambda i, bb: (i, bb, 0, 0, 0)),
            pl.BlockSpec((1, 1, N_MEM_HEADS * V_ROWS, m), lambda i, bb: (i, bb, 0, 0)),
        ],
        out_shape=[
            jax.ShapeDtypeStruct((depth, b, N_MEM_HEADS, m, MEM_WIDTH), BF16),
            jax.ShapeDtypeStruct((depth, b, N_MEM_HEADS * V_ROWS, m), BF16),
        ],
        compiler_params=_params("arbitrary", "arbitrary"),
        name="mem_kv",
    )(mem, mem_norm_g.reshape(1, d), wk, wvT)


def _moba_projection(h, wT_ref, w_ref, qT_ref, k_ref, kext_ref, vT_ref, kstat_ref, qmT_ref, z_ref):
    tm = h.shape[0]
    nb = tm // MOBA_BLOCK
    outT = lax.dot_general(wT_ref[...], h, _NT, preferred_element_type=F32)
    out = jnp.dot(h, w_ref[...], preferred_element_type=F32)
    ones = jnp.ones((ONES_ROWS, MOBA_BLOCK), BF16)
    for p in range(N_PAIRS):
        for bb in range(nb):
            cols = slice(bb * MOBA_BLOCK, (bb + 1) * MOBA_BLOCK)
            qT_ref[0, p, bb] = outT[p * LANES:(p + 1) * LANES, cols].astype(BF16)
            for hh in range(HEAD_PAIR):
                v0 = ATTN_WIDTH + (p * HEAD_PAIR + hh) * HEAD_DIM
                vT_ref[0, p, bb, hh * V_ROWS:hh * V_ROWS + HEAD_DIM] = (
                    outT[v0:v0 + HEAD_DIM, cols].astype(BF16))
                vT_ref[0, p, bb, hh * V_ROWS + HEAD_DIM:(hh + 1) * V_ROWS] = ones
    qmT_ref[0] = outT[2 * ATTN_WIDTH:, :].astype(BF16)
    k = out[:, :ATTN_WIDTH]
    z_ref[0] = out[:, ATTN_WIDTH:].astype(BF16)
    row = lax.broadcasted_iota(jnp.int32, (tm, LANES), 0)
    lane = lax.broadcasted_iota(jnp.int32, (tm, LANES), 1)
    blk = pl.program_id(1) * nb + row // MOBA_BLOCK
    s_rel = (row % MOBA_BLOCK).astype(F32)
    ext = jnp.where(lane < SEL_COLS, (lane == blk).astype(F32),
                    jnp.where(lane < SEL_COLS + ALIBI_COLS, s_rel, 0.0)).astype(BF16)
    k_bf16 = k.astype(BF16)
    k_ref[0] = k_bf16
    kext_ref[0] = ext
    k_abs = jnp.abs(k_bf16.astype(F32))
    for bb in range(nb):
        rows = slice(bb * MOBA_BLOCK, (bb + 1) * MOBA_BLOCK)
        kstat_ref[0, 0, 0, bb:bb + 1, :] = jnp.mean(k[rows], axis=0, keepdims=True)
        kstat_ref[0, 0, 1, bb:bb + 1, :] = jnp.max(k_abs[rows], axis=0, keepdims=True)


def _moba_projection_specs(b, s, tm):
    nblk = s // MOBA_BLOCK
    nb = tm // MOBA_BLOCK
    specs = [
        pl.BlockSpec((1, N_PAIRS, nb, LANES, MOBA_BLOCK), lambda bb, i: (bb, 0, i, 0, 0)),
        pl.BlockSpec((1, tm, ATTN_WIDTH), lambda bb, i: (bb, i, 0)),
        pl.BlockSpec((1, tm, LANES), lambda bb, i: (bb, i, 0)),
        pl.BlockSpec((1, N_PAIRS, nb, HEAD_PAIR * V_ROWS, MOBA_BLOCK),
                     lambda bb, i: (bb, 0, i, 0, 0)),
        pl.BlockSpec((1, 1, 2, nb, ATTN_WIDTH), lambda bb, i: (bb, i, 0, 0, 0)),
        pl.BlockSpec((1, MEM_WIDTH, tm), lambda bb, i: (bb, 0, i)),
        pl.BlockSpec((1, tm, BRANCH_WIDTH), lambda bb, i: (bb, i, 0)),
    ]
    shapes = [
        jax.ShapeDtypeStruct((b, N_PAIRS, nblk, LANES, MOBA_BLOCK), BF16),
        jax.ShapeDtypeStruct((b, s, ATTN_WIDTH), BF16),
        jax.ShapeDtypeStruct((b, s, LANES), BF16),
        jax.ShapeDtypeStruct((b, N_PAIRS, nblk, HEAD_PAIR * V_ROWS, MOBA_BLOCK), BF16),
        jax.ShapeDtypeStruct((b, s // tm, 2, nb, ATTN_WIDTH), F32),
        jax.ShapeDtypeStruct((b, MEM_WIDTH, s), BF16),
        jax.ShapeDtypeStruct((b, s, BRANCH_WIDTH), BF16),
    ]
    return specs, shapes


def _swa_projection(h, wT_ref, w_ref, qT_ref, k_ref, vT_ref, qmT_ref, z_ref):
    outT = lax.dot_general(wT_ref[...], h, _NT, preferred_element_type=F32)
    out = jnp.dot(h, w_ref[...], preferred_element_type=F32)
    tm = h.shape[0]
    qT_ref[0] = outT[:ATTN_WIDTH].astype(BF16)
    for g in range(N_KV_HEADS_B):
        v0 = ATTN_WIDTH + g * HEAD_DIM
        vT_ref[0, g * V_ROWS:g * V_ROWS + HEAD_DIM] = outT[v0:v0 + HEAD_DIM].astype(BF16)
        vT_ref[0, g * V_ROWS + HEAD_DIM:(g + 1) * V_ROWS] = jnp.ones((ONES_ROWS, tm), BF16)
    qmT_ref[0] = outT[ATTN_WIDTH + KV_WIDTH_B:].astype(BF16)
    k_ref[0] = out[:, :KV_WIDTH_B].astype(BF16)
    z_ref[0] = out[:, KV_WIDTH_B:].astype(BF16)


def _swa_projection_specs(b, s, tm):
    specs = [
        pl.BlockSpec((1, ATTN_WIDTH, tm), lambda bb, i: (bb, 0, i)),
        pl.BlockSpec((1, tm, KV_WIDTH_B), lambda bb, i: (bb, i, 0)),
        pl.BlockSpec((1, N_KV_HEADS_B * V_ROWS, tm), lambda bb, i: (bb, 0, i)),
        pl.BlockSpec((1, MEM_WIDTH, tm), lambda bb, i: (bb, 0, i)),
        pl.BlockSpec((1, tm, BRANCH_WIDTH), lambda bb, i: (bb, i, 0)),
    ]
    shapes = [
        jax.ShapeDtypeStruct((b, ATTN_WIDTH, s), BF16),
        jax.ShapeDtypeStruct((b, s, KV_WIDTH_B), BF16),
        jax.ShapeDtypeStruct((b, N_KV_HEADS_B * V_ROWS, s), BF16),
        jax.ShapeDtypeStruct((b, MEM_WIDTH, s), BF16),
        jax.ShapeDtypeStruct((b, s, BRANCH_WIDTH), BF16),
    ]
    return specs, shapes


_PROJECTIONS = {
    "moba": (_moba_projection, _moba_projection_specs),
    "swa": (_swa_projection, _swa_projection_specs),
}


def _split_in_weights(w_in, mixer):
    scale = HEAD_DIM ** -0.5 * LOG2E
    kvw = ATTN_WIDTH if mixer == "moba" else KV_WIDTH_B
    o = np.cumsum([0, ATTN_WIDTH, kvw, kvw, MEM_WIDTH, BRANCH_WIDTH])
    q, k, v, qm, z = (w_in[:, o[i]:o[i + 1]] for i in range(5))
    wT = jnp.concatenate([q * scale, v, qm * scale], axis=1).T.astype(BF16)
    w = jnp.concatenate([k, z], axis=1).astype(BF16)
    return wT, w


def _first_projection_kernel(x_ref, g_ref, wT_ref, w_ref, *out_refs, mixer):
    h = _rms_bf16(x_ref[0], g_ref[...])
    _PROJECTIONS[mixer][0](h, wT_ref, w_ref, *out_refs)


def _first_projection(x, g, w_in, mixer):
    b, s, d = x.shape
    tm = min(ROW_TILE, s)
    wT, w = _split_in_weights(w_in, mixer)
    specs, shapes = _PROJECTIONS[mixer][1](b, s, tm)
    return pl.pallas_call(
        functools.partial(_first_projection_kernel, mixer=mixer),
        grid=(b, s // tm),
        in_specs=[
            pl.BlockSpec((1, tm, d), lambda bb, i: (bb, i, 0)),
            pl.BlockSpec((1, d), lambda bb, i: (0, 0)),
            pl.BlockSpec(wT.shape, lambda bb, i: (0, 0)),
            pl.BlockSpec(w.shape, lambda bb, i: (0, 0)),
        ],
        out_specs=specs,
        out_shape=shapes,
        compiler_params=_params("parallel", "arbitrary"),
        name=f"in_proj_{mixer}",
    )(x, g.reshape(1, d), wT, w)


def _memory_attention_T(qmT, mk_ref, mvT_ref):
    sT = [jnp.dot(mk_ref[0, hh], qmT, preferred_element_type=F32)
          for hh in range(N_MEM_HEADS)]
    outs = []
    for hh in range(N_MEM_HEADS):
        p = jnp.exp2(sT[hh] - jnp.max(sT[hh], axis=0, keepdims=True))
        acc = jnp.dot(mvT_ref[0, hh * V_ROWS:(hh + 1) * V_ROWS, :], p.astype(BF16),
                      preferred_element_type=F32)
        outs.append(acc[:HEAD_DIM] * (1.0 / acc[HEAD_DIM:HEAD_DIM + 1]))
    return jnp.concatenate(outs, axis=0)


def _epilogue_kernel(y_ref, qmT_ref, z_ref, x_ref, mk_ref, mvT_ref, wout_ref, g_ref, *rest, mixer):
    ymem = _memory_attention_T(qmT_ref[0], mk_ref, mvT_ref).T
    y = jnp.concatenate([y_ref[0].astype(F32), ymem], axis=-1)
    z = z_ref[0].astype(F32)
    gated = (y * (z / (1.0 + jnp.exp(-z)))).astype(BF16)
    xn = x_ref[0] + jnp.dot(gated, wout_ref[...], preferred_element_type=F32)
    if mixer is None:
        (o_ref,) = rest
        ms = jnp.mean(xn * xn, axis=-1, keepdims=True)
        o_ref[0] = xn * lax.rsqrt(ms + RMS_EPS) * g_ref[...]
    else:
        wT_ref, w_ref, xo_ref = rest[:3]
        xo_ref[0] = xn
        _PROJECTIONS[mixer][0](_rms_bf16(xn, g_ref[...]), wT_ref, w_ref, *rest[3:])


def _epilogue(y_self, qmT, z, x, mem_k, mem_vT, layer, w_out, g, w_in_next=None, mixer=None):
    b, s, d = x.shape
    tm = min(ROW_TILE, s)
    m = mem_k.shape[3]
    in_specs = [
        pl.BlockSpec((1, tm, ATTN_WIDTH), lambda bb, i: (bb, i, 0)),
        pl.BlockSpec((1, MEM_WIDTH, tm), lambda bb, i: (bb, 0, i)),
        pl.BlockSpec((1, tm, BRANCH_WIDTH), lambda bb, i: (bb, i, 0)),
        pl.BlockSpec((1, tm, d), lambda bb, i: (bb, i, 0)),
        pl.BlockSpec((1, N_MEM_HEADS, m, MEM_WIDTH), lambda bb, i: (bb, 0, 0, 0)),
        pl.BlockSpec((1, N_MEM_HEADS * V_ROWS, m), lambda bb, i: (bb, 0, 0)),
        pl.BlockSpec((BRANCH_WIDTH, d), lambda bb, i: (0, 0)),
        pl.BlockSpec((1, d), lambda bb, i: (0, 0)),
    ]
    args = [y_self, qmT, z, x, mem_k[layer], mem_vT[layer], w_out.astype(BF16), g.reshape(1, d)]
    x_spec = pl.BlockSpec((1, tm, d), lambda bb, i: (bb, i, 0))
    x_shape = jax.ShapeDtypeStruct((b, s, d), F32)
    if mixer is None:
        out_specs, out_shape = [x_spec], [x_shape]
    else:
        wT, w = _split_in_weights(w_in_next, mixer)
        in_specs += [pl.BlockSpec(wT.shape, lambda bb, i: (0, 0)),
                     pl.BlockSpec(w.shape, lambda bb, i: (0, 0))]
        args += [wT, w]
        specs, shapes = _PROJECTIONS[mixer][1](b, s, tm)
        out_specs, out_shape = [x_spec] + specs, [x_shape] + shapes
    return pl.pallas_call(
        functools.partial(_epilogue_kernel, mixer=mixer),
        grid=(b, s // tm),
        in_specs=in_specs,
        out_specs=out_specs,
        out_shape=out_shape,
        compiler_params=_params("parallel", "arbitrary"),
        name=f"epilogue_{mixer}",
    )(*args)


def _moba_kernel(qT_ref, k_ref, kext_ref, vT_ref, kmean_ref, kabs_ref, srows_ref, slope_ref, o_ref,
                 qaug_ref):
    nblk = kmean_ref.shape[1]
    tq = qT_ref.shape[-1]
    kmean = kmean_ref[0]
    drow = lax.broadcasted_iota(jnp.int32, (LANES, tq), 0)
    klane = lax.broadcasted_iota(jnp.int32, kmean.shape, 1)
    blk = lax.broadcasted_iota(jnp.int32, (nblk, tq), 0)
    blk_f32 = blk.astype(F32)
    blk_col = lax.broadcasted_iota(jnp.int32, (nblk, 1), 0)
    krow = lax.broadcasted_iota(jnp.int32, (MOBA_BLOCK, tq), 0)
    qlane = lax.broadcasted_iota(jnp.int32, (MOBA_BLOCK, tq), 1)
    causal = krow <= qlane
    head_lanes = [(klane >= hh * HEAD_DIM) & (klane < (hh + 1) * HEAD_DIM) for hh in range(HEAD_PAIR)]
    head_rows = [(drow >= hh * HEAD_DIM) & (drow < (hh + 1) * HEAD_DIM) for hh in range(HEAD_PAIR)]
    no_choice = jnp.zeros((SEL_COLS, tq), BF16)

    def k_aug(n):
        return jnp.concatenate([k_ref[0, n], kext_ref[0, n]], axis=1)

    def weighted_values(n, hh, p):
        return jnp.dot(vT_ref[0, 0, n, hh * V_ROWS:(hh + 1) * V_ROWS, :], p.astype(BF16),
                       preferred_element_type=F32)

    chains = [(t, hh) for t in range(Q_BLOCKS_PER_STEP) for hh in range(HEAD_PAIR)]
    block_j = [pl.program_id(2) * Q_BLOCKS_PER_STEP + t for t in range(Q_BLOCKS_PER_STEP)]
    last_j = block_j[-1]
    qT = [qT_ref[0, 0, t] for t in range(Q_BLOCKS_PER_STEP)]
    q_abs = [jnp.abs(q) for q in qT]

    q_head = {(t, hh): jnp.where(head_rows[hh], qT[t], jnp.zeros_like(qT[t])) for t, hh in chains}
    gates = {(t, hh): jnp.dot(jnp.where(head_lanes[hh], kmean, 0.0).astype(BF16), qT[t],
                              preferred_element_type=F32) for t, hh in chains}
    own_sT = {(t, hh): jnp.dot(
        k_aug(block_j[t]), jnp.concatenate([q_head[t, hh], no_choice, srows_ref[0, hh]], axis=0),
        preferred_element_type=F32) for t, hh in chains}
    qk_bound = {(t, hh): jnp.dot(
        jnp.where(head_lanes[hh], kabs_ref[0] * BF16_ROUND_UP, 0.0).astype(BF16), q_abs[t],
        preferred_element_type=F32) for t, hh in chains}

    own_max, own_acc = {}, {}
    for t, hh in chains:
        sT = jnp.where(causal, own_sT[t, hh], MASKED)
        own_max[t, hh] = jnp.max(sT, axis=0, keepdims=True)
        own_acc[t, hh] = weighted_values(block_j[t], hh, jnp.exp2(sT - own_max[t, hh]))

    first_needed = last_j
    for t in range(Q_BLOCKS_PER_STEP):
        j = block_j[t]
        needed = blk_col < 0
        for hh in range(HEAD_PAIR):
            qk_over_shift = jnp.max(qk_bound[t, hh] - own_max[t, hh], axis=1, keepdims=True)
            alibi = slope_ref[0, hh][:, :1] * ((MOBA_BLOCK - 1) - MOBA_BLOCK * (j - blk_col)).astype(F32)
            needed = needed | (qk_over_shift + alibi + BOUND_SLACK >= EXP2_IS_ZERO_BELOW)
        first_needed = jnp.minimum(first_needed, jnp.min(
            jnp.where(needed & (blk_col < j), blk_col, j).astype(F32)).astype(jnp.int32))

    gate = {c: jnp.where(blk < block_j[c[0]], gates[c], -jnp.inf) for c in chains}
    bias = {c: jnp.full((nblk, tq), MASKED, F32) for c in chains}
    for _ in range(MOBA_TOPK):
        for c in chains:
            best = jnp.max(gate[c], axis=0, keepdims=True)
            first = jnp.min(jnp.where(gate[c] == best, blk_f32, float(nblk)), axis=0, keepdims=True)
            pick = blk_f32 == first
            chosen = jnp.where(best > -jnp.inf, 0.0, MASKED)
            bias[c] = jnp.where(pick, jnp.maximum(bias[c], chosen), bias[c])
            gate[c] = jnp.where(pick, -jnp.inf, gate[c])
    for t, hh in chains:
        rows = bias[t, hh].astype(BF16)
        if nblk < SEL_COLS:
            rows = jnp.concatenate([rows, jnp.zeros((SEL_COLS - nblk, tq), BF16)], axis=0)
        qaug_ref[t, hh] = jnp.concatenate([q_head[t, hh], rows, srows_ref[0, hh]], axis=0)

    def scores(t, n, hh):
        return jnp.dot(k_aug(n), qaug_ref[t, hh], preferred_element_type=F32)

    def block_offset(t, n, hh):
        return slope_ref[0, hh] * ((n - block_j[t]) * MOBA_BLOCK).astype(F32)

    def finish(t, accs):
        oT = jnp.concatenate(
            [acc[:HEAD_DIM] * (1.0 / acc[HEAD_DIM:HEAD_DIM + 1]) for acc in accs], axis=0)
        o_ref[0, t * tq:(t + 1) * tq, :] = oT.T.astype(o_ref.dtype)

    def fixed_shift_blocks(first, count, accs):
        slots = [first + u for u in range(count)]
        blocks = [jnp.minimum(n, last_j) for n in slots]
        sT = [{c: scores(c[0], n, c[1]) for c in chains} for n in blocks]
        accs = list(accs)
        for u, n in enumerate(blocks):
            for i, (t, hh) in enumerate(chains):
                shift = jnp.where(slots[u] < block_j[t],
                                  own_max[t, hh] - block_offset(t, n, hh), jnp.inf)
                accs[i] += weighted_values(n, hh, jnp.exp2(sT[u][t, hh] - shift))
        return accs

    fast = [own_acc[c] for c in chains]
    start = first_needed
    for unroll in TRIP_SIZES:
        left = last_j - start
        trips = (left + 1) // unroll if unroll == MASKED_TRIP else left // unroll
        fast = lax.fori_loop(
            0, trips,
            lambda g, a, start=start, unroll=unroll: fixed_shift_blocks(start + g * unroll, unroll, a),
            fast)
        start = jnp.minimum(start + trips * unroll, last_j)

    in_range = []
    for t in range(Q_BLOCKS_PER_STEP):
        mine = fast[t * HEAD_PAIR:(t + 1) * HEAD_PAIR]
        finish(t, mine)
        denominators = functools.reduce(jnp.maximum, [a[HEAD_DIM:HEAD_DIM + 1] for a in mine])
        in_range.append(jnp.max(denominators) < OVERFLOW_GUARD)

    for t in range(Q_BLOCKS_PER_STEP):
        @pl.when(jnp.logical_not(in_range[t]))
        def _(t=t):
            def running_max_block(n, carry):
                out = []
                for hh in range(HEAD_PAIR):
                    m, acc = carry[2 * hh:2 * hh + 2]
                    sT = scores(t, n, hh)
                    c = block_offset(t, n, hh)
                    m_new = jnp.maximum(m, jnp.max(sT, axis=0, keepdims=True) + c)
                    acc = (jnp.exp2(m - m_new) * acc
                           + weighted_values(n, hh, jnp.exp2(sT - (m_new - c))))
                    out += [m_new, acc]
                return out

            slow = lax.fori_loop(
                0, block_j[t], running_max_block,
                [x for hh in range(HEAD_PAIR) for x in (own_max[t, hh], own_acc[t, hh])])
            finish(t, slow[1::2])


def _moba_tables(tq):
    slopes = _alibi_slopes(N_HEADS).astype(np.float64) * LOG2E
    pieces = _bf16_pieces(slopes)
    rows = np.zeros((N_HEADS, AUG - LANES - SEL_COLS, tq), np.float32)
    for c, piece in enumerate(pieces):
        rows[:, c, :] = piece[:, None]
    srows = jnp.asarray(rows.reshape(N_PAIRS, HEAD_PAIR, AUG - LANES - SEL_COLS, tq), BF16)
    svec = jnp.asarray(np.broadcast_to(slopes[:, None, None], (N_HEADS, 1, tq))
                       .reshape(N_PAIRS, HEAD_PAIR, 1, tq), F32)
    return srows, svec


def _moba_attention(qT, k, kext, vT, kstat):
    b, _, nblk, _, tq = qT.shape
    assert nblk <= SEL_COLS and tq == MOBA_BLOCK and nblk % Q_BLOCKS_PER_STEP == 0
    s = nblk * MOBA_BLOCK
    k = k.reshape(b, nblk, MOBA_BLOCK, ATTN_WIDTH)
    kext = kext.reshape(b, nblk, MOBA_BLOCK, LANES)
    kmean = kstat[:, :, 0].reshape(b, nblk, ATTN_WIDTH)
    kabs = kstat[:, :, 1].reshape(b, nblk, ATTN_WIDTH)
    srows, svec = _moba_tables(tq)
    return pl.pallas_call(
        _moba_kernel,
        grid=(b, N_PAIRS, nblk // Q_BLOCKS_PER_STEP),
        in_specs=[
            pl.BlockSpec((1, 1, Q_BLOCKS_PER_STEP, LANES, tq), lambda bb, p, j: (bb, p, j, 0, 0)),
            pl.BlockSpec((1, nblk, MOBA_BLOCK, LANES), lambda bb, p, j: (bb, 0, 0, p)),
            pl.BlockSpec((1, nblk, MOBA_BLOCK, LANES), lambda bb, p, j: (bb, 0, 0, 0)),
            pl.BlockSpec((1, 1, nblk, HEAD_PAIR * V_ROWS, MOBA_BLOCK),
                         lambda bb, p, j: (bb, p, 0, 0, 0)),
            pl.BlockSpec((1, nblk, LANES), lambda bb, p, j: (bb, 0, p)),
            pl.BlockSpec((1, nblk, LANES), lambda bb, p, j: (bb, 0, p)),
            pl.BlockSpec((1, HEAD_PAIR, AUG - LANES - SEL_COLS, tq), lambda bb, p, j: (p, 0, 0, 0)),
            pl.BlockSpec((1, HEAD_PAIR, 1, tq), lambda bb, p, j: (p, 0, 0, 0)),
        ],
        out_specs=pl.BlockSpec((1, Q_BLOCKS_PER_STEP * tq, LANES), lambda bb, p, j: (bb, j, p)),
        out_shape=jax.ShapeDtypeStruct((b, s, ATTN_WIDTH), BF16),
        scratch_shapes=[pltpu.VMEM((Q_BLOCKS_PER_STEP, HEAD_PAIR, AUG, tq), BF16)],
        compiler_params=_params("parallel", "parallel", "arbitrary"),
        name="moba_attention",
    )(qT, k, kext, vT, kmean, kabs, srows, svec)


def _swa_kernel(qT_ref, kp_ref, kc_ref, vTp_ref, vTc_ref, bias_ref, sink_ref, o_ref, yT_ref):
    tq = qT_ref.shape[-1]
    kwin = jnp.concatenate([kp_ref[0], kc_ref[0]], axis=0)
    vTwin = jnp.concatenate([vTp_ref[0], vTc_ref[0]], axis=1)
    zeros = jnp.zeros((HEAD_DIM, tq), BF16)
    group = N_HEADS // N_KV_HEADS_B

    def padded_q(h):
        q_h = qT_ref[0, h * HEAD_DIM:(h + 1) * HEAD_DIM, :]
        return jnp.concatenate([q_h, zeros] if h // group == 0 else [zeros, q_h], axis=0)

    worst = jnp.zeros((1, group * tq), F32)
    for g in range(N_KV_HEADS_B):
        heads = range(g * group, (g + 1) * group)
        sink = jnp.concatenate([sink_ref[h] * LOG2E for h in heads], axis=1)
        s_g = jnp.dot(kwin, jnp.concatenate([padded_q(h) for h in heads], axis=1),
                      preferred_element_type=F32)
        p = jnp.exp2(s_g + bias_ref[0, g] - sink).astype(BF16)
        acc = jnp.dot(vTwin[g * V_ROWS:(g + 1) * V_ROWS, :], p, preferred_element_type=F32)
        denom = acc[HEAD_DIM:HEAD_DIM + 1] + 1.0
        worst = jnp.maximum(worst, denom)
        out = acc[:HEAD_DIM] * (1.0 / denom)
        for i, h in enumerate(heads):
            yT_ref[h * HEAD_DIM:(h + 1) * HEAD_DIM, :] = out[:, i * tq:(i + 1) * tq]
    o_ref[0] = yT_ref[...].T.astype(o_ref.dtype)
    in_range = jnp.max(worst) < OVERFLOW_GUARD

    @pl.when(jnp.logical_not(in_range))
    def _():
        for h in range(N_HEADS):
            g, i = divmod(h, group)
            s_h = (jnp.dot(kwin, padded_q(h), preferred_element_type=F32)
                   + bias_ref[0, g, :, i * tq:(i + 1) * tq])
            sink = sink_ref[h] * LOG2E
            m = jnp.maximum(jnp.max(s_h, axis=0, keepdims=True), sink)
            acc = jnp.dot(vTwin[g * V_ROWS:(g + 1) * V_ROWS, :], jnp.exp2(s_h - m).astype(BF16),
                          preferred_element_type=F32)
            denom = acc[HEAD_DIM:HEAD_DIM + 1] + jnp.exp2(sink - m)
            yT_ref[h * HEAD_DIM:(h + 1) * HEAD_DIM, :] = acc[:HEAD_DIM] * (1.0 / denom)
        o_ref[0] = yT_ref[...].T.astype(o_ref.dtype)


def _swa_bias_table(tq):
    slopes = _alibi_slopes(N_HEADS).astype(np.float64) * LOG2E
    kw = np.arange(WINDOW + tq)[:, None]
    q = np.arange(tq)[None, :]
    dist = q + WINDOW - kw
    ok = (dist >= 0) & (dist < WINDOW)
    first_tile = ok & (kw >= WINDOW)
    table = np.stack([np.where(valid[None], -slopes[:, None, None] * dist[None], MASKED)
                      for valid in (first_tile, ok)])
    table = table.reshape(2, N_KV_HEADS_B, N_HEADS // N_KV_HEADS_B, WINDOW + tq, tq)
    table = np.moveaxis(table, 2, 3).reshape(2, N_KV_HEADS_B, WINDOW + tq, -1)
    return jnp.asarray(table, F32)


def _swa_attention(qT, k, vT, sinks):
    b, _, s = qT.shape
    tq = min(SWA_TILE, s)
    r = tq // WINDOW
    bias = _swa_bias_table(tq)
    sink_rows = jnp.broadcast_to(sinks.astype(F32)[:, None, None], (N_HEADS, 1, tq))
    prev = lambda j: jnp.maximum(j * r - 1, 0)
    return pl.pallas_call(
        _swa_kernel,
        grid=(b, s // tq),
        in_specs=[
            pl.BlockSpec((1, ATTN_WIDTH, tq), lambda bb, j: (bb, 0, j)),
            pl.BlockSpec((1, WINDOW, KV_WIDTH_B), lambda bb, j: (bb, prev(j), 0)),
            pl.BlockSpec((1, tq, KV_WIDTH_B), lambda bb, j: (bb, j, 0)),
            pl.BlockSpec((1, N_KV_HEADS_B * V_ROWS, WINDOW), lambda bb, j: (bb, 0, prev(j))),
            pl.BlockSpec((1, N_KV_HEADS_B * V_ROWS, tq), lambda bb, j: (bb, 0, j)),
            pl.BlockSpec((1,) + bias.shape[1:], lambda bb, j: (jnp.minimum(j, 1), 0, 0, 0)),
            pl.BlockSpec(sink_rows.shape, lambda bb, j: (0, 0, 0)),
        ],
        out_specs=pl.BlockSpec((1, tq, ATTN_WIDTH), lambda bb, j: (bb, j, 0)),
        out_shape=jax.ShapeDtypeStruct((b, s, ATTN_WIDTH), BF16),
        scratch_shapes=[pltpu.VMEM((ATTN_WIDTH, tq), F32)],
        compiler_params=_params("parallel", "arbitrary"),
        name="swa_attention",
    )(qT, k, k, vT, vT, bias, sink_rows)


def kernel(x, mem, norm_g, w_in_a, w_in_b, sinks_b, w_mem_kv, w_out, mem_norm_g, final_norm_g):
    depth = norm_g.shape[0]
    b, s, _ = x.shape
    assert s % ROW_TILE == 0 or s < ROW_TILE
    mem_k, mem_vT = _mem_kv(mem, mem_norm_g, w_mem_kv)
    mixers = ["moba" if i % 2 == 0 else "swa" for i in range(depth)]
    w_in = [w_in_a[i // 2] if i % 2 == 0 else w_in_b[i // 2] for i in range(depth)]
    proj = _first_projection(x, norm_g[0], w_in[0], mixers[0])
    for i in range(depth):
        if mixers[i] == "moba":
            qT, k, kext, vT, kstat, qmT, z = proj
            y_self = _moba_attention(qT, k, kext, vT, kstat)
        else:
            qT, k, vT, qmT, z = proj
            y_self = _swa_attention(qT, k, vT, sinks_b[i // 2])
        if i + 1 < depth:
            x, *proj = _epilogue(y_self, qmT, z, x, mem_k, mem_vT, i, w_out[i], norm_g[i + 1],
                                 w_in[i + 1], mixers[i + 1])
        else:
            (x,) = _epilogue(y_self, qmT, z, x, mem_k, mem_vT, i, w_out[i], final_norm_g)
    return x
```

```python
import functools
import math

import jax
import jax.numpy as jnp
import numpy as np
from jax import lax
from jax.experimental import pallas as pl
from jax.experimental.pallas import tpu as pltpu

HEAD_DIM = 64
N_HEADS = 12
N_KV_HEADS_B = 2
N_MEM_HEADS = 4
ATTN_WIDTH = N_HEADS * HEAD_DIM
MEM_WIDTH = N_MEM_HEADS * HEAD_DIM
BRANCH_WIDTH = ATTN_WIDTH + MEM_WIDTH
KV_WIDTH_B = N_KV_HEADS_B * HEAD_DIM
MOBA_BLOCK = 256
MOBA_TOPK = 3
WINDOW = 128
RMS_EPS = 1e-6

LANES = 128
HEAD_PAIR = LANES // HEAD_DIM
N_PAIRS = N_HEADS // HEAD_PAIR
SEL_COLS = 64
ALIBI_COLS = 4
ONES_ROWS = 16
V_ROWS = HEAD_DIM + ONES_ROWS
LOG2E = math.log2(math.e)
AUG = 2 * LANES
MASKED = -(2.0 ** 30)
OVERFLOW_GUARD = 2.0 ** 100
EXP2_IS_ZERO_BELOW = -135.0
BOUND_SLACK = 1.0
BF16_ROUND_UP = 1.0 + 2.0 ** -7
TRIP_SIZES = (16, 4, 2, 1)
MASKED_TRIP = 4
Q_BLOCKS_PER_STEP = 2
ROW_TILE = 512
SWA_TILE = 256
VMEM_LIMIT = 56 * 1024 * 1024

F32 = jnp.float32
BF16 = jnp.bfloat16
_NT = (((1,), (1,)), ((), ()))


def _alibi_slopes(n_heads):
    def pow2_slopes(n):
        start = 2.0 ** (-8.0 / n)
        return [start ** (i + 1) for i in range(n)]
    if math.log2(n_heads).is_integer():
        vals = pow2_slopes(n_heads)
    else:
        c = 2 ** math.floor(math.log2(n_heads))
        vals = pow2_slopes(c) + pow2_slopes(2 * c)[0::2][: n_heads - c]
    return np.array(vals, dtype=np.float32)


def _bf16_pieces(v):
    rest = np.asarray(v, np.float64)
    pieces = []
    for _ in range(ALIBI_COLS):
        p = rest.astype(BF16).astype(np.float64)
        pieces.append(p.astype(np.float32))
        rest = rest - p
    assert np.all(np.abs(rest) <= np.abs(v) * 2.0 ** -30)
    return pieces


def _rms_bf16(x, g):
    ms = jnp.mean(x * x, axis=-1, keepdims=True)
    return (x * lax.rsqrt(ms + RMS_EPS) * g).astype(BF16)


def _params(*sem):
    return pltpu.CompilerParams(dimension_semantics=sem, vmem_limit_bytes=VMEM_LIMIT)


def _mem_kv_kernel(mem_ref, g_ref, wk_ref, wvT_ref, k_ref, vT_ref):
    h = _rms_bf16(mem_ref[0], g_ref[...])
    k = jnp.dot(h, wk_ref[0], preferred_element_type=F32)
    vT = lax.dot_general(wvT_ref[0], h, _NT, preferred_element_type=F32)
    lane = lax.broadcasted_iota(jnp.int32, k.shape, 1)
    for hh in range(N_MEM_HEADS):
        mine = (lane >= hh * HEAD_DIM) & (lane < (hh + 1) * HEAD_DIM)
        k_ref[0, 0, hh] = jnp.where(mine, k, 0.0).astype(BF16)
        vT_ref[0, 0, hh * V_ROWS:hh * V_ROWS + HEAD_DIM] = (
            vT[hh * HEAD_DIM:(hh + 1) * HEAD_DIM].astype(BF16))
        vT_ref[0, 0, hh * V_ROWS + HEAD_DIM:(hh + 1) * V_ROWS] = jnp.ones(
            (ONES_ROWS, vT.shape[1]), BF16)


def _mem_kv(mem, mem_norm_g, w_mem_kv):
    b, m, d = mem.shape
    depth = w_mem_kv.shape[0]
    wk = w_mem_kv[:, :, :MEM_WIDTH].astype(BF16)
    wvT = jnp.swapaxes(w_mem_kv[:, :, MEM_WIDTH:], 1, 2).astype(BF16)
    return pl.pallas_call(
        _mem_kv_kernel,
        grid=(depth, b),
        in_specs=[
            pl.BlockSpec((1, m, d), lambda i, bb: (bb, 0, 0)),
            pl.BlockSpec((1, d), lambda i, bb: (0, 0)),
            pl.BlockSpec((1, d, MEM_WIDTH), lambda i, bb: (i, 0, 0)),
            pl.BlockSpec((1, MEM_WIDTH, d), lambda i, bb: (i, 0, 0)),
        ],
        out_specs=[
            pl.BlockSpec((1, 1, N_MEM_HEADS, m, MEM_WIDTH), lambda i, bb: (i, bb, 0, 0, 0)),
            pl.BlockSpec((1, 1, N_MEM_HEADS * V_ROWS, m), lambda i, bb: (i, bb, 0, 0)),
        ],
        out_shape=[
            jax.ShapeDtypeStruct((depth, b, N_MEM_HEADS, m, MEM_WIDTH), BF16),
            jax.ShapeDtypeStruct((depth, b, N_MEM_HEADS * V_ROWS, m), BF16),
        ],
        compiler_params=_params("arbitrary", "arbitrary"),
        name="mem_kv",
    )(mem, mem_norm_g.reshape(1, d), wk, wvT)


def _moba_projection(h, wT_ref, w_ref, qT_ref, k_ref, kext_ref, vT_ref, kstat_ref, qmT_ref, z_ref):
    tm = h.shape[0]
    nb = tm // MOBA_BLOCK
    outT = lax.dot_general(wT_ref[...], h, _NT, preferred_element_type=F32)
    out = jnp.dot(h, w_ref[...], preferred_element_type=F32)
    ones = jnp.ones((ONES_ROWS, MOBA_BLOCK), BF16)
    for p in range(N_PAIRS):
        for bb in range(nb):
            cols = slice(bb * MOBA_BLOCK, (bb + 1) * MOBA_BLOCK)
            qT_ref[0, p, bb] = outT[p * LANES:(p + 1) * LANES, cols].astype(BF16)
            for hh in range(HEAD_PAIR):
                v0 = ATTN_WIDTH + (p * HEAD_PAIR + hh) * HEAD_DIM
                vT_ref[0, p, bb, hh * V_ROWS:hh * V_ROWS + HEAD_DIM] = (
                    outT[v0:v0 + HEAD_DIM, cols].astype(BF16))
                vT_ref[0, p, bb, hh * V_ROWS + HEAD_DIM:(hh + 1) * V_ROWS] = ones
    qmT_ref[0] = outT[2 * ATTN_WIDTH:, :].astype(BF16)
    k = out[:, :ATTN_WIDTH]
    z_ref[0] = out[:, ATTN_WIDTH:].astype(BF16)
    row = lax.broadcasted_iota(jnp.int32, (tm, LANES), 0)
    lane = lax.broadcasted_iota(jnp.int32, (tm, LANES), 1)
    blk = pl.program_id(1) * nb + row // MOBA_BLOCK
    s_rel = (row % MOBA_BLOCK).astype(F32)
    ext = jnp.where(lane < SEL_COLS, (lane == blk).astype(F32),
                    jnp.where(lane < SEL_COLS + ALIBI_COLS, s_rel, 0.0)).astype(BF16)
    k_bf16 = k.astype(BF16)
    k_ref[0] = k_bf16
    kext_ref[0] = ext
    k_abs = jnp.abs(k_bf16.astype(F32))
    for bb in range(nb):
        rows = slice(bb * MOBA_BLOCK, (bb + 1) * MOBA_BLOCK)
        kstat_ref[0, 0, 0, bb:bb + 1, :] = jnp.mean(k[rows], axis=0, keepdims=True)
        kstat_ref[0, 0, 1, bb:bb + 1, :] = jnp.max(k_abs[rows], axis=0, keepdims=True)


def _moba_projection_specs(b, s, tm):
    nblk = s // MOBA_BLOCK
    nb = tm // MOBA_BLOCK
    specs = [
        pl.BlockSpec((1, N_PAIRS, nb, LANES, MOBA_BLOCK), lambda bb, i: (bb, 0, i, 0, 0)),
        pl.BlockSpec((1, tm, ATTN_WIDTH), lambda bb, i: (bb, i, 0)),
        pl.BlockSpec((1, tm, LANES), lambda bb, i: (bb, i, 0)),
        pl.BlockSpec((1, N_PAIRS, nb, HEAD_PAIR * V_ROWS, MOBA_BLOCK),
                     lambda bb, i: (bb, 0, i, 0, 0)),
        pl.BlockSpec((1, 1, 2, nb, ATTN_WIDTH), lambda bb, i: (bb, i, 0, 0, 0)),
        pl.BlockSpec((1, MEM_WIDTH, tm), lambda bb, i: (bb, 0, i)),
        pl.BlockSpec((1, tm, BRANCH_WIDTH), lambda bb, i: (bb, i, 0)),
    ]
    shapes = [
        jax.ShapeDtypeStruct((b, N_PAIRS, nblk, LANES, MOBA_BLOCK), BF16),
        jax.ShapeDtypeStruct((b, s, ATTN_WIDTH), BF16),
        jax.ShapeDtypeStruct((b, s, LANES), BF16),
        jax.ShapeDtypeStruct((b, N_PAIRS, nblk, HEAD_PAIR * V_ROWS, MOBA_BLOCK), BF16),
        jax.ShapeDtypeStruct((b, s // tm, 2, nb, ATTN_WIDTH), F32),
        jax.ShapeDtypeStruct((b, MEM_WIDTH, s), BF16),
        jax.ShapeDtypeStruct((b, s, BRANCH_WIDTH), BF16),
    ]
    return specs, shapes


def _swa_projection(h, wT_ref, w_ref, qT_ref, k_ref, vT_ref, qmT_ref, z_ref):
    outT = lax.dot_general(wT_ref[...], h, _NT, preferred_element_type=F32)
    out = jnp.dot(h, w_ref[...], preferred_element_type=F32)
    tm = h.shape[0]
    qT_ref[0] = outT[:ATTN_WIDTH].astype(BF16)
    for g in range(N_KV_HEADS_B):
        v0 = ATTN_WIDTH + g * HEAD_DIM
        vT_ref[0, g * V_ROWS:g * V_ROWS + HEAD_DIM] = outT[v0:v0 + HEAD_DIM].astype(BF16)
        vT_ref[0, g * V_ROWS + HEAD_DIM:(g + 1) * V_ROWS] = jnp.ones((ONES_ROWS, tm), BF16)
    qmT_ref[0] = outT[ATTN_WIDTH + KV_WIDTH_B:].astype(BF16)
    k_ref[0] = out[:, :KV_WIDTH_B].astype(BF16)
    z_ref[0] = out[:, KV_WIDTH_B:].astype(BF16)


def _swa_projection_specs(b, s, tm):
    specs = [
        pl.BlockSpec((1, ATTN_WIDTH, tm), lambda bb, i: (bb, 0, i)),
        pl.BlockSpec((1, tm, KV_WIDTH_B), lambda bb, i: (bb, i, 0)),
        pl.BlockSpec((1, N_KV_HEADS_B * V_ROWS, tm), lambda bb, i: (bb, 0, i)),
        pl.BlockSpec((1, MEM_WIDTH, tm), lambda bb, i: (bb, 0, i)),
        pl.BlockSpec((1, tm, BRANCH_WIDTH), lambda bb, i: (bb, i, 0)),
    ]
    shapes = [
        jax.ShapeDtypeStruct((b, ATTN_WIDTH, s), BF16),
        jax.ShapeDtypeStruct((b, s, KV_WIDTH_B), BF16),
        jax.ShapeDtypeStruct((b, N_KV_HEADS_B * V_ROWS, s), BF16),
        jax.ShapeDtypeStruct((b, MEM_WIDTH, s), BF16),
        jax.ShapeDtypeStruct((b, s, BRANCH_WIDTH), BF16),
    ]
    return specs, shapes


_PROJECTIONS = {
    "moba": (_moba_projection, _moba_projection_specs),
    "swa": (_swa_projection, _swa_projection_specs),
}


def _split_in_weights(w_in, mixer):
    scale = HEAD_DIM ** -0.5 * LOG2E
    kvw = ATTN_WIDTH if mixer == "moba" else KV_WIDTH_B
    o = np.cumsum([0, ATTN_WIDTH, kvw, kvw, MEM_WIDTH, BRANCH_WIDTH])
    q, k, v, qm, z = (w_in[:, o[i]:o[i + 1]] for i in range(5))
    wT = jnp.concatenate([q * scale, v, qm * scale], axis=1).T.astype(BF16)
    w = jnp.concatenate([k, z], axis=1).astype(BF16)
    return wT, w


def _first_projection_kernel(x_ref, g_ref, wT_ref, w_ref, *out_refs, mixer):
    h = _rms_bf16(x_ref[0], g_ref[...])
    _PROJECTIONS[mixer][0](h, wT_ref, w_ref, *out_refs)


def _first_projection(x, g, w_in, mixer):
    b, s, d = x.shape
    tm = min(ROW_TILE, s)
    wT, w = _split_in_weights(w_in, mixer)
    specs, shapes = _PROJECTIONS[mixer][1](b, s, tm)
    return pl.pallas_call(
        functools.partial(_first_projection_kernel, mixer=mixer),
        grid=(b, s // tm),
        in_specs=[
            pl.BlockSpec((1, tm, d), lambda bb, i: (bb, i, 0)),
            pl.BlockSpec((1, d), lambda bb, i: (0, 0)),
            pl.BlockSpec(wT.shape, lambda bb, i: (0, 0)),
            pl.BlockSpec(w.shape, lambda bb, i: (0, 0)),
        ],
        out_specs=specs,
        out_shape=shapes,
        compiler_params=_params("parallel", "arbitrary"),
        name=f"in_proj_{mixer}",
    )(x, g.reshape(1, d), wT, w)


def _memory_attention_T(qmT, mk_ref, mvT_ref):
    sT = [jnp.dot(mk_ref[0, hh], qmT, preferred_element_type=F32)
          for hh in range(N_MEM_HEADS)]
    outs = []
    for hh in range(N_MEM_HEADS):
        p = jnp.exp2(sT[hh] - jnp.max(sT[hh], axis=0, keepdims=True))
        acc = jnp.dot(mvT_ref[0, hh * V_ROWS:(hh + 1) * V_ROWS, :], p.astype(BF16),
                      preferred_element_type=F32)
        outs.append(acc[:HEAD_DIM] * (1.0 / acc[HEAD_DIM:HEAD_DIM + 1]))
    return jnp.concatenate(outs, axis=0)


def _epilogue_kernel(y_ref, qmT_ref, z_ref, x_ref, mk_ref, mvT_ref, wout_ref, g_ref, *rest, mixer):
    ymem = _memory_attention_T(qmT_ref[0], mk_ref, mvT_ref).T
    y = jnp.concatenate([y_ref[0].astype(F32), ymem], axis=-1)
    z = z_ref[0].astype(F32)
    gated = (y * (z / (1.0 + jnp.exp(-z)))).astype(BF16)
    xn = x_ref[0] + jnp.dot(gated, wout_ref[...], preferred_element_type=F32)
    if mixer is None:
        (o_ref,) = rest
        ms = jnp.mean(xn * xn, axis=-1, keepdims=True)
        o_ref[0] = xn * lax.rsqrt(ms + RMS_EPS) * g_ref[...]
    else:
        wT_ref, w_ref, xo_ref = rest[:3]
        xo_ref[0] = xn
        _PROJECTIONS[mixer][0](_rms_bf16(xn, g_ref[...]), wT_ref, w_ref, *rest[3:])


def _epilogue(y_self, qmT, z, x, mem_k, mem_vT, layer, w_out, g, w_in_next=None, mixer=None):
    b, s, d = x.shape
    tm = min(ROW_TILE, s)
    m = mem_k.shape[3]
    in_specs = [
        pl.BlockSpec((1, tm, ATTN_WIDTH), lambda bb, i: (bb, i, 0)),
        pl.BlockSpec((1, MEM_WIDTH, tm), lambda bb, i: (bb, 0, i)),
        pl.BlockSpec((1, tm, BRANCH_WIDTH), lambda bb, i: (bb, i, 0)),
        pl.BlockSpec((1, tm, d), lambda bb, i: (bb, i, 0)),
        pl.BlockSpec((1, N_MEM_HEADS, m, MEM_WIDTH), lambda bb, i: (bb, 0, 0, 0)),
        pl.BlockSpec((1, N_MEM_HEADS * V_ROWS, m), lambda bb, i: (bb, 0, 0)),
        pl.BlockSpec((BRANCH_WIDTH, d), lambda bb, i: (0, 0)),
        pl.BlockSpec((1, d), lambda bb, i: (0, 0)),
    ]
    args = [y_self, qmT, z, x, mem_k[layer], mem_vT[layer], w_out.astype(BF16), g.reshape(1, d)]
    x_spec = pl.BlockSpec((1, tm, d), lambda bb, i: (bb, i, 0))
    x_shape = jax.ShapeDtypeStruct((b, s, d), F32)
    if mixer is None:
        out_specs, out_shape = [x_spec], [x_shape]
    else:
        wT, w = _split_in_weights(w_in_next, mixer)
        in_specs += [pl.BlockSpec(wT.shape, lambda bb, i: (0, 0)),
                     pl.BlockSpec(w.shape, lambda bb, i: (0, 0))]
        args += [wT, w]
        specs, shapes = _PROJECTIONS[mixer][1](b, s, tm)
        out_specs, out_shape = [x_spec] + specs, [x_shape] + shapes
    return pl.pallas_call(
        functools.partial(_epilogue_kernel, mixer=mixer),
        grid=(b, s // tm),
        in_specs=in_specs,
        out_specs=out_specs,
        out_shape=out_shape,
        compiler_params=_params("parallel", "arbitrary"),
        name=f"epilogue_{mixer}",
    )(*args)


def _moba_kernel(qT_ref, k_ref, kext_ref, vT_ref, kmean_ref, kabs_ref, srows_ref, slope_ref, o_ref,
                 qaug_ref):
    nblk = kmean_ref.shape[1]
    tq = qT_ref.shape[-1]
    kmean = kmean_ref[0]
    drow = lax.broadcasted_iota(jnp.int32, (LANES, tq), 0)
    klane = lax.broadcasted_iota(jnp.int32, kmean.shape, 1)
    blk = lax.broadcasted_iota(jnp.int32, (nblk, tq), 0)
    blk_f32 = blk.astype(F32)
    blk_col = lax.broadcasted_iota(jnp.int32, (nblk, 1), 0)
    krow = lax.broadcasted_iota(jnp.int32, (MOBA_BLOCK, tq), 0)
    qlane = lax.broadcasted_iota(jnp.int32, (MOBA_BLOCK, tq), 1)
    causal = krow <= qlane
    head_lanes = [(klane >= hh * HEAD_DIM) & (klane < (hh + 1) * HEAD_DIM) for hh in range(HEAD_PAIR)]
    head_rows = [(drow >= hh * HEAD_DIM) & (drow < (hh + 1) * HEAD_DIM) for hh in range(HEAD_PAIR)]
    no_choice = jnp.zeros((SEL_COLS, tq), BF16)

    def k_aug(n):
        return jnp.concatenate([k_ref[0, n], kext_ref[0, n]], axis=1)

    def weighted_values(n, hh, p):
        return jnp.dot(vT_ref[0, 0, n, hh * V_ROWS:(hh + 1) * V_ROWS, :], p.astype(BF16),
                       preferred_element_type=F32)

    chains = [(t, hh) for t in range(Q_BLOCKS_PER_STEP) for hh in range(HEAD_PAIR)]
    block_j = [pl.program_id(2) * Q_BLOCKS_PER_STEP + t for t in range(Q_BLOCKS_PER_STEP)]
    last_j = block_j[-1]
    qT = [qT_ref[0, 0, t] for t in range(Q_BLOCKS_PER_STEP)]
    q_abs = [jnp.abs(q) for q in qT]

    q_head = {(t, hh): jnp.where(head_rows[hh], qT[t], jnp.zeros_like(qT[t])) for t, hh in chains}
    kmean_h = [jnp.where(head_lanes[hh], kmean, 0.0).astype(BF16) for hh in range(HEAD_PAIR)]
    gates = {(t, hh): jnp.dot(kmean_h[hh], qT[t], preferred_element_type=F32)
             for t, hh in chains}
    own_sT = {(t, hh): jnp.dot(
        k_aug(block_j[t]), jnp.concatenate([q_head[t, hh], no_choice, srows_ref[0, hh]], axis=0),
        preferred_element_type=F32) for t, hh in chains}
    kabs_h = [jnp.where(head_lanes[hh], kabs_ref[0] * BF16_ROUND_UP, 0.0).astype(BF16)
              for hh in range(HEAD_PAIR)]
    qk_bound = {(t, hh): jnp.dot(kabs_h[hh], q_abs[t], preferred_element_type=F32)
                for t, hh in chains}

    own_max, own_acc = {}, {}
    for t, hh in chains:
        sT = jnp.where(causal, own_sT[t, hh], MASKED)
        own_max[t, hh] = jnp.max(sT, axis=0, keepdims=True)
        own_acc[t, hh] = weighted_values(block_j[t], hh, jnp.exp2(sT - own_max[t, hh]))

    first_needed = last_j
    for t in range(Q_BLOCKS_PER_STEP):
        j = block_j[t]
        needed = blk_col < 0
        for hh in range(HEAD_PAIR):
            qk_over_shift = jnp.max(qk_bound[t, hh] - own_max[t, hh], axis=1, keepdims=True)
            alibi = slope_ref[0, hh][:, :1] * ((MOBA_BLOCK - 1) - MOBA_BLOCK * (j - blk_col)).astype(F32)
            needed = needed | (qk_over_shift + alibi + BOUND_SLACK >= EXP2_IS_ZERO_BELOW)
        first_needed = jnp.minimum(first_needed, jnp.min(
            jnp.where(needed & (blk_col < j), blk_col, j).astype(F32)).astype(jnp.int32))

    gate = {c: jnp.where(blk < block_j[c[0]], gates[c], -jnp.inf) for c in chains}
    bias = {c: jnp.full((nblk, tq), MASKED, F32) for c in chains}
    for k in range(MOBA_TOPK):
        for c in chains:
            best = jnp.max(gate[c], axis=0, keepdims=True)
            first = jnp.min(jnp.where(gate[c] == best, blk_f32, float(nblk)), axis=0, keepdims=True)
            pick = blk_f32 == jnp.where(best > -jnp.inf, first, float(nblk))
            bias[c] = jnp.where(pick, 0.0, bias[c])
            if k + 1 < MOBA_TOPK:
                gate[c] = jnp.where(pick, -jnp.inf, gate[c])
    for t, hh in chains:
        rows = bias[t, hh].astype(BF16)
        if nblk < SEL_COLS:
            rows = jnp.concatenate([rows, jnp.zeros((SEL_COLS - nblk, tq), BF16)], axis=0)
        qaug_ref[t, hh] = jnp.concatenate([q_head[t, hh], rows, srows_ref[0, hh]], axis=0)

    def scores(t, n, hh):
        return jnp.dot(k_aug(n), qaug_ref[t, hh], preferred_element_type=F32)

    def block_offset(t, n, hh):
        return slope_ref[0, hh] * ((n - block_j[t]) * MOBA_BLOCK).astype(F32)

    def finish(t, accs):
        oT = jnp.concatenate(
            [acc[:HEAD_DIM] * (1.0 / acc[HEAD_DIM:HEAD_DIM + 1]) for acc in accs], axis=0)
        o_ref[0, t * tq:(t + 1) * tq, :] = oT.T.astype(o_ref.dtype)

    def fixed_shift_blocks(first, count, accs):
        slots = [first + u for u in range(count)]
        blocks = [jnp.minimum(n, last_j) for n in slots]
        sT = [{c: scores(c[0], n, c[1]) for c in chains} for n in blocks]
        accs = list(accs)
        for u, n in enumerate(blocks):
            for i, (t, hh) in enumerate(chains):
                shift = jnp.where(slots[u] < block_j[t],
                                  own_max[t, hh] - block_offset(t, n, hh), jnp.inf)
                accs[i] += weighted_values(n, hh, jnp.exp2(sT[u][t, hh] - shift))
        return accs

    fast = [own_acc[c] for c in chains]
    start = first_needed
    for unroll in TRIP_SIZES:
        left = last_j - start
        trips = (left + 1) // unroll if unroll == MASKED_TRIP else left // unroll
        fast = lax.fori_loop(
            0, trips,
            lambda g, a, start=start, unroll=unroll: fixed_shift_blocks(start + g * unroll, unroll, a),
            fast)
        start = jnp.minimum(start + trips * unroll, last_j)

    in_range = []
    for t in range(Q_BLOCKS_PER_STEP):
        mine = fast[t * HEAD_PAIR:(t + 1) * HEAD_PAIR]
        finish(t, mine)
        denominators = functools.reduce(jnp.maximum, [a[HEAD_DIM:HEAD_DIM + 1] for a in mine])
        in_range.append(jnp.max(denominators) < OVERFLOW_GUARD)

    for t in range(Q_BLOCKS_PER_STEP):
        @pl.when(jnp.logical_not(in_range[t]))
        def _(t=t):
            def running_max_block(n, carry):
                out = []
                for hh in range(HEAD_PAIR):
                    m, acc = carry[2 * hh:2 * hh + 2]
                    sT = scores(t, n, hh)
                    c = block_offset(t, n, hh)
                    m_new = jnp.maximum(m, jnp.max(sT, axis=0, keepdims=True) + c)
                    acc = (jnp.exp2(m - m_new) * acc
                           + weighted_values(n, hh, jnp.exp2(sT - (m_new - c))))
                    out += [m_new, acc]
                return out

            slow = lax.fori_loop(
                0, block_j[t], running_max_block,
                [x for hh in range(HEAD_PAIR) for x in (own_max[t, hh], own_acc[t, hh])])
            finish(t, slow[1::2])


def _moba_tables(tq):
    slopes = _alibi_slopes(N_HEADS).astype(np.float64) * LOG2E
    pieces = _bf16_pieces(slopes)
    rows = np.zeros((N_HEADS, AUG - LANES - SEL_COLS, tq), np.float32)
    for c, piece in enumerate(pieces):
        rows[:, c, :] = piece[:, None]
    srows = jnp.asarray(rows.reshape(N_PAIRS, HEAD_PAIR, AUG - LANES - SEL_COLS, tq), BF16)
    svec = jnp.asarray(np.broadcast_to(slopes[:, None, None], (N_HEADS, 1, tq))
                       .reshape(N_PAIRS, HEAD_PAIR, 1, tq), F32)
    return srows, svec


def _moba_attention(qT, k, kext, vT, kstat):
    b, _, nblk, _, tq = qT.shape
    assert nblk <= SEL_COLS and tq == MOBA_BLOCK and nblk % Q_BLOCKS_PER_STEP == 0
    s = nblk * MOBA_BLOCK
    k = k.reshape(b, nblk, MOBA_BLOCK, ATTN_WIDTH)
    kext = kext.reshape(b, nblk, MOBA_BLOCK, LANES)
    kmean = kstat[:, :, 0].reshape(b, nblk, ATTN_WIDTH)
    kabs = kstat[:, :, 1].reshape(b, nblk, ATTN_WIDTH)
    srows, svec = _moba_tables(tq)
    return pl.pallas_call(
        _moba_kernel,
        grid=(b, N_PAIRS, nblk // Q_BLOCKS_PER_STEP),
        in_specs=[
            pl.BlockSpec((1, 1, Q_BLOCKS_PER_STEP, LANES, tq), lambda bb, p, j: (bb, p, j, 0, 0)),
            pl.BlockSpec((1, nblk, MOBA_BLOCK, LANES), lambda bb, p, j: (bb, 0, 0, p)),
            pl.BlockSpec((1, nblk, MOBA_BLOCK, LANES), lambda bb, p, j: (bb, 0, 0, 0)),
            pl.BlockSpec((1, 1, nblk, HEAD_PAIR * V_ROWS, MOBA_BLOCK),
                         lambda bb, p, j: (bb, p, 0, 0, 0)),
            pl.BlockSpec((1, nblk, LANES), lambda bb, p, j: (bb, 0, p)),
            pl.BlockSpec((1, nblk, LANES), lambda bb, p, j: (bb, 0, p)),
            pl.BlockSpec((1, HEAD_PAIR, AUG - LANES - SEL_COLS, tq), lambda bb, p, j: (p, 0, 0, 0)),
            pl.BlockSpec((1, HEAD_PAIR, 1, tq), lambda bb, p, j: (p, 0, 0, 0)),
        ],
        out_specs=pl.BlockSpec((1, Q_BLOCKS_PER_STEP * tq, LANES), lambda bb, p, j: (bb, j, p)),
        out_shape=jax.ShapeDtypeStruct((b, s, ATTN_WIDTH), BF16),
        scratch_shapes=[pltpu.VMEM((Q_BLOCKS_PER_STEP, HEAD_PAIR, AUG, tq), BF16)],
        compiler_params=_params("parallel", "parallel", "arbitrary"),
        name="moba_attention",
    )(qT, k, kext, vT, kmean, kabs, srows, svec)


def _swa_kernel(qT_ref, kp_ref, kc_ref, vTp_ref, vTc_ref, bias_ref, sink_ref, o_ref, yT_ref):
    tq = qT_ref.shape[-1]
    kwin = jnp.concatenate([kp_ref[0], kc_ref[0]], axis=0)
    vTwin = jnp.concatenate([vTp_ref[0], vTc_ref[0]], axis=1)
    zeros = jnp.zeros((HEAD_DIM, tq), BF16)
    group = N_HEADS // N_KV_HEADS_B

    def padded_q(h):
        q_h = qT_ref[0, h * HEAD_DIM:(h + 1) * HEAD_DIM, :]
        return jnp.concatenate([q_h, zeros] if h // group == 0 else [zeros, q_h], axis=0)

    worst = jnp.zeros((1, group * tq), F32)
    for g in range(N_KV_HEADS_B):
        heads = range(g * group, (g + 1) * group)
        sink = jnp.concatenate([sink_ref[h] * LOG2E for h in heads], axis=1)
        s_g = jnp.dot(kwin, jnp.concatenate([padded_q(h) for h in heads], axis=1),
                      preferred_element_type=F32)
        p = jnp.exp2(s_g + bias_ref[0, g] - sink).astype(BF16)
        acc = jnp.dot(vTwin[g * V_ROWS:(g + 1) * V_ROWS, :], p, preferred_element_type=F32)
        denom = acc[HEAD_DIM:HEAD_DIM + 1] + 1.0
        worst = jnp.maximum(worst, denom)
        out = acc[:HEAD_DIM] * (1.0 / denom)
        for i, h in enumerate(heads):
            yT_ref[h * HEAD_DIM:(h + 1) * HEAD_DIM, :] = out[:, i * tq:(i + 1) * tq]
    o_ref[0] = yT_ref[...].T.astype(o_ref.dtype)
    in_range = jnp.max(worst) < OVERFLOW_GUARD

    @pl.when(jnp.logical_not(in_range))
    def _():
        for h in range(N_HEADS):
            g, i = divmod(h, group)
            s_h = (jnp.dot(kwin, padded_q(h), preferred_element_type=F32)
                   + bias_ref[0, g, :, i * tq:(i + 1) * tq])
            sink = sink_ref[h] * LOG2E
            m = jnp.maximum(jnp.max(s_h, axis=0, keepdims=True), sink)
            acc = jnp.dot(vTwin[g * V_ROWS:(g + 1) * V_ROWS, :], jnp.exp2(s_h - m).astype(BF16),
                          preferred_element_type=F32)
            denom = acc[HEAD_DIM:HEAD_DIM + 1] + jnp.exp2(sink - m)
            yT_ref[h * HEAD_DIM:(h + 1) * HEAD_DIM, :] = acc[:HEAD_DIM] * (1.0 / denom)
        o_ref[0] = yT_ref[...].T.astype(o_ref.dtype)


def _swa_bias_table(tq):
    slopes = _alibi_slopes(N_HEADS).astype(np.float64) * LOG2E
    kw = np.arange(WINDOW + tq)[:, None]
    q = np.arange(tq)[None, :]
    dist = q + WINDOW - kw
    ok = (dist >= 0) & (dist < WINDOW)
    first_tile = ok & (kw >= WINDOW)
    table = np.stack([np.where(valid[None], -slopes[:, None, None] * dist[None], MASKED)
                      for valid in (first_tile, ok)])
    table = table.reshape(2, N_KV_HEADS_B, N_HEADS // N_KV_HEADS_B, WINDOW + tq, tq)
    table = np.moveaxis(table, 2, 3).reshape(2, N_KV_HEADS_B, WINDOW + tq, -1)
    return jnp.asarray(table, F32)


def _swa_attention(qT, k, vT, sinks):
    b, _, s = qT.shape
    tq = min(SWA_TILE, s)
    r = tq // WINDOW
    bias = _swa_bias_table(tq)
    sink_rows = jnp.broadcast_to(sinks.astype(F32)[:, None, None], (N_HEADS, 1, tq))
    prev = lambda j: jnp.maximum(j * r - 1, 0)
    return pl.pallas_call(
        _swa_kernel,
        grid=(b, s // tq),
        in_specs=[
            pl.BlockSpec((1, ATTN_WIDTH, tq), lambda bb, j: (bb, 0, j)),
            pl.BlockSpec((1, WINDOW, KV_WIDTH_B), lambda bb, j: (bb, prev(j), 0)),
            pl.BlockSpec((1, tq, KV_WIDTH_B), lambda bb, j: (bb, j, 0)),
            pl.BlockSpec((1, N_KV_HEADS_B * V_ROWS, WINDOW), lambda bb, j: (bb, 0, prev(j))),
            pl.BlockSpec((1, N_KV_HEADS_B * V_ROWS, tq), lambda bb, j: (bb, 0, j)),
            pl.BlockSpec((1,) + bias.shape[1:], lambda bb, j: (jnp.minimum(j, 1), 0, 0, 0)),
            pl.BlockSpec(sink_rows.shape, lambda bb, j: (0, 0, 0)),
        ],
        out_specs=pl.BlockSpec((1, tq, ATTN_WIDTH), lambda bb, j: (bb, j, 0)),
        out_shape=jax.ShapeDtypeStruct((b, s, ATTN_WIDTH), BF16),
        scratch_shapes=[pltpu.VMEM((ATTN_WIDTH, tq), F32)],
        compiler_params=_params("parallel", "arbitrary"),
        name="swa_attention",
    )(qT, k, k, vT, vT, bias, sink_rows)


def kernel(x, mem, norm_g, w_in_a, w_in_b, sinks_b, w_mem_kv, w_out, mem_norm_g, final_norm_g):
    depth = norm_g.shape[0]
    b, s, _ = x.shape
    assert s % ROW_TILE == 0 or s < ROW_TILE
    mem_k, mem_vT = _mem_kv(mem, mem_norm_g, w_mem_kv)
    mixers = ["moba" if i % 2 == 0 else "swa" for i in range(depth)]
    w_in = [w_in_a[i // 2] if i % 2 == 0 else w_in_b[i // 2] for i in range(depth)]
    proj = _first_projection(x, norm_g[0], w_in[0], mixers[0])
    for i in range(depth):
        if mixers[i] == "moba":
            qT, k, kext, vT, kstat, qmT, z = proj
            y_self = _moba_attention(qT, k, kext, vT, kstat)
        else:
            qT, k, vT, qmT, z = proj
            y_self = _swa_attention(qT, k, vT, sinks_b[i // 2])
        if i + 1 < depth:
            x, *proj = _epilogue(y_self, qmT, z, x, mem_k, mem_vT, i, w_out[i], norm_g[i + 1],
                                 w_in[i + 1], mixers[i + 1])
        else:
            (x,) = _epilogue(y_self, qmT, z, x, mem_k, mem_vT, i, w_out[i], final_norm_g)
    return x
```

```python
import functools
import math

import jax
import jax.numpy as jnp
import numpy as np
from jax import lax
from jax.experimental import pallas as pl
from jax.experimental.pallas import tpu as pltpu

HEAD_DIM = 64
N_HEADS = 12
N_KV_HEADS_B = 2
N_MEM_HEADS = 4
ATTN_WIDTH = N_HEADS * HEAD_DIM
MEM_WIDTH = N_MEM_HEADS * HEAD_DIM
BRANCH_WIDTH = ATTN_WIDTH + MEM_WIDTH
KV_WIDTH_B = N_KV_HEADS_B * HEAD_DIM
MOBA_BLOCK = 256
MOBA_TOPK = 3
WINDOW = 128
RMS_EPS = 1e-6

LANES = 128
HEAD_PAIR = LANES // HEAD_DIM
N_PAIRS = N_HEADS // HEAD_PAIR
SEL_COLS = 64
ALIBI_COLS = 4
ONES_ROWS = 16
V_ROWS = HEAD_DIM + ONES_ROWS
LOG2E = math.log2(math.e)
AUG = 2 * LANES
MASKED = -(2.0 ** 30)
OVERFLOW_GUARD = 2.0 ** 100
EXP2_IS_ZERO_BELOW = -135.0
BOUND_SLACK = 1.0
BF16_ROUND_UP = 1.0 + 2.0 ** -7
TRIP_SIZES = (16, 4, 2, 1)
MASKED_TRIP = 4
Q_BLOCKS_PER_STEP = 2
NEAR_BLOCKS = 2
ROW_TILE = 512
SWA_TILE = 256
VMEM_LIMIT = 56 * 1024 * 1024

F32 = jnp.float32
BF16 = jnp.bfloat16
_NT = (((1,), (1,)), ((), ()))


def _alibi_slopes(n_heads):
    def pow2_slopes(n):
        start = 2.0 ** (-8.0 / n)
        return [start ** (i + 1) for i in range(n)]
    if math.log2(n_heads).is_integer():
        vals = pow2_slopes(n_heads)
    else:
        c = 2 ** math.floor(math.log2(n_heads))
        vals = pow2_slopes(c) + pow2_slopes(2 * c)[0::2][: n_heads - c]
    return np.array(vals, dtype=np.float32)


def _bf16_pieces(v):
    rest = np.asarray(v, np.float64)
    pieces = []
    for _ in range(ALIBI_COLS):
        p = rest.astype(BF16).astype(np.float64)
        pieces.append(p.astype(np.float32))
        rest = rest - p
    assert np.all(np.abs(rest) <= np.abs(v) * 2.0 ** -30)
    return pieces


def _rms_bf16(x, g):
    ms = jnp.mean(x * x, axis=-1, keepdims=True)
    return (x * lax.rsqrt(ms + RMS_EPS) * g).astype(BF16)


def _params(*sem):
    return pltpu.CompilerParams(dimension_semantics=sem, vmem_limit_bytes=VMEM_LIMIT)


def _mem_kv_kernel(mem_ref, g_ref, wk_ref, wvT_ref, k_ref, vT_ref):
    h = _rms_bf16(mem_ref[0], g_ref[...])
    k = jnp.dot(h, wk_ref[0], preferred_element_type=F32)
    vT = lax.dot_general(wvT_ref[0], h, _NT, preferred_element_type=F32)
    lane = lax.broadcasted_iota(jnp.int32, k.shape, 1)
    for hh in range(N_MEM_HEADS):
        mine = (lane >= hh * HEAD_DIM) & (lane < (hh + 1) * HEAD_DIM)
        k_ref[0, 0, hh] = jnp.where(mine, k, 0.0).astype(BF16)
        vT_ref[0, 0, hh * V_ROWS:hh * V_ROWS + HEAD_DIM] = (
            vT[hh * HEAD_DIM:(hh + 1) * HEAD_DIM].astype(BF16))
        vT_ref[0, 0, hh * V_ROWS + HEAD_DIM:(hh + 1) * V_ROWS] = jnp.ones(
            (ONES_ROWS, vT.shape[1]), BF16)


def _mem_kv(mem, mem_norm_g, w_mem_kv):
    b, m, d = mem.shape
    depth = w_mem_kv.shape[0]
    wk = w_mem_kv[:, :, :MEM_WIDTH].astype(BF16)
    wvT = jnp.swapaxes(w_mem_kv[:, :, MEM_WIDTH:], 1, 2).astype(BF16)
    return pl.pallas_call(
        _mem_kv_kernel,
        grid=(depth, b),
        in_specs=[
            pl.BlockSpec((1, m, d), lambda i, bb: (bb, 0, 0)),
            pl.BlockSpec((1, d), lambda i, bb: (0, 0)),
            pl.BlockSpec((1, d, MEM_WIDTH), lambda i, bb: (i, 0, 0)),
            pl.BlockSpec((1, MEM_WIDTH, d), lambda i, bb: (i, 0, 0)),
        ],
        out_specs=[
            pl.BlockSpec((1, 1, N_MEM_HEADS, m, MEM_WIDTH), lambda i, bb: (i, bb, 0, 0, 0)),
            pl.BlockSpec((1, 1, N_MEM_HEADS * V_ROWS, m), lambda i, bb: (i, bb, 0, 0)),
        ],
        out_shape=[
            jax.ShapeDtypeStruct((depth, b, N_MEM_HEADS, m, MEM_WIDTH), BF16),
            jax.ShapeDtypeStruct((depth, b, N_MEM_HEADS * V_ROWS, m), BF16),
        ],
        compiler_params=_params("arbitrary", "arbitrary"),
        name="mem_kv",
    )(mem, mem_norm_g.reshape(1, d), wk, wvT)


def _moba_projection(h, wT_ref, w_ref, qT_ref, k_ref, kext_ref, vT_ref, kstat_ref, qmT_ref, z_ref):
    tm = h.shape[0]
    nb = tm // MOBA_BLOCK
    outT = lax.dot_general(wT_ref[...], h, _NT, preferred_element_type=F32)
    out = jnp.dot(h, w_ref[...], preferred_element_type=F32)
    ones = jnp.ones((ONES_ROWS, MOBA_BLOCK), BF16)
    for p in range(N_PAIRS):
        for bb in range(nb):
            cols = slice(bb * MOBA_BLOCK, (bb + 1) * MOBA_BLOCK)
            qT_ref[0, p, bb] = outT[p * LANES:(p + 1) * LANES, cols].astype(BF16)
            for hh in range(HEAD_PAIR):
                v0 = ATTN_WIDTH + (p * HEAD_PAIR + hh) * HEAD_DIM
                vT_ref[0, p, bb, hh * V_ROWS:hh * V_ROWS + HEAD_DIM] = (
                    outT[v0:v0 + HEAD_DIM, cols].astype(BF16))
                vT_ref[0, p, bb, hh * V_ROWS + HEAD_DIM:(hh + 1) * V_ROWS] = ones
    qmT_ref[0] = outT[2 * ATTN_WIDTH:, :].astype(BF16)
    k = out[:, :ATTN_WIDTH]
    z_ref[0] = out[:, ATTN_WIDTH:].astype(BF16)
    row = lax.broadcasted_iota(jnp.int32, (tm, LANES), 0)
    lane = lax.broadcasted_iota(jnp.int32, (tm, LANES), 1)
    blk = pl.program_id(1) * nb + row // MOBA_BLOCK
    s_rel = (row % MOBA_BLOCK).astype(F32)
    ext = jnp.where(lane < SEL_COLS, (lane == blk).astype(F32),
                    jnp.where(lane < SEL_COLS + ALIBI_COLS, s_rel, 0.0)).astype(BF16)
    k_bf16 = k.astype(BF16)
    k_ref[0] = k_bf16
    kext_ref[0] = ext
    k_abs = jnp.abs(k_bf16.astype(F32))
    for bb in range(nb):
        rows = slice(bb * MOBA_BLOCK, (bb + 1) * MOBA_BLOCK)
        kstat_ref[0, 0, 0, bb:bb + 1, :] = jnp.mean(k[rows], axis=0, keepdims=True)
        kstat_ref[0, 0, 1, bb:bb + 1, :] = jnp.max(k_abs[rows], axis=0, keepdims=True)


def _moba_projection_specs(b, s, tm):
    nblk = s // MOBA_BLOCK
    nb = tm // MOBA_BLOCK
    specs = [
        pl.BlockSpec((1, N_PAIRS, nb, LANES, MOBA_BLOCK), lambda bb, i: (bb, 0, i, 0, 0)),
        pl.BlockSpec((1, tm, ATTN_WIDTH), lambda bb, i: (bb, i, 0)),
        pl.BlockSpec((1, tm, LANES), lambda bb, i: (bb, i, 0)),
        pl.BlockSpec((1, N_PAIRS, nb, HEAD_PAIR * V_ROWS, MOBA_BLOCK),
                     lambda bb, i: (bb, 0, i, 0, 0)),
        pl.BlockSpec((1, 1, 2, nb, ATTN_WIDTH), lambda bb, i: (bb, i, 0, 0, 0)),
        pl.BlockSpec((1, MEM_WIDTH, tm), lambda bb, i: (bb, 0, i)),
        pl.BlockSpec((1, tm, BRANCH_WIDTH), lambda bb, i: (bb, i, 0)),
    ]
    shapes = [
        jax.ShapeDtypeStruct((b, N_PAIRS, nblk, LANES, MOBA_BLOCK), BF16),
        jax.ShapeDtypeStruct((b, s, ATTN_WIDTH), BF16),
        jax.ShapeDtypeStruct((b, s, LANES), BF16),
        jax.ShapeDtypeStruct((b, N_PAIRS, nblk, HEAD_PAIR * V_ROWS, MOBA_BLOCK), BF16),
        jax.ShapeDtypeStruct((b, s // tm, 2, nb, ATTN_WIDTH), F32),
        jax.ShapeDtypeStruct((b, MEM_WIDTH, s), BF16),
        jax.ShapeDtypeStruct((b, s, BRANCH_WIDTH), BF16),
    ]
    return specs, shapes


def _swa_projection(h, wT_ref, w_ref, qT_ref, k_ref, vT_ref, qmT_ref, z_ref):
    outT = lax.dot_general(wT_ref[...], h, _NT, preferred_element_type=F32)
    out = jnp.dot(h, w_ref[...], preferred_element_type=F32)
    tm = h.shape[0]
    qT_ref[0] = outT[:ATTN_WIDTH].astype(BF16)
    for g in range(N_KV_HEADS_B):
        v0 = ATTN_WIDTH + g * HEAD_DIM
        vT_ref[0, g * V_ROWS:g * V_ROWS + HEAD_DIM] = outT[v0:v0 + HEAD_DIM].astype(BF16)
        vT_ref[0, g * V_ROWS + HEAD_DIM:(g + 1) * V_ROWS] = jnp.ones((ONES_ROWS, tm), BF16)
    qmT_ref[0] = outT[ATTN_WIDTH + KV_WIDTH_B:].astype(BF16)
    k_ref[0] = out[:, :KV_WIDTH_B].astype(BF16)
    z_ref[0] = out[:, KV_WIDTH_B:].astype(BF16)


def _swa_projection_specs(b, s, tm):
    specs = [
        pl.BlockSpec((1, ATTN_WIDTH, tm), lambda bb, i: (bb, 0, i)),
        pl.BlockSpec((1, tm, KV_WIDTH_B), lambda bb, i: (bb, i, 0)),
        pl.BlockSpec((1, N_KV_HEADS_B * V_ROWS, tm), lambda bb, i: (bb, 0, i)),
        pl.BlockSpec((1, MEM_WIDTH, tm), lambda bb, i: (bb, 0, i)),
        pl.BlockSpec((1, tm, BRANCH_WIDTH), lambda bb, i: (bb, i, 0)),
    ]
    shapes = [
        jax.ShapeDtypeStruct((b, ATTN_WIDTH, s), BF16),
        jax.ShapeDtypeStruct((b, s, KV_WIDTH_B), BF16),
        jax.ShapeDtypeStruct((b, N_KV_HEADS_B * V_ROWS, s), BF16),
        jax.ShapeDtypeStruct((b, MEM_WIDTH, s), BF16),
        jax.ShapeDtypeStruct((b, s, BRANCH_WIDTH), BF16),
    ]
    return specs, shapes


_PROJECTIONS = {
    "moba": (_moba_projection, _moba_projection_specs),
    "swa": (_swa_projection, _swa_projection_specs),
}


def _split_in_weights(w_in, mixer):
    scale = HEAD_DIM ** -0.5 * LOG2E
    kvw = ATTN_WIDTH if mixer == "moba" else KV_WIDTH_B
    o = np.cumsum([0, ATTN_WIDTH, kvw, kvw, MEM_WIDTH, BRANCH_WIDTH])
    q, k, v, qm, z = (w_in[:, o[i]:o[i + 1]] for i in range(5))
    wT = jnp.concatenate([q * scale, v, qm * scale], axis=1).T.astype(BF16)
    w = jnp.concatenate([k, z], axis=1).astype(BF16)
    return wT, w


def _first_projection_kernel(x_ref, g_ref, wT_ref, w_ref, *out_refs, mixer):
    h = _rms_bf16(x_ref[0], g_ref[...])
    _PROJECTIONS[mixer][0](h, wT_ref, w_ref, *out_refs)


def _first_projection(x, g, w_in, mixer):
    b, s, d = x.shape
    tm = min(ROW_TILE, s)
    wT, w = _split_in_weights(w_in, mixer)
    specs, shapes = _PROJECTIONS[mixer][1](b, s, tm)
    return pl.pallas_call(
        functools.partial(_first_projection_kernel, mixer=mixer),
        grid=(b, s // tm),
        in_specs=[
            pl.BlockSpec((1, tm, d), lambda bb, i: (bb, i, 0)),
            pl.BlockSpec((1, d), lambda bb, i: (0, 0)),
            pl.BlockSpec(wT.shape, lambda bb, i: (0, 0)),
            pl.BlockSpec(w.shape, lambda bb, i: (0, 0)),
        ],
        out_specs=specs,
        out_shape=shapes,
        compiler_params=_params("parallel", "arbitrary"),
        name=f"in_proj_{mixer}",
    )(x, g.reshape(1, d), wT, w)


def _memory_attention_T(qmT, mk_ref, mvT_ref):
    sT = [jnp.dot(mk_ref[0, hh], qmT, preferred_element_type=F32)
          for hh in range(N_MEM_HEADS)]
    outs = []
    for hh in range(N_MEM_HEADS):
        p = jnp.exp2(sT[hh] - jnp.max(sT[hh], axis=0, keepdims=True))
        acc = jnp.dot(mvT_ref[0, hh * V_ROWS:(hh + 1) * V_ROWS, :], p.astype(BF16),
                      preferred_element_type=F32)
        outs.append(acc[:HEAD_DIM] * (1.0 / acc[HEAD_DIM:HEAD_DIM + 1]))
    return jnp.concatenate(outs, axis=0)


def _epilogue_kernel(y_ref, qmT_ref, z_ref, x_ref, mk_ref, mvT_ref, wout_ref, g_ref, *rest, mixer):
    ymem = _memory_attention_T(qmT_ref[0], mk_ref, mvT_ref).T
    y = jnp.concatenate([y_ref[0].astype(F32), ymem], axis=-1)
    z = z_ref[0].astype(F32)
    gated = (y * (z / (1.0 + jnp.exp(-z)))).astype(BF16)
    xn = x_ref[0] + jnp.dot(gated, wout_ref[...], preferred_element_type=F32)
    if mixer is None:
        (o_ref,) = rest
        ms = jnp.mean(xn * xn, axis=-1, keepdims=True)
        o_ref[0] = xn * lax.rsqrt(ms + RMS_EPS) * g_ref[...]
    else:
        wT_ref, w_ref, xo_ref = rest[:3]
        xo_ref[0] = xn
        _PROJECTIONS[mixer][0](_rms_bf16(xn, g_ref[...]), wT_ref, w_ref, *rest[3:])


def _epilogue(y_self, qmT, z, x, mem_k, mem_vT, layer, w_out, g, w_in_next=None, mixer=None):
    b, s, d = x.shape
    tm = min(ROW_TILE, s)
    m = mem_k.shape[3]
    in_specs = [
        pl.BlockSpec((1, tm, ATTN_WIDTH), lambda bb, i: (bb, i, 0)),
        pl.BlockSpec((1, MEM_WIDTH, tm), lambda bb, i: (bb, 0, i)),
        pl.BlockSpec((1, tm, BRANCH_WIDTH), lambda bb, i: (bb, i, 0)),
        pl.BlockSpec((1, tm, d), lambda bb, i: (bb, i, 0)),
        pl.BlockSpec((1, N_MEM_HEADS, m, MEM_WIDTH), lambda bb, i: (bb, 0, 0, 0)),
        pl.BlockSpec((1, N_MEM_HEADS * V_ROWS, m), lambda bb, i: (bb, 0, 0)),
        pl.BlockSpec((BRANCH_WIDTH, d), lambda bb, i: (0, 0)),
        pl.BlockSpec((1, d), lambda bb, i: (0, 0)),
    ]
    args = [y_self, qmT, z, x, mem_k[layer], mem_vT[layer], w_out.astype(BF16), g.reshape(1, d)]
    x_spec = pl.BlockSpec((1, tm, d), lambda bb, i: (bb, i, 0))
    x_shape = jax.ShapeDtypeStruct((b, s, d), F32)
    if mixer is None:
        out_specs, out_shape = [x_spec], [x_shape]
    else:
        wT, w = _split_in_weights(w_in_next, mixer)
        in_specs += [pl.BlockSpec(wT.shape, lambda bb, i: (0, 0)),
                     pl.BlockSpec(w.shape, lambda bb, i: (0, 0))]
        args += [wT, w]
        specs, shapes = _PROJECTIONS[mixer][1](b, s, tm)
        out_specs, out_shape = [x_spec] + specs, [x_shape] + shapes
    return pl.pallas_call(
        functools.partial(_epilogue_kernel, mixer=mixer),
        grid=(b, s // tm),
        in_specs=in_specs,
        out_specs=out_specs,
        out_shape=out_shape,
        compiler_params=_params("parallel", "arbitrary"),
        name=f"epilogue_{mixer}",
    )(*args)


def _moba_kernel(qT_ref, k_ref, kext_ref, vT_ref, kmean_ref, kabs_ref, srows_ref, slope_ref, o_ref,
                 qaug_ref):
    nblk = kmean_ref.shape[1]
    tq = qT_ref.shape[-1]
    kmean = kmean_ref[0]
    drow = lax.broadcasted_iota(jnp.int32, (LANES, tq), 0)
    klane = lax.broadcasted_iota(jnp.int32, kmean.shape, 1)
    blk = lax.broadcasted_iota(jnp.int32, (nblk, tq), 0)
    blk_f32 = blk.astype(F32)
    blk_col = lax.broadcasted_iota(jnp.int32, (nblk, 1), 0)
    krow = lax.broadcasted_iota(jnp.int32, (MOBA_BLOCK, tq), 0)
    qlane = lax.broadcasted_iota(jnp.int32, (MOBA_BLOCK, tq), 1)
    causal = krow <= qlane
    head_lanes = [(klane >= hh * HEAD_DIM) & (klane < (hh + 1) * HEAD_DIM) for hh in range(HEAD_PAIR)]
    head_rows = [(drow >= hh * HEAD_DIM) & (drow < (hh + 1) * HEAD_DIM) for hh in range(HEAD_PAIR)]
    no_choice = jnp.zeros((SEL_COLS, tq), BF16)

    def k_aug(n):
        return jnp.concatenate([k_ref[0, n], kext_ref[0, n]], axis=1)

    def weighted_values(n, hh, p):
        return jnp.dot(vT_ref[0, 0, n, hh * V_ROWS:(hh + 1) * V_ROWS, :], p.astype(BF16),
                       preferred_element_type=F32)

    chains = [(t, hh) for t in range(Q_BLOCKS_PER_STEP) for hh in range(HEAD_PAIR)]
    block_j = [pl.program_id(2) * Q_BLOCKS_PER_STEP + t for t in range(Q_BLOCKS_PER_STEP)]
    last_j = block_j[-1]
    qT = [qT_ref[0, 0, t] for t in range(Q_BLOCKS_PER_STEP)]
    q_abs = [jnp.abs(q) for q in qT]

    q_head = {(t, hh): jnp.where(head_rows[hh], qT[t], jnp.zeros_like(qT[t])) for t, hh in chains}
    kmean_h = [jnp.where(head_lanes[hh], kmean, 0.0).astype(BF16) for hh in range(HEAD_PAIR)]
    gates = {(t, hh): jnp.dot(kmean_h[hh], qT[t], preferred_element_type=F32)
             for t, hh in chains}
    own_sT = {(t, hh): jnp.dot(
        k_aug(block_j[t]), jnp.concatenate([q_head[t, hh], no_choice, srows_ref[0, hh]], axis=0),
        preferred_element_type=F32) for t, hh in chains}
    kabs_h = [jnp.where(head_lanes[hh], kabs_ref[0] * BF16_ROUND_UP, 0.0).astype(BF16)
              for hh in range(HEAD_PAIR)]
    qk_bound = {(t, hh): jnp.dot(kabs_h[hh], q_abs[t], preferred_element_type=F32)
                for t, hh in chains}

    own_max, own_acc = {}, {}
    for t, hh in chains:
        sT = jnp.where(causal, own_sT[t, hh], MASKED)
        own_max[t, hh] = jnp.max(sT, axis=0, keepdims=True)
        own_acc[t, hh] = weighted_values(block_j[t], hh, jnp.exp2(sT - own_max[t, hh]))

    first_needed = last_j
    for t in range(Q_BLOCKS_PER_STEP):
        j = block_j[t]
        needed = blk_col < 0
        for hh in range(HEAD_PAIR):
            qk_over_shift = jnp.max(qk_bound[t, hh] - own_max[t, hh], axis=1, keepdims=True)
            alibi = slope_ref[0, hh][:, :1] * ((MOBA_BLOCK - 1) - MOBA_BLOCK * (j - blk_col)).astype(F32)
            needed = needed | (qk_over_shift + alibi + BOUND_SLACK >= EXP2_IS_ZERO_BELOW)
        first_needed = jnp.minimum(first_needed, jnp.min(
            jnp.where(needed & (blk_col < j - NEAR_BLOCKS), blk_col, last_j).astype(F32)
        ).astype(jnp.int32))

    gate = {c: jnp.where(blk < block_j[c[0]], gates[c], -jnp.inf) for c in chains}
    bias = {c: jnp.full((nblk, tq), MASKED, F32) for c in chains}
    for k in range(MOBA_TOPK):
        for c in chains:
            best = jnp.max(gate[c], axis=0, keepdims=True)
            first = jnp.min(jnp.where(gate[c] == best, blk_f32, float(nblk)), axis=0, keepdims=True)
            pick = blk_f32 == jnp.where(best > -jnp.inf, first, float(nblk))
            bias[c] = jnp.where(pick, 0.0, bias[c])
            if k + 1 < MOBA_TOPK:
                gate[c] = jnp.where(pick, -jnp.inf, gate[c])
    for t, hh in chains:
        rows = bias[t, hh].astype(BF16)
        if nblk < SEL_COLS:
            rows = jnp.concatenate([rows, jnp.zeros((SEL_COLS - nblk, tq), BF16)], axis=0)
        qaug_ref[t, hh] = jnp.concatenate([q_head[t, hh], rows, srows_ref[0, hh]], axis=0)

    def scores(t, n, hh):
        return jnp.dot(k_aug(n), qaug_ref[t, hh], preferred_element_type=F32)

    def block_offset(t, n, hh):
        return slope_ref[0, hh] * ((n - block_j[t]) * MOBA_BLOCK).astype(F32)

    def finish(t, accs):
        oT = jnp.concatenate(
            [acc[:HEAD_DIM] * (1.0 / acc[HEAD_DIM:HEAD_DIM + 1]) for acc in accs], axis=0)
        o_ref[0, t * tq:(t + 1) * tq, :] = oT.T.astype(o_ref.dtype)

    def fixed_shift_blocks(first, count, accs):
        slots = [first + u for u in range(count)]
        blocks = [jnp.minimum(n, last_j) for n in slots]
        sT = [{c: scores(c[0], n, c[1]) for c in chains} for n in blocks]
        accs = list(accs)
        for u, n in enumerate(blocks):
            for i, (t, hh) in enumerate(chains):
                shift = jnp.where(slots[u] < block_j[t] - NEAR_BLOCKS,
                                  own_max[t, hh] - block_offset(t, n, hh), jnp.inf)
                accs[i] += weighted_values(n, hh, jnp.exp2(sT[u][t, hh] - shift))
        return accs

    near = [(t, hh, back) for back in range(1, NEAR_BLOCKS + 1) for t, hh in chains]
    near_sT = {c: scores(c[0], jnp.maximum(block_j[c[0]] - c[2], 0), c[1]) for c in near}
    fast = {c: own_acc[c] for c in chains}
    for t, hh, back in near:
        n = jnp.maximum(block_j[t] - back, 0)
        shift = jnp.where(block_j[t] >= back, own_max[t, hh] - block_offset(t, n, hh), jnp.inf)
        fast[t, hh] = fast[t, hh] + weighted_values(n, hh, jnp.exp2(near_sT[t, hh, back] - shift))
    fast = [fast[c] for c in chains]

    start = first_needed
    loop_end = last_j - NEAR_BLOCKS
    for unroll in TRIP_SIZES:
        left = loop_end - start
        trips = jnp.maximum((left + 1) // unroll if unroll == MASKED_TRIP else left // unroll, 0)
        fast = lax.fori_loop(
            0, trips,
            lambda g, a, start=start, unroll=unroll: fixed_shift_blocks(start + g * unroll, unroll, a),
            fast)
        start = start + trips * unroll

    in_range = []
    for t in range(Q_BLOCKS_PER_STEP):
        mine = fast[t * HEAD_PAIR:(t + 1) * HEAD_PAIR]
        finish(t, mine)
        denominators = functools.reduce(jnp.maximum, [a[HEAD_DIM:HEAD_DIM + 1] for a in mine])
        in_range.append(jnp.max(denominators) < OVERFLOW_GUARD)

    for t in range(Q_BLOCKS_PER_STEP):
        @pl.when(jnp.logical_not(in_range[t]))
        def _(t=t):
            def running_max_block(n, carry):
                out = []
                for hh in range(HEAD_PAIR):
                    m, acc = carry[2 * hh:2 * hh + 2]
                    sT = scores(t, n, hh)
                    c = block_offset(t, n, hh)
                    m_new = jnp.maximum(m, jnp.max(sT, axis=0, keepdims=True) + c)
                    acc = (jnp.exp2(m - m_new) * acc
                           + weighted_values(n, hh, jnp.exp2(sT - (m_new - c))))
                    out += [m_new, acc]
                return out

            slow = lax.fori_loop(
                0, block_j[t], running_max_block,
                [x for hh in range(HEAD_PAIR) for x in (own_max[t, hh], own_acc[t, hh])])
            finish(t, slow[1::2])


def _moba_tables(tq):
    slopes = _alibi_slopes(N_HEADS).astype(np.float64) * LOG2E
    pieces = _bf16_pieces(slopes)
    rows = np.zeros((N_HEADS, AUG - LANES - SEL_COLS, tq), np.float32)
    for c, piece in enumerate(pieces):
        rows[:, c, :] = piece[:, None]
    srows = jnp.asarray(rows.reshape(N_PAIRS, HEAD_PAIR, AUG - LANES - SEL_COLS, tq), BF16)
    svec = jnp.asarray(np.broadcast_to(slopes[:, None, None], (N_HEADS, 1, tq))
                       .reshape(N_PAIRS, HEAD_PAIR, 1, tq), F32)
    return srows, svec


def _moba_attention(qT, k, kext, vT, kstat):
    b, _, nblk, _, tq = qT.shape
    assert nblk <= SEL_COLS and tq == MOBA_BLOCK and nblk % Q_BLOCKS_PER_STEP == 0
    s = nblk * MOBA_BLOCK
    k = k.reshape(b, nblk, MOBA_BLOCK, ATTN_WIDTH)
    kext = kext.reshape(b, nblk, MOBA_BLOCK, LANES)
    kmean = kstat[:, :, 0].reshape(b, nblk, ATTN_WIDTH)
    kabs = kstat[:, :, 1].reshape(b, nblk, ATTN_WIDTH)
    srows, svec = _moba_tables(tq)
    return pl.pallas_call(
        _moba_kernel,
        grid=(b, N_PAIRS, nblk // Q_BLOCKS_PER_STEP),
        in_specs=[
            pl.BlockSpec((1, 1, Q_BLOCKS_PER_STEP, LANES, tq), lambda bb, p, j: (bb, p, j, 0, 0)),
            pl.BlockSpec((1, nblk, MOBA_BLOCK, LANES), lambda bb, p, j: (bb, 0, 0, p)),
            pl.BlockSpec((1, nblk, MOBA_BLOCK, LANES), lambda bb, p, j: (bb, 0, 0, 0)),
            pl.BlockSpec((1, 1, nblk, HEAD_PAIR * V_ROWS, MOBA_BLOCK),
                         lambda bb, p, j: (bb, p, 0, 0, 0)),
            pl.BlockSpec((1, nblk, LANES), lambda bb, p, j: (bb, 0, p)),
            pl.BlockSpec((1, nblk, LANES), lambda bb, p, j: (bb, 0, p)),
            pl.BlockSpec((1, HEAD_PAIR, AUG - LANES - SEL_COLS, tq), lambda bb, p, j: (p, 0, 0, 0)),
            pl.BlockSpec((1, HEAD_PAIR, 1, tq), lambda bb, p, j: (p, 0, 0, 0)),
        ],
        out_specs=pl.BlockSpec((1, Q_BLOCKS_PER_STEP * tq, LANES), lambda bb, p, j: (bb, j, p)),
        out_shape=jax.ShapeDtypeStruct((b, s, ATTN_WIDTH), BF16),
        scratch_shapes=[pltpu.VMEM((Q_BLOCKS_PER_STEP, HEAD_PAIR, AUG, tq), BF16)],
        compiler_params=_params("parallel", "parallel", "arbitrary"),
        name="moba_attention",
    )(qT, k, kext, vT, kmean, kabs, srows, svec)


def _swa_kernel(qT_ref, kp_ref, kc_ref, vTp_ref, vTc_ref, bias_ref, sink_ref, o_ref, yT_ref):
    tq = qT_ref.shape[-1]
    kwin = jnp.concatenate([kp_ref[0], kc_ref[0]], axis=0)
    vTwin = jnp.concatenate([vTp_ref[0], vTc_ref[0]], axis=1)
    zeros = jnp.zeros((HEAD_DIM, tq), BF16)
    group = N_HEADS // N_KV_HEADS_B

    def padded_q(h):
        q_h = qT_ref[0, h * HEAD_DIM:(h + 1) * HEAD_DIM, :]
        return jnp.concatenate([q_h, zeros] if h // group == 0 else [zeros, q_h], axis=0)

    worst = jnp.zeros((1, group * tq), F32)
    for g in range(N_KV_HEADS_B):
        heads = range(g * group, (g + 1) * group)
        sink = jnp.concatenate([sink_ref[h] * LOG2E for h in heads], axis=1)
        s_g = jnp.dot(kwin, jnp.concatenate([padded_q(h) for h in heads], axis=1),
                      preferred_element_type=F32)
        p = jnp.exp2(s_g + bias_ref[0, g] - sink).astype(BF16)
        acc = jnp.dot(vTwin[g * V_ROWS:(g + 1) * V_ROWS, :], p, preferred_element_type=F32)
        denom = acc[HEAD_DIM:HEAD_DIM + 1] + 1.0
        worst = jnp.maximum(worst, denom)
        out = acc[:HEAD_DIM] * (1.0 / denom)
        for i, h in enumerate(heads):
            yT_ref[h * HEAD_DIM:(h + 1) * HEAD_DIM, :] = out[:, i * tq:(i + 1) * tq]
    o_ref[0] = yT_ref[...].T.astype(o_ref.dtype)
    in_range = jnp.max(worst) < OVERFLOW_GUARD

    @pl.when(jnp.logical_not(in_range))
    def _():
        for h in range(N_HEADS):
            g, i = divmod(h, group)
            s_h = (jnp.dot(kwin, padded_q(h), preferred_element_type=F32)
                   + bias_ref[0, g, :, i * tq:(i + 1) * tq])
            sink = sink_ref[h] * LOG2E
            m = jnp.maximum(jnp.max(s_h, axis=0, keepdims=True), sink)
            acc = jnp.dot(vTwin[g * V_ROWS:(g + 1) * V_ROWS, :], jnp.exp2(s_h - m).astype(BF16),
                          preferred_element_type=F32)
            denom = acc[HEAD_DIM:HEAD_DIM + 1] + jnp.exp2(sink - m)
            yT_ref[h * HEAD_DIM:(h + 1) * HEAD_DIM, :] = acc[:HEAD_DIM] * (1.0 / denom)
        o_ref[0] = yT_ref[...].T.astype(o_ref.dtype)


def _swa_bias_table(tq):
    slopes = _alibi_slopes(N_HEADS).astype(np.float64) * LOG2E
    kw = np.arange(WINDOW + tq)[:, None]
    q = np.arange(tq)[None, :]
    dist = q + WINDOW - kw
    ok = (dist >= 0) & (dist < WINDOW)
    first_tile = ok & (kw >= WINDOW)
    table = np.stack([np.where(valid[None], -slopes[:, None, None] * dist[None], MASKED)
                      for valid in (first_tile, ok)])
    table = table.reshape(2, N_KV_HEADS_B, N_HEADS // N_KV_HEADS_B, WINDOW + tq, tq)
    table = np.moveaxis(table, 2, 3).reshape(2, N_KV_HEADS_B, WINDOW + tq, -1)
    return jnp.asarray(table, F32)


def _swa_attention(qT, k, vT, sinks):
    b, _, s = qT.shape
    tq = min(SWA_TILE, s)
    r = tq // WINDOW
    bias = _swa_bias_table(tq)
    sink_rows = jnp.broadcast_to(sinks.astype(F32)[:, None, None], (N_HEADS, 1, tq))
    prev = lambda j: jnp.maximum(j * r - 1, 0)
    return pl.pallas_call(
        _swa_kernel,
        grid=(b, s // tq),
        in_specs=[
            pl.BlockSpec((1, ATTN_WIDTH, tq), lambda bb, j: (bb, 0, j)),
            pl.BlockSpec((1, WINDOW, KV_WIDTH_B), lambda bb, j: (bb, prev(j), 0)),
            pl.BlockSpec((1, tq, KV_WIDTH_B), lambda bb, j: (bb, j, 0)),
            pl.BlockSpec((1, N_KV_HEADS_B * V_ROWS, WINDOW), lambda bb, j: (bb, 0, prev(j))),
            pl.BlockSpec((1, N_KV_HEADS_B * V_ROWS, tq), lambda bb, j: (bb, 0, j)),
            pl.BlockSpec((1,) + bias.shape[1:], lambda bb, j: (jnp.minimum(j, 1), 0, 0, 0)),
            pl.BlockSpec(sink_rows.shape, lambda bb, j: (0, 0, 0)),
        ],
        out_specs=pl.BlockSpec((1, tq, ATTN_WIDTH), lambda bb, j: (bb, j, 0)),
        out_shape=jax.ShapeDtypeStruct((b, s, ATTN_WIDTH), BF16),
        scratch_shapes=[pltpu.VMEM((ATTN_WIDTH, tq), F32)],
        compiler_params=_params("parallel", "arbitrary"),
        name="swa_attention",
    )(qT, k, k, vT, vT, bias, sink_rows)


def kernel(x, mem, norm_g, w_in_a, w_in_b, sinks_b, w_mem_kv, w_out, mem_norm_g, final_norm_g):
    depth = norm_g.shape[0]
    b, s, _ = x.shape
    assert s % ROW_TILE == 0 or s < ROW_TILE
    mem_k, mem_vT = _mem_kv(mem, mem_norm_g, w_mem_kv)
    mixers = ["moba" if i % 2 == 0 else "swa" for i in range(depth)]
    w_in = [w_in_a[i // 2] if i % 2 == 0 else w_in_b[i // 2] for i in range(depth)]
    proj = _first_projection(x, norm_g[0], w_in[0], mixers[0])
    for i in range(depth):
        if mixers[i] == "moba":
            qT, k, kext, vT, kstat, qmT, z = proj
            y_self = _moba_attention(qT, k, kext, vT, kstat)
        else:
            qT, k, vT, qmT, z = proj
            y_self = _swa_attention(qT, k, vT, sinks_b[i // 2])
        if i + 1 < depth:
            x, *proj = _epilogue(y_self, qmT, z, x, mem_k, mem_vT, i, w_out[i], norm_g[i + 1],
                                 w_in[i + 1], mixers[i + 1])
        else:
            (x,) = _epilogue(y_self, qmT, z, x, mem_k, mem_vT, i, w_out[i], final_norm_g)
    return x
```

```python
import functools
import math

import jax
import jax.numpy as jnp
import numpy as np
from jax import lax
from jax.experimental import pallas as pl
from jax.experimental.pallas import tpu as pltpu

HEAD_DIM = 64
N_HEADS = 12
N_KV_HEADS_B = 2
N_MEM_HEADS = 4
ATTN_WIDTH = N_HEADS * HEAD_DIM
MEM_WIDTH = N_MEM_HEADS * HEAD_DIM
BRANCH_WIDTH = ATTN_WIDTH + MEM_WIDTH
KV_WIDTH_B = N_KV_HEADS_B * HEAD_DIM
MOBA_BLOCK = 256
MOBA_TOPK = 3
WINDOW = 128
RMS_EPS = 1e-6

LANES = 128
HEAD_PAIR = LANES // HEAD_DIM
N_PAIRS = N_HEADS // HEAD_PAIR
SEL_COLS = 64
ALIBI_COLS = 4
ONES_ROWS = 16
V_ROWS = HEAD_DIM + ONES_ROWS
LOG2E = math.log2(math.e)
AUG = 2 * LANES
MASKED = -(2.0 ** 30)
OVERFLOW_GUARD = 2.0 ** 100
EXP2_IS_ZERO_BELOW = -135.0
BOUND_SLACK = 1.0
BF16_ROUND_UP = 1.0 + 2.0 ** -7
TRIP_SIZES = (16, 4, 2, 1)
MASKED_TRIP = 4
Q_BLOCKS_PER_STEP = 2
NEAR_BLOCKS = 2
ROW_TILE = 512
SWA_TILE = 256
VMEM_LIMIT = 56 * 1024 * 1024

F32 = jnp.float32
BF16 = jnp.bfloat16
_NT = (((1,), (1,)), ((), ()))


def _alibi_slopes(n_heads):
    def pow2_slopes(n):
        start = 2.0 ** (-8.0 / n)
        return [start ** (i + 1) for i in range(n)]
    if math.log2(n_heads).is_integer():
        vals = pow2_slopes(n_heads)
    else:
        c = 2 ** math.floor(math.log2(n_heads))
        vals = pow2_slopes(c) + pow2_slopes(2 * c)[0::2][: n_heads - c]
    return np.array(vals, dtype=np.float32)


def _bf16_pieces(v):
    rest = np.asarray(v, np.float64)
    pieces = []
    for _ in range(ALIBI_COLS):
        p = rest.astype(BF16).astype(np.float64)
        pieces.append(p.astype(np.float32))
        rest = rest - p
    assert np.all(np.abs(rest) <= np.abs(v) * 2.0 ** -30)
    return pieces


def _rms_bf16(x, g):
    ms = jnp.mean(x * x, axis=-1, keepdims=True)
    return (x * lax.rsqrt(ms + RMS_EPS) * g).astype(BF16)


def _params(*sem):
    return pltpu.CompilerParams(dimension_semantics=sem, vmem_limit_bytes=VMEM_LIMIT)


def _mem_kv_kernel(mem_ref, g_ref, wk_ref, wvT_ref, k_ref, vT_ref):
    h = _rms_bf16(mem_ref[0], g_ref[...])
    k = jnp.dot(h, wk_ref[0], preferred_element_type=F32)
    vT = lax.dot_general(wvT_ref[0], h, _NT, preferred_element_type=F32)
    lane = lax.broadcasted_iota(jnp.int32, k.shape, 1)
    for hh in range(N_MEM_HEADS):
        mine = (lane >= hh * HEAD_DIM) & (lane < (hh + 1) * HEAD_DIM)
        k_ref[0, 0, hh] = jnp.where(mine, k, 0.0).astype(BF16)
        vT_ref[0, 0, hh * V_ROWS:hh * V_ROWS + HEAD_DIM] = (
            vT[hh * HEAD_DIM:(hh + 1) * HEAD_DIM].astype(BF16))
        vT_ref[0, 0, hh * V_ROWS + HEAD_DIM:(hh + 1) * V_ROWS] = jnp.ones(
            (ONES_ROWS, vT.shape[1]), BF16)


def _mem_kv(mem, mem_norm_g, w_mem_kv):
    b, m, d = mem.shape
    depth = w_mem_kv.shape[0]
    wk = w_mem_kv[:, :, :MEM_WIDTH].astype(BF16)
    wvT = jnp.swapaxes(w_mem_kv[:, :, MEM_WIDTH:], 1, 2).astype(BF16)
    return pl.pallas_call(
        _mem_kv_kernel,
        grid=(depth, b),
        in_specs=[
            pl.BlockSpec((1, m, d), lambda i, bb: (bb, 0, 0)),
            pl.BlockSpec((1, d), lambda i, bb: (0, 0)),
            pl.BlockSpec((1, d, MEM_WIDTH), lambda i, bb: (i, 0, 0)),
            pl.BlockSpec((1, MEM_WIDTH, d), lambda i, bb: (i, 0, 0)),
        ],
        out_specs=[
            pl.BlockSpec((1, 1, N_MEM_HEADS, m, MEM_WIDTH), lambda i, bb: (i, bb, 0, 0, 0)),
            pl.BlockSpec((1, 1, N_MEM_HEADS * V_ROWS, m), lambda i, bb: (i, bb, 0, 0)),
        ],
        out_shape=[
            jax.ShapeDtypeStruct((depth, b, N_MEM_HEADS, m, MEM_WIDTH), BF16),
            jax.ShapeDtypeStruct((depth, b, N_MEM_HEADS * V_ROWS, m), BF16),
        ],
        compiler_params=_params("arbitrary", "arbitrary"),
        name="mem_kv",
    )(mem, mem_norm_g.reshape(1, d), wk, wvT)


def _moba_projection(h, wT_ref, w_ref, qT_ref, k_ref, kext_ref, vT_ref, kstat_ref, qmT_ref, z_ref):
    tm = h.shape[0]
    nb = tm // MOBA_BLOCK
    outT = lax.dot_general(wT_ref[...], h, _NT, preferred_element_type=F32)
    out = jnp.dot(h, w_ref[...], preferred_element_type=F32)
    ones = jnp.ones((ONES_ROWS, MOBA_BLOCK), BF16)
    for p in range(N_PAIRS):
        for bb in range(nb):
            cols = slice(bb * MOBA_BLOCK, (bb + 1) * MOBA_BLOCK)
            qT_ref[0, p, bb] = outT[p * LANES:(p + 1) * LANES, cols].astype(BF16)
            for hh in range(HEAD_PAIR):
                v0 = ATTN_WIDTH + (p * HEAD_PAIR + hh) * HEAD_DIM
                vT_ref[0, p, bb, hh * V_ROWS:hh * V_ROWS + HEAD_DIM] = (
                    outT[v0:v0 + HEAD_DIM, cols].astype(BF16))
                vT_ref[0, p, bb, hh * V_ROWS + HEAD_DIM:(hh + 1) * V_ROWS] = ones
    qmT_ref[0] = outT[2 * ATTN_WIDTH:, :].astype(BF16)
    k = out[:, :ATTN_WIDTH]
    z_ref[0] = out[:, ATTN_WIDTH:].astype(BF16)
    row = lax.broadcasted_iota(jnp.int32, (tm, LANES), 0)
    lane = lax.broadcasted_iota(jnp.int32, (tm, LANES), 1)
    blk = pl.program_id(1) * nb + row // MOBA_BLOCK
    s_rel = (row % MOBA_BLOCK).astype(F32)
    ext = jnp.where(lane < SEL_COLS, (lane == blk).astype(F32),
                    jnp.where(lane < SEL_COLS + ALIBI_COLS, s_rel, 0.0)).astype(BF16)
    k_bf16 = k.astype(BF16)
    k_ref[0] = k_bf16
    kext_ref[0] = ext
    k_abs = jnp.abs(k_bf16.astype(F32))
    for bb in range(nb):
        rows = slice(bb * MOBA_BLOCK, (bb + 1) * MOBA_BLOCK)
        kstat_ref[0, 0, 0, bb:bb + 1, :] = jnp.mean(k[rows], axis=0, keepdims=True)
        kstat_ref[0, 0, 1, bb:bb + 1, :] = jnp.max(k_abs[rows], axis=0, keepdims=True)


def _moba_projection_specs(b, s, tm):
    nblk = s // MOBA_BLOCK
    nb = tm // MOBA_BLOCK
    specs = [
        pl.BlockSpec((1, N_PAIRS, nb, LANES, MOBA_BLOCK), lambda bb, i: (bb, 0, i, 0, 0)),
        pl.BlockSpec((1, tm, ATTN_WIDTH), lambda bb, i: (bb, i, 0)),
        pl.BlockSpec((1, tm, LANES), lambda bb, i: (bb, i, 0)),
        pl.BlockSpec((1, N_PAIRS, nb, HEAD_PAIR * V_ROWS, MOBA_BLOCK),
                     lambda bb, i: (bb, 0, i, 0, 0)),
        pl.BlockSpec((1, 1, 2, nb, ATTN_WIDTH), lambda bb, i: (bb, i, 0, 0, 0)),
        pl.BlockSpec((1, MEM_WIDTH, tm), lambda bb, i: (bb, 0, i)),
        pl.BlockSpec((1, tm, BRANCH_WIDTH), lambda bb, i: (bb, i, 0)),
    ]
    shapes = [
        jax.ShapeDtypeStruct((b, N_PAIRS, nblk, LANES, MOBA_BLOCK), BF16),
        jax.ShapeDtypeStruct((b, s, ATTN_WIDTH), BF16),
        jax.ShapeDtypeStruct((b, s, LANES), BF16),
        jax.ShapeDtypeStruct((b, N_PAIRS, nblk, HEAD_PAIR * V_ROWS, MOBA_BLOCK), BF16),
        jax.ShapeDtypeStruct((b, s // tm, 2, nb, ATTN_WIDTH), F32),
        jax.ShapeDtypeStruct((b, MEM_WIDTH, s), BF16),
        jax.ShapeDtypeStruct((b, s, BRANCH_WIDTH), BF16),
    ]
    return specs, shapes


def _swa_projection(h, wT_ref, w_ref, qT_ref, k_ref, vT_ref, qmT_ref, z_ref):
    outT = lax.dot_general(wT_ref[...], h, _NT, preferred_element_type=F32)
    out = jnp.dot(h, w_ref[...], preferred_element_type=F32)
    tm = h.shape[0]
    qT_ref[0] = outT[:ATTN_WIDTH].astype(BF16)
    for g in range(N_KV_HEADS_B):
        v0 = ATTN_WIDTH + g * HEAD_DIM
        vT_ref[0, g * V_ROWS:g * V_ROWS + HEAD_DIM] = outT[v0:v0 + HEAD_DIM].astype(BF16)
        vT_ref[0, g * V_ROWS + HEAD_DIM:(g + 1) * V_ROWS] = jnp.ones((ONES_ROWS, tm), BF16)
    qmT_ref[0] = outT[ATTN_WIDTH + KV_WIDTH_B:].astype(BF16)
    k_ref[0] = out[:, :KV_WIDTH_B].astype(BF16)
    z_ref[0] = out[:, KV_WIDTH_B:].astype(BF16)


def _swa_projection_specs(b, s, tm):
    specs = [
        pl.BlockSpec((1, ATTN_WIDTH, tm), lambda bb, i: (bb, 0, i)),
        pl.BlockSpec((1, tm, KV_WIDTH_B), lambda bb, i: (bb, i, 0)),
        pl.BlockSpec((1, N_KV_HEADS_B * V_ROWS, tm), lambda bb, i: (bb, 0, i)),
        pl.BlockSpec((1, MEM_WIDTH, tm), lambda bb, i: (bb, 0, i)),
        pl.BlockSpec((1, tm, BRANCH_WIDTH), lambda bb, i: (bb, i, 0)),
    ]
    shapes = [
        jax.ShapeDtypeStruct((b, ATTN_WIDTH, s), BF16),
        jax.ShapeDtypeStruct((b, s, KV_WIDTH_B), BF16),
        jax.ShapeDtypeStruct((b, N_KV_HEADS_B * V_ROWS, s), BF16),
        jax.ShapeDtypeStruct((b, MEM_WIDTH, s), BF16),
        jax.ShapeDtypeStruct((b, s, BRANCH_WIDTH), BF16),
    ]
    return specs, shapes


_PROJECTIONS = {
    "moba": (_moba_projection, _moba_projection_specs),
    "swa": (_swa_projection, _swa_projection_specs),
}


def _split_in_weights(w_in, mixer):
    scale = HEAD_DIM ** -0.5 * LOG2E
    kvw = ATTN_WIDTH if mixer == "moba" else KV_WIDTH_B
    o = np.cumsum([0, ATTN_WIDTH, kvw, kvw, MEM_WIDTH, BRANCH_WIDTH])
    q, k, v, qm, z = (w_in[:, o[i]:o[i + 1]] for i in range(5))
    wT = jnp.concatenate([q * scale, v, qm * scale], axis=1).T.astype(BF16)
    w = jnp.concatenate([k, z], axis=1).astype(BF16)
    return wT, w


def _first_projection_kernel(x_ref, g_ref, wT_ref, w_ref, *out_refs, mixer):
    h = _rms_bf16(x_ref[0], g_ref[...])
    _PROJECTIONS[mixer][0](h, wT_ref, w_ref, *out_refs)


def _first_projection(x, g, w_in, mixer):
    b, s, d = x.shape
    tm = min(ROW_TILE, s)
    wT, w = _split_in_weights(w_in, mixer)
    specs, shapes = _PROJECTIONS[mixer][1](b, s, tm)
    return pl.pallas_call(
        functools.partial(_first_projection_kernel, mixer=mixer),
        grid=(b, s // tm),
        in_specs=[
            pl.BlockSpec((1, tm, d), lambda bb, i: (bb, i, 0)),
            pl.BlockSpec((1, d), lambda bb, i: (0, 0)),
            pl.BlockSpec(wT.shape, lambda bb, i: (0, 0)),
            pl.BlockSpec(w.shape, lambda bb, i: (0, 0)),
        ],
        out_specs=specs,
        out_shape=shapes,
        compiler_params=_params("parallel", "arbitrary"),
        name=f"in_proj_{mixer}",
    )(x, g.reshape(1, d), wT, w)


def _memory_attention_T(qmT, mk_ref, mvT_ref):
    sT = [jnp.dot(mk_ref[0, hh], qmT, preferred_element_type=F32)
          for hh in range(N_MEM_HEADS)]
    outs = []
    for hh in range(N_MEM_HEADS):
        p = jnp.exp2(sT[hh] - jnp.max(sT[hh], axis=0, keepdims=True))
        acc = jnp.dot(mvT_ref[0, hh * V_ROWS:(hh + 1) * V_ROWS, :], p.astype(BF16),
                      preferred_element_type=F32)
        outs.append(acc[:HEAD_DIM] * (1.0 / acc[HEAD_DIM:HEAD_DIM + 1]))
    return jnp.concatenate(outs, axis=0)


def _epilogue_kernel(y_ref, qmT_ref, z_ref, x_ref, mk_ref, mvT_ref, wout_ref, g_ref, *rest, mixer):
    ymem = _memory_attention_T(qmT_ref[0], mk_ref, mvT_ref).T
    y = jnp.concatenate([y_ref[0].astype(F32), ymem], axis=-1)
    z = z_ref[0].astype(F32)
    gated = (y * (z / (1.0 + jnp.exp(-z)))).astype(BF16)
    xn = x_ref[0] + jnp.dot(gated, wout_ref[...], preferred_element_type=F32)
    if mixer is None:
        (o_ref,) = rest
        ms = jnp.mean(xn * xn, axis=-1, keepdims=True)
        o_ref[0] = xn * lax.rsqrt(ms + RMS_EPS) * g_ref[...]
    else:
        wT_ref, w_ref, xo_ref = rest[:3]
        xo_ref[0] = xn
        _PROJECTIONS[mixer][0](_rms_bf16(xn, g_ref[...]), wT_ref, w_ref, *rest[3:])


def _epilogue(y_self, qmT, z, x, mem_k, mem_vT, layer, w_out, g, w_in_next=None, mixer=None):
    b, s, d = x.shape
    tm = min(ROW_TILE, s)
    m = mem_k.shape[3]
    in_specs = [
        pl.BlockSpec((1, tm, ATTN_WIDTH), lambda bb, i: (bb, i, 0)),
        pl.BlockSpec((1, MEM_WIDTH, tm), lambda bb, i: (bb, 0, i)),
        pl.BlockSpec((1, tm, BRANCH_WIDTH), lambda bb, i: (bb, i, 0)),
        pl.BlockSpec((1, tm, d), lambda bb, i: (bb, i, 0)),
        pl.BlockSpec((1, N_MEM_HEADS, m, MEM_WIDTH), lambda bb, i: (bb, 0, 0, 0)),
        pl.BlockSpec((1, N_MEM_HEADS * V_ROWS, m), lambda bb, i: (bb, 0, 0)),
        pl.BlockSpec((BRANCH_WIDTH, d), lambda bb, i: (0, 0)),
        pl.BlockSpec((1, d), lambda bb, i: (0, 0)),
    ]
    args = [y_self, qmT, z, x, mem_k[layer], mem_vT[layer], w_out.astype(BF16), g.reshape(1, d)]
    x_spec = pl.BlockSpec((1, tm, d), lambda bb, i: (bb, i, 0))
    x_shape = jax.ShapeDtypeStruct((b, s, d), F32)
    if mixer is None:
        out_specs, out_shape = [x_spec], [x_shape]
    else:
        wT, w = _split_in_weights(w_in_next, mixer)
        in_specs += [pl.BlockSpec(wT.shape, lambda bb, i: (0, 0)),
                     pl.BlockSpec(w.shape, lambda bb, i: (0, 0))]
        args += [wT, w]
        specs, shapes = _PROJECTIONS[mixer][1](b, s, tm)
        out_specs, out_shape = [x_spec] + specs, [x_shape] + shapes
    return pl.pallas_call(
        functools.partial(_epilogue_kernel, mixer=mixer),
        grid=(b, s // tm),
        in_specs=in_specs,
        out_specs=out_specs,
        out_shape=out_shape,
        compiler_params=_params("parallel", "arbitrary"),
        name=f"epilogue_{mixer}",
    )(*args)


def _moba_kernel(qT_ref, k_ref, kext_ref, vT_ref, kmean_ref, kabs_ref, srows_ref, slope_ref, o_ref,
                 qaug_ref):
    nblk = kmean_ref.shape[1]
    tq = qT_ref.shape[-1]
    kmean = kmean_ref[0]
    drow = lax.broadcasted_iota(jnp.int32, (LANES, tq), 0)
    klane = lax.broadcasted_iota(jnp.int32, kmean.shape, 1)
    blk = lax.broadcasted_iota(jnp.int32, (nblk, tq), 0)
    blk_f32 = blk.astype(F32)
    blk_col = lax.broadcasted_iota(jnp.int32, (nblk, 1), 0)
    krow = lax.broadcasted_iota(jnp.int32, (MOBA_BLOCK, tq), 0)
    qlane = lax.broadcasted_iota(jnp.int32, (MOBA_BLOCK, tq), 1)
    causal = krow <= qlane
    head_lanes = [(klane >= hh * HEAD_DIM) & (klane < (hh + 1) * HEAD_DIM) for hh in range(HEAD_PAIR)]
    head_rows = [(drow >= hh * HEAD_DIM) & (drow < (hh + 1) * HEAD_DIM) for hh in range(HEAD_PAIR)]
    no_choice = jnp.zeros((SEL_COLS, tq), BF16)

    def k_aug(n):
        return jnp.concatenate([k_ref[0, n], kext_ref[0, n]], axis=1)

    def weighted_values(n, hh, p):
        return jnp.dot(vT_ref[0, 0, n, hh * V_ROWS:(hh + 1) * V_ROWS, :], p.astype(BF16),
                       preferred_element_type=F32)

    chains = [(t, hh) for t in range(Q_BLOCKS_PER_STEP) for hh in range(HEAD_PAIR)]
    block_j = [pl.program_id(2) * Q_BLOCKS_PER_STEP + t for t in range(Q_BLOCKS_PER_STEP)]
    last_j = block_j[-1]
    qT = [qT_ref[0, 0, t] for t in range(Q_BLOCKS_PER_STEP)]
    q_abs = [jnp.abs(q) for q in qT]

    q_head = {(t, hh): jnp.where(head_rows[hh], qT[t], jnp.zeros_like(qT[t])) for t, hh in chains}
    kmean_h = [jnp.where(head_lanes[hh], kmean, 0.0).astype(BF16) for hh in range(HEAD_PAIR)]
    gates = {(t, hh): jnp.dot(kmean_h[hh], qT[t], preferred_element_type=F32)
             for t, hh in chains}
    own_sT = {(t, hh): jnp.dot(
        k_aug(block_j[t]), jnp.concatenate([q_head[t, hh], no_choice, srows_ref[0, hh]], axis=0),
        preferred_element_type=F32) for t, hh in chains}
    kabs_h = [jnp.where(head_lanes[hh], kabs_ref[0] * BF16_ROUND_UP, 0.0).astype(BF16)
              for hh in range(HEAD_PAIR)]
    qk_bound = {(t, hh): jnp.dot(kabs_h[hh], q_abs[t], preferred_element_type=F32)
                for t, hh in chains}

    own_max, own_acc = {}, {}
    for t, hh in chains:
        sT = jnp.where(causal, own_sT[t, hh], MASKED)
        own_max[t, hh] = jnp.max(sT, axis=0, keepdims=True)
        own_acc[t, hh] = weighted_values(block_j[t], hh, jnp.exp2(sT - own_max[t, hh]))

    first_needed = last_j
    for t in range(Q_BLOCKS_PER_STEP):
        j = block_j[t]
        needed = blk_col < 0
        for hh in range(HEAD_PAIR):
            qk_over_shift = jnp.max(qk_bound[t, hh] - own_max[t, hh], axis=1, keepdims=True)
            alibi = slope_ref[0, hh][:, :1] * ((MOBA_BLOCK - 1) - MOBA_BLOCK * (j - blk_col)).astype(F32)
            needed = needed | (qk_over_shift + alibi + BOUND_SLACK >= EXP2_IS_ZERO_BELOW)
        first_needed = jnp.minimum(first_needed, jnp.min(
            jnp.where(needed & (blk_col < j - NEAR_BLOCKS), blk_col, last_j).astype(F32)
        ).astype(jnp.int32))

    gate = {c: jnp.where(blk < block_j[c[0]], gates[c], -jnp.inf) for c in chains}
    bias = {c: jnp.full((nblk, tq), MASKED, F32) for c in chains}
    for k in range(MOBA_TOPK):
        for c in chains:
            best = jnp.max(gate[c], axis=0, keepdims=True)
            first = jnp.min(jnp.where(gate[c] == best, blk_f32, float(nblk)), axis=0, keepdims=True)
            pick = blk_f32 == jnp.where(best > -jnp.inf, first, float(nblk))
            bias[c] = jnp.where(pick, 0.0, bias[c])
            if k + 1 < MOBA_TOPK:
                gate[c] = jnp.where(pick, -jnp.inf, gate[c])
    for t, hh in chains:
        rows = bias[t, hh].astype(BF16)
        if nblk < SEL_COLS:
            rows = jnp.concatenate([rows, jnp.zeros((SEL_COLS - nblk, tq), BF16)], axis=0)
        qaug_ref[t, hh] = jnp.concatenate([q_head[t, hh], rows, srows_ref[0, hh]], axis=0)

    def scores(t, n, hh):
        return jnp.dot(k_aug(n), qaug_ref[t, hh], preferred_element_type=F32)

    def block_offset(t, n, hh):
        return slope_ref[0, hh] * ((n - block_j[t]) * MOBA_BLOCK).astype(F32)

    def finish(t, accs):
        oT = jnp.concatenate(
            [acc[:HEAD_DIM] * (1.0 / acc[HEAD_DIM:HEAD_DIM + 1]) for acc in accs], axis=0)
        o_ref[0, t * tq:(t + 1) * tq, :] = oT.T.astype(o_ref.dtype)

    def fixed_shift_blocks(first, count, accs):
        slots = [first + u for u in range(count)]
        blocks = [jnp.minimum(n, last_j) for n in slots]
        sT = [{c: scores(c[0], n, c[1]) for c in chains} for n in blocks]
        accs = list(accs)
        for u, n in enumerate(blocks):
            for i, (t, hh) in enumerate(chains):
                shift = jnp.where(slots[u] < block_j[t] - NEAR_BLOCKS,
                                  own_max[t, hh] - block_offset(t, n, hh), jnp.inf)
                accs[i] += weighted_values(n, hh, jnp.exp2(sT[u][t, hh] - shift))
        return accs

    near = [(t, hh, back) for back in range(1, NEAR_BLOCKS + 1) for t, hh in chains]
    near_sT = {c: scores(c[0], jnp.maximum(block_j[c[0]] - c[2], 0), c[1]) for c in near}
    fast = {c: own_acc[c] for c in chains}
    for t, hh, back in near:
        n = jnp.maximum(block_j[t] - back, 0)
        shift = jnp.where(block_j[t] >= back, own_max[t, hh] - block_offset(t, n, hh), jnp.inf)
        fast[t, hh] = fast[t, hh] + weighted_values(n, hh, jnp.exp2(near_sT[t, hh, back] - shift))
    fast = [fast[c] for c in chains]

    start = first_needed
    loop_end = last_j - NEAR_BLOCKS
    for unroll in TRIP_SIZES:
        left = loop_end - start
        trips = jnp.maximum((left + 1) // unroll if unroll == MASKED_TRIP else left // unroll, 0)
        fast = lax.fori_loop(
            0, trips,
            lambda g, a, start=start, unroll=unroll: fixed_shift_blocks(start + g * unroll, unroll, a),
            fast)
        start = start + trips * unroll

    in_range = []
    for t in range(Q_BLOCKS_PER_STEP):
        mine = fast[t * HEAD_PAIR:(t + 1) * HEAD_PAIR]
        finish(t, mine)
        denominators = functools.reduce(jnp.maximum, [a[HEAD_DIM:HEAD_DIM + 1] for a in mine])
        in_range.append(jnp.max(denominators) < OVERFLOW_GUARD)

    for t in range(Q_BLOCKS_PER_STEP):
        @pl.when(jnp.logical_not(in_range[t]))
        def _(t=t):
            def running_max_block(n, carry):
                out = []
                for hh in range(HEAD_PAIR):
                    m, acc = carry[2 * hh:2 * hh + 2]
                    sT = scores(t, n, hh)
                    c = block_offset(t, n, hh)
                    m_new = jnp.maximum(m, jnp.max(sT, axis=0, keepdims=True) + c)
                    acc = (jnp.exp2(m - m_new) * acc
                           + weighted_values(n, hh, jnp.exp2(sT - (m_new - c))))
                    out += [m_new, acc]
                return out

            slow = lax.fori_loop(
                0, block_j[t], running_max_block,
                [x for hh in range(HEAD_PAIR) for x in (own_max[t, hh], own_acc[t, hh])])
            finish(t, slow[1::2])


def _moba_tables(tq):
    slopes = _alibi_slopes(N_HEADS).astype(np.float64) * LOG2E
    pieces = _bf16_pieces(slopes)
    rows = np.zeros((N_HEADS, AUG - LANES - SEL_COLS, tq), np.float32)
    for c, piece in enumerate(pieces):
        rows[:, c, :] = piece[:, None]
    srows = jnp.asarray(rows.reshape(N_PAIRS, HEAD_PAIR, AUG - LANES - SEL_COLS, tq), BF16)
    svec = jnp.asarray(np.broadcast_to(slopes[:, None, None], (N_HEADS, 1, tq))
                       .reshape(N_PAIRS, HEAD_PAIR, 1, tq), F32)
    return srows, svec


def _moba_attention(qT, k, kext, vT, kstat):
    b, _, nblk, _, tq = qT.shape
    assert nblk <= SEL_COLS and tq == MOBA_BLOCK and nblk % Q_BLOCKS_PER_STEP == 0
    s = nblk * MOBA_BLOCK
    k = k.reshape(b, nblk, MOBA_BLOCK, ATTN_WIDTH)
    kext = kext.reshape(b, nblk, MOBA_BLOCK, LANES)
    kmean = kstat[:, :, 0].reshape(b, nblk, ATTN_WIDTH)
    kabs = kstat[:, :, 1].reshape(b, nblk, ATTN_WIDTH)
    srows, svec = _moba_tables(tq)
    return pl.pallas_call(
        _moba_kernel,
        grid=(b, N_PAIRS, nblk // Q_BLOCKS_PER_STEP),
        in_specs=[
            pl.BlockSpec((1, 1, Q_BLOCKS_PER_STEP, LANES, tq), lambda bb, p, j: (bb, p, j, 0, 0)),
            pl.BlockSpec((1, nblk, MOBA_BLOCK, LANES), lambda bb, p, j: (bb, 0, 0, p)),
            pl.BlockSpec((1, nblk, MOBA_BLOCK, LANES), lambda bb, p, j: (bb, 0, 0, 0)),
            pl.BlockSpec((1, 1, nblk, HEAD_PAIR * V_ROWS, MOBA_BLOCK),
                         lambda bb, p, j: (bb, p, 0, 0, 0)),
            pl.BlockSpec((1, nblk, LANES), lambda bb, p, j: (bb, 0, p)),
            pl.BlockSpec((1, nblk, LANES), lambda bb, p, j: (bb, 0, p)),
            pl.BlockSpec((1, HEAD_PAIR, AUG - LANES - SEL_COLS, tq), lambda bb, p, j: (p, 0, 0, 0)),
            pl.BlockSpec((1, HEAD_PAIR, 1, tq), lambda bb, p, j: (p, 0, 0, 0)),
        ],
        out_specs=pl.BlockSpec((1, Q_BLOCKS_PER_STEP * tq, LANES), lambda bb, p, j: (bb, j, p)),
        out_shape=jax.ShapeDtypeStruct((b, s, ATTN_WIDTH), BF16),
        scratch_shapes=[pltpu.VMEM((Q_BLOCKS_PER_STEP, HEAD_PAIR, AUG, tq), BF16)],
        compiler_params=_params("parallel", "parallel", "arbitrary"),
        name="moba_attention",
    )(qT, k, kext, vT, kmean, kabs, srows, svec)


def _swa_kernel(qT_ref, kp_ref, kc_ref, vTp_ref, vTc_ref, bias_ref, sink_ref, o_ref, yT_ref):
    j = pl.program_id(1)
    tq = qT_ref.shape[-1]
    kwin = jnp.concatenate([kp_ref[0], kc_ref[0]], axis=0)
    vTwin = jnp.concatenate([vTp_ref[0], vTc_ref[0]], axis=1)
    zeros = jnp.zeros((HEAD_DIM, WINDOW), BF16)
    group = N_HEADS // N_KV_HEADS_B
    units = [(u, g) for u in range(tq // WINDOW) for g in range(N_KV_HEADS_B)]

    def queries(u):
        return slice(u * WINDOW, (u + 1) * WINDOW)

    def keys(u):
        return slice(u * WINDOW, (u + 2) * WINDOW)

    def padded_q(h, u):
        q_h = qT_ref[0, h * HEAD_DIM:(h + 1) * HEAD_DIM, queries(u)]
        return jnp.concatenate([q_h, zeros] if h // group == 0 else [zeros, q_h], axis=0)

    def bias(u, g):
        return bias_ref[jnp.minimum(j, 1), g] if u == 0 else bias_ref[1, g]

    heads = {g: range(g * group, (g + 1) * group) for g in range(N_KV_HEADS_B)}
    sink = {g: jnp.concatenate([sink_ref[h][:, :WINDOW] * LOG2E for h in heads[g]], axis=1)
            for g in range(N_KV_HEADS_B)}
    s = {(u, g): jnp.dot(kwin[keys(u)], jnp.concatenate([padded_q(h, u) for h in heads[g]], axis=1),
                         preferred_element_type=F32) for u, g in units}
    p = {(u, g): jnp.exp2(s[u, g] + bias(u, g) - sink[g]).astype(BF16) for u, g in units}
    acc = {(u, g): jnp.dot(vTwin[g * V_ROWS:(g + 1) * V_ROWS, keys(u)], p[u, g],
                           preferred_element_type=F32) for u, g in units}
    worst = jnp.zeros((1, group * WINDOW), F32)
    for u, g in units:
        denom = acc[u, g][HEAD_DIM:HEAD_DIM + 1] + 1.0
        worst = jnp.maximum(worst, denom)
        out = acc[u, g][:HEAD_DIM] * (1.0 / denom)
        for i, h in enumerate(heads[g]):
            yT_ref[h * HEAD_DIM:(h + 1) * HEAD_DIM, queries(u)] = out[:, i * WINDOW:(i + 1) * WINDOW]
    o_ref[0] = yT_ref[...].T.astype(o_ref.dtype)
    in_range = jnp.max(worst) < OVERFLOW_GUARD

    @pl.when(jnp.logical_not(in_range))
    def _():
        for u, g in units:
            for i, h in enumerate(heads[g]):
                s_h = (jnp.dot(kwin[keys(u)], padded_q(h, u), preferred_element_type=F32)
                       + bias(u, g)[:, i * WINDOW:(i + 1) * WINDOW])
                sink_h = sink_ref[h][:, :WINDOW] * LOG2E
                m = jnp.maximum(jnp.max(s_h, axis=0, keepdims=True), sink_h)
                acc_h = jnp.dot(vTwin[g * V_ROWS:(g + 1) * V_ROWS, keys(u)],
                                jnp.exp2(s_h - m).astype(BF16), preferred_element_type=F32)
                denom = acc_h[HEAD_DIM:HEAD_DIM + 1] + jnp.exp2(sink_h - m)
                yT_ref[h * HEAD_DIM:(h + 1) * HEAD_DIM, queries(u)] = acc_h[:HEAD_DIM] * (1.0 / denom)
        o_ref[0] = yT_ref[...].T.astype(o_ref.dtype)


def _swa_bias_table():
    tq = WINDOW
    slopes = _alibi_slopes(N_HEADS).astype(np.float64) * LOG2E
    kw = np.arange(WINDOW + tq)[:, None]
    q = np.arange(tq)[None, :]
    dist = q + WINDOW - kw
    ok = (dist >= 0) & (dist < WINDOW)
    first_tile = ok & (kw >= WINDOW)
    table = np.stack([np.where(valid[None], -slopes[:, None, None] * dist[None], MASKED)
                      for valid in (first_tile, ok)])
    table = table.reshape(2, N_KV_HEADS_B, N_HEADS // N_KV_HEADS_B, WINDOW + tq, tq)
    table = np.moveaxis(table, 2, 3).reshape(2, N_KV_HEADS_B, WINDOW + tq, -1)
    return jnp.asarray(table, F32)


def _swa_attention(qT, k, vT, sinks):
    b, _, s = qT.shape
    tq = min(SWA_TILE, s)
    assert tq % WINDOW == 0
    r = tq // WINDOW
    bias = _swa_bias_table()
    sink_rows = jnp.broadcast_to(sinks.astype(F32)[:, None, None], (N_HEADS, 1, tq))
    prev = lambda j: jnp.maximum(j * r - 1, 0)
    return pl.pallas_call(
        _swa_kernel,
        grid=(b, s // tq),
        in_specs=[
            pl.BlockSpec((1, ATTN_WIDTH, tq), lambda bb, j: (bb, 0, j)),
            pl.BlockSpec((1, WINDOW, KV_WIDTH_B), lambda bb, j: (bb, prev(j), 0)),
            pl.BlockSpec((1, tq, KV_WIDTH_B), lambda bb, j: (bb, j, 0)),
            pl.BlockSpec((1, N_KV_HEADS_B * V_ROWS, WINDOW), lambda bb, j: (bb, 0, prev(j))),
            pl.BlockSpec((1, N_KV_HEADS_B * V_ROWS, tq), lambda bb, j: (bb, 0, j)),
            pl.BlockSpec(bias.shape, lambda bb, j: (0, 0, 0, 0)),
            pl.BlockSpec(sink_rows.shape, lambda bb, j: (0, 0, 0)),
        ],
        out_specs=pl.BlockSpec((1, tq, ATTN_WIDTH), lambda bb, j: (bb, j, 0)),
        out_shape=jax.ShapeDtypeStruct((b, s, ATTN_WIDTH), BF16),
        scratch_shapes=[pltpu.VMEM((ATTN_WIDTH, tq), F32)],
        compiler_params=_params("parallel", "arbitrary"),
        name="swa_attention",
    )(qT, k, k, vT, vT, bias, sink_rows)


def kernel(x, mem, norm_g, w_in_a, w_in_b, sinks_b, w_mem_kv, w_out, mem_norm_g, final_norm_g):
    depth = norm_g.shape[0]
    b, s, _ = x.shape
    assert s % ROW_TILE == 0 or s < ROW_TILE
    mem_k, mem_vT = _mem_kv(mem, mem_norm_g, w_mem_kv)
    mixers = ["moba" if i % 2 == 0 else "swa" for i in range(depth)]
    w_in = [w_in_a[i // 2] if i % 2 == 0 else w_in_b[i // 2] for i in range(depth)]
    proj = _first_projection(x, norm_g[0], w_in[0], mixers[0])
    for i in range(depth):
        if mixers[i] == "moba":
            qT, k, kext, vT, kstat, qmT, z = proj
            y_self = _moba_attention(qT, k, kext, vT, kstat)
        else:
            qT, k, vT, qmT, z = proj
            y_self = _swa_attention(qT, k, vT, sinks_b[i // 2])
        if i + 1 < depth:
            x, *proj = _epilogue(y_self, qmT, z, x, mem_k, mem_vT, i, w_out[i], norm_g[i + 1],
                                 w_in[i + 1], mixers[i + 1])
        else:
            (x,) = _epilogue(y_self, qmT, z, x, mem_k, mem_vT, i, w_out[i], final_norm_g)
    return x
```

```python
import functools
import math

import jax
import jax.numpy as jnp
import numpy as np
from jax import lax
from jax.experimental import pallas as pl
from jax.experimental.pallas import tpu as pltpu

HEAD_DIM = 64
N_HEADS = 12
N_KV_HEADS_B = 2
N_MEM_HEADS = 4
ATTN_WIDTH = N_HEADS * HEAD_DIM
MEM_WIDTH = N_MEM_HEADS * HEAD_DIM
BRANCH_WIDTH = ATTN_WIDTH + MEM_WIDTH
KV_WIDTH_B = N_KV_HEADS_B * HEAD_DIM
MOBA_BLOCK = 256
MOBA_TOPK = 3
WINDOW = 128
RMS_EPS = 1e-6

LANES = 128
HEAD_PAIR = LANES // HEAD_DIM
N_PAIRS = N_HEADS // HEAD_PAIR
SEL_COLS = 64
ALIBI_COLS = 4
ONES_ROWS = 16
V_ROWS = HEAD_DIM + ONES_ROWS
LOG2E = math.log2(math.e)
AUG = 2 * LANES
MASKED = -(2.0 ** 30)
OVERFLOW_GUARD = 2.0 ** 100
EXP2_IS_ZERO_BELOW = -135.0
BOUND_SLACK = 1.0
BF16_ROUND_UP = 1.0 + 2.0 ** -7
TRIP_SIZES = (16, 4, 2, 1)
MASKED_TRIP = 4
Q_BLOCKS_PER_STEP = 2
NEAR_BLOCKS = 2
ROW_TILE = 512
SWA_TILE = 256
VMEM_LIMIT = 56 * 1024 * 1024

F32 = jnp.float32
BF16 = jnp.bfloat16
_NT = (((1,), (1,)), ((), ()))


def _alibi_slopes(n_heads):
    def pow2_slopes(n):
        start = 2.0 ** (-8.0 / n)
        return [start ** (i + 1) for i in range(n)]
    if math.log2(n_heads).is_integer():
        vals = pow2_slopes(n_heads)
    else:
        c = 2 ** math.floor(math.log2(n_heads))
        vals = pow2_slopes(c) + pow2_slopes(2 * c)[0::2][: n_heads - c]
    return np.array(vals, dtype=np.float32)


def _bf16_pieces(v):
    rest = np.asarray(v, np.float64)
    pieces = []
    for _ in range(ALIBI_COLS):
        p = rest.astype(BF16).astype(np.float64)
        pieces.append(p.astype(np.float32))
        rest = rest - p
    assert np.all(np.abs(rest) <= np.abs(v) * 2.0 ** -30)
    return pieces


def _rms_bf16(x, g):
    ms = jnp.mean(x * x, axis=-1, keepdims=True)
    return (x * lax.rsqrt(ms + RMS_EPS) * g).astype(BF16)


def _params(*sem):
    return pltpu.CompilerParams(dimension_semantics=sem, vmem_limit_bytes=VMEM_LIMIT)


def _mem_kv_kernel(mem_ref, g_ref, wk_ref, wvT_ref, k_ref, vT_ref):
    h = _rms_bf16(mem_ref[0], g_ref[...])
    k = jnp.dot(h, wk_ref[0], preferred_element_type=F32)
    vT = lax.dot_general(wvT_ref[0], h, _NT, preferred_element_type=F32)
    lane = lax.broadcasted_iota(jnp.int32, k.shape, 1)
    for hh in range(N_MEM_HEADS):
        mine = (lane >= hh * HEAD_DIM) & (lane < (hh + 1) * HEAD_DIM)
        k_ref[0, 0, hh] = jnp.where(mine, k, 0.0).astype(BF16)
        vT_ref[0, 0, hh * V_ROWS:hh * V_ROWS + HEAD_DIM] = (
            vT[hh * HEAD_DIM:(hh + 1) * HEAD_DIM].astype(BF16))
        vT_ref[0, 0, hh * V_ROWS + HEAD_DIM:(hh + 1) * V_ROWS] = jnp.ones(
            (ONES_ROWS, vT.shape[1]), BF16)


def _mem_kv(mem, mem_norm_g, w_mem_kv):
    b, m, d = mem.shape
    depth = w_mem_kv.shape[0]
    wk = w_mem_kv[:, :, :MEM_WIDTH].astype(BF16)
    wvT = jnp.swapaxes(w_mem_kv[:, :, MEM_WIDTH:], 1, 2).astype(BF16)
    return pl.pallas_call(
        _mem_kv_kernel,
        grid=(depth, b),
        in_specs=[
            pl.BlockSpec((1, m, d), lambda i, bb: (bb, 0, 0)),
            pl.BlockSpec((1, d), lambda i, bb: (0, 0)),
            pl.BlockSpec((1, d, MEM_WIDTH), lambda i, bb: (i, 0, 0)),
            pl.BlockSpec((1, MEM_WIDTH, d), lambda i, bb: (i, 0, 0)),
        ],
        out_specs=[
            pl.BlockSpec((1, 1, N_MEM_HEADS, m, MEM_WIDTH), lambda i, bb: (i, bb, 0, 0, 0)),
            pl.BlockSpec((1, 1, N_MEM_HEADS * V_ROWS, m), lambda i, bb: (i, bb, 0, 0)),
        ],
        out_shape=[
            jax.ShapeDtypeStruct((depth, b, N_MEM_HEADS, m, MEM_WIDTH), BF16),
            jax.ShapeDtypeStruct((depth, b, N_MEM_HEADS * V_ROWS, m), BF16),
        ],
        compiler_params=_params("arbitrary", "arbitrary"),
        name="mem_kv",
    )(mem, mem_norm_g.reshape(1, d), wk, wvT)


def _moba_projection(h, wT_ref, w_ref, qT_ref, k_ref, kext_ref, vT_ref, kstat_ref, qmT_ref, z_ref):
    tm = h.shape[0]
    nb = tm // MOBA_BLOCK
    outT = lax.dot_general(wT_ref[...], h, _NT, preferred_element_type=F32)
    out = jnp.dot(h, w_ref[...], preferred_element_type=F32)
    ones = jnp.ones((ONES_ROWS, MOBA_BLOCK), BF16)
    for p in range(N_PAIRS):
        for bb in range(nb):
            cols = slice(bb * MOBA_BLOCK, (bb + 1) * MOBA_BLOCK)
            qT_ref[0, p, bb] = outT[p * LANES:(p + 1) * LANES, cols].astype(BF16)
            for hh in range(HEAD_PAIR):
                v0 = ATTN_WIDTH + (p * HEAD_PAIR + hh) * HEAD_DIM
                vT_ref[0, p, bb, hh * V_ROWS:hh * V_ROWS + HEAD_DIM] = (
                    outT[v0:v0 + HEAD_DIM, cols].astype(BF16))
                vT_ref[0, p, bb, hh * V_ROWS + HEAD_DIM:(hh + 1) * V_ROWS] = ones
    qmT_ref[0] = outT[2 * ATTN_WIDTH:, :].astype(BF16)
    k = out[:, :ATTN_WIDTH]
    z_ref[0] = out[:, ATTN_WIDTH:].astype(BF16)
    row = lax.broadcasted_iota(jnp.int32, (tm, LANES), 0)
    lane = lax.broadcasted_iota(jnp.int32, (tm, LANES), 1)
    blk = pl.program_id(1) * nb + row // MOBA_BLOCK
    s_rel = (row % MOBA_BLOCK).astype(F32)
    ext = jnp.where(lane < SEL_COLS, (lane == blk).astype(F32),
                    jnp.where(lane < SEL_COLS + ALIBI_COLS, s_rel, 0.0)).astype(BF16)
    k_bf16 = k.astype(BF16)
    k_ref[0] = k_bf16
    kext_ref[0] = ext
    k_abs = jnp.abs(k_bf16.astype(F32))
    for bb in range(nb):
        rows = slice(bb * MOBA_BLOCK, (bb + 1) * MOBA_BLOCK)
        kstat_ref[0, 0, 0, bb:bb + 1, :] = jnp.mean(k[rows], axis=0, keepdims=True)
        kstat_ref[0, 0, 1, bb:bb + 1, :] = jnp.max(k_abs[rows], axis=0, keepdims=True)


def _moba_projection_specs(b, s, tm):
    nblk = s // MOBA_BLOCK
    nb = tm // MOBA_BLOCK
    specs = [
        pl.BlockSpec((1, N_PAIRS, nb, LANES, MOBA_BLOCK), lambda bb, i: (bb, 0, i, 0, 0)),
        pl.BlockSpec((1, tm, ATTN_WIDTH), lambda bb, i: (bb, i, 0)),
        pl.BlockSpec((1, tm, LANES), lambda bb, i: (bb, i, 0)),
        pl.BlockSpec((1, N_PAIRS, nb, HEAD_PAIR * V_ROWS, MOBA_BLOCK),
                     lambda bb, i: (bb, 0, i, 0, 0)),
        pl.BlockSpec((1, 1, 2, nb, ATTN_WIDTH), lambda bb, i: (bb, i, 0, 0, 0)),
        pl.BlockSpec((1, MEM_WIDTH, tm), lambda bb, i: (bb, 0, i)),
        pl.BlockSpec((1, tm, BRANCH_WIDTH), lambda bb, i: (bb, i, 0)),
    ]
    shapes = [
        jax.ShapeDtypeStruct((b, N_PAIRS, nblk, LANES, MOBA_BLOCK), BF16),
        jax.ShapeDtypeStruct((b, s, ATTN_WIDTH), BF16),
        jax.ShapeDtypeStruct((b, s, LANES), BF16),
        jax.ShapeDtypeStruct((b, N_PAIRS, nblk, HEAD_PAIR * V_ROWS, MOBA_BLOCK), BF16),
        jax.ShapeDtypeStruct((b, s // tm, 2, nb, ATTN_WIDTH), F32),
        jax.ShapeDtypeStruct((b, MEM_WIDTH, s), BF16),
        jax.ShapeDtypeStruct((b, s, BRANCH_WIDTH), BF16),
    ]
    return specs, shapes


def _swa_projection(h, wT_ref, w_ref, qT_ref, k_ref, vT_ref, qmT_ref, z_ref):
    outT = lax.dot_general(wT_ref[...], h, _NT, preferred_element_type=F32)
    out = jnp.dot(h, w_ref[...], preferred_element_type=F32)
    tm = h.shape[0]
    qT_ref[0] = outT[:ATTN_WIDTH].astype(BF16)
    for g in range(N_KV_HEADS_B):
        v0 = ATTN_WIDTH + g * HEAD_DIM
        vT_ref[0, g * V_ROWS:g * V_ROWS + HEAD_DIM] = outT[v0:v0 + HEAD_DIM].astype(BF16)
        vT_ref[0, g * V_ROWS + HEAD_DIM:(g + 1) * V_ROWS] = jnp.ones((ONES_ROWS, tm), BF16)
    qmT_ref[0] = outT[ATTN_WIDTH + KV_WIDTH_B:].astype(BF16)
    k_ref[0] = out[:, :KV_WIDTH_B].astype(BF16)
    z_ref[0] = out[:, KV_WIDTH_B:].astype(BF16)


def _swa_projection_specs(b, s, tm):
    specs = [
        pl.BlockSpec((1, ATTN_WIDTH, tm), lambda bb, i: (bb, 0, i)),
        pl.BlockSpec((1, tm, KV_WIDTH_B), lambda bb, i: (bb, i, 0)),
        pl.BlockSpec((1, N_KV_HEADS_B * V_ROWS, tm), lambda bb, i: (bb, 0, i)),
        pl.BlockSpec((1, MEM_WIDTH, tm), lambda bb, i: (bb, 0, i)),
        pl.BlockSpec((1, tm, BRANCH_WIDTH), lambda bb, i: (bb, i, 0)),
    ]
    shapes = [
        jax.ShapeDtypeStruct((b, ATTN_WIDTH, s), BF16),
        jax.ShapeDtypeStruct((b, s, KV_WIDTH_B), BF16),
        jax.ShapeDtypeStruct((b, N_KV_HEADS_B * V_ROWS, s), BF16),
        jax.ShapeDtypeStruct((b, MEM_WIDTH, s), BF16),
        jax.ShapeDtypeStruct((b, s, BRANCH_WIDTH), BF16),
    ]
    return specs, shapes


_PROJECTIONS = {
    "moba": (_moba_projection, _moba_projection_specs),
    "swa": (_swa_projection, _swa_projection_specs),
}


def _split_in_weights(w_in, mixer):
    scale = HEAD_DIM ** -0.5 * LOG2E
    kvw = ATTN_WIDTH if mixer == "moba" else KV_WIDTH_B
    o = np.cumsum([0, ATTN_WIDTH, kvw, kvw, MEM_WIDTH, BRANCH_WIDTH])
    q, k, v, qm, z = (w_in[:, o[i]:o[i + 1]] for i in range(5))
    wT = jnp.concatenate([q * scale, v, qm * scale], axis=1).T.astype(BF16)
    w = jnp.concatenate([k, z], axis=1).astype(BF16)
    return wT, w


def _first_projection_kernel(x_ref, g_ref, wT_ref, w_ref, *out_refs, mixer):
    h = _rms_bf16(x_ref[0], g_ref[...])
    _PROJECTIONS[mixer][0](h, wT_ref, w_ref, *out_refs)


def _first_projection(x, g, w_in, mixer):
    b, s, d = x.shape
    tm = min(ROW_TILE, s)
    wT, w = _split_in_weights(w_in, mixer)
    specs, shapes = _PROJECTIONS[mixer][1](b, s, tm)
    return pl.pallas_call(
        functools.partial(_first_projection_kernel, mixer=mixer),
        grid=(b, s // tm),
        in_specs=[
            pl.BlockSpec((1, tm, d), lambda bb, i: (bb, i, 0)),
            pl.BlockSpec((1, d), lambda bb, i: (0, 0)),
            pl.BlockSpec(wT.shape, lambda bb, i: (0, 0)),
            pl.BlockSpec(w.shape, lambda bb, i: (0, 0)),
        ],
        out_specs=specs,
        out_shape=shapes,
        compiler_params=_params("parallel", "arbitrary"),
        name=f"in_proj_{mixer}",
    )(x, g.reshape(1, d), wT, w)


def _memory_attention_T(qmT, mk_ref, mvT_ref):
    sT = [jnp.dot(mk_ref[0, hh], qmT, preferred_element_type=F32)
          for hh in range(N_MEM_HEADS)]
    outs = []
    for hh in range(N_MEM_HEADS):
        p = jnp.exp2(sT[hh] - jnp.max(sT[hh], axis=0, keepdims=True))
        acc = jnp.dot(mvT_ref[0, hh * V_ROWS:(hh + 1) * V_ROWS, :], p.astype(BF16),
                      preferred_element_type=F32)
        outs.append(acc[:HEAD_DIM] * (1.0 / acc[HEAD_DIM:HEAD_DIM + 1]))
    return jnp.concatenate(outs, axis=0)


def _epilogue_kernel(y_ref, qmT_ref, z_ref, x_ref, mk_ref, mvT_ref, wout_ref, g_ref, *rest, mixer):
    ymem = _memory_attention_T(qmT_ref[0], mk_ref, mvT_ref).T
    y = jnp.concatenate([y_ref[0].astype(F32), ymem], axis=-1)
    z = z_ref[0].astype(F32)
    gated = (y * (z / (1.0 + jnp.exp(-z)))).astype(BF16)
    xn = x_ref[0] + jnp.dot(gated, wout_ref[...], preferred_element_type=F32)
    if mixer is None:
        (o_ref,) = rest
        ms = jnp.mean(xn * xn, axis=-1, keepdims=True)
        o_ref[0] = xn * lax.rsqrt(ms + RMS_EPS) * g_ref[...]
    else:
        wT_ref, w_ref, xo_ref = rest[:3]
        xo_ref[0] = xn
        _PROJECTIONS[mixer][0](_rms_bf16(xn, g_ref[...]), wT_ref, w_ref, *rest[3:])


def _epilogue(y_self, qmT, z, x, mem_k, mem_vT, layer, w_out, g, w_in_next=None, mixer=None):
    b, s, d = x.shape
    tm = min(ROW_TILE, s)
    m = mem_k.shape[3]
    in_specs = [
        pl.BlockSpec((1, tm, ATTN_WIDTH), lambda bb, i: (bb, i, 0)),
        pl.BlockSpec((1, MEM_WIDTH, tm), lambda bb, i: (bb, 0, i)),
        pl.BlockSpec((1, tm, BRANCH_WIDTH), lambda bb, i: (bb, i, 0)),
        pl.BlockSpec((1, tm, d), lambda bb, i: (bb, i, 0)),
        pl.BlockSpec((1, N_MEM_HEADS, m, MEM_WIDTH), lambda bb, i: (bb, 0, 0, 0)),
        pl.BlockSpec((1, N_MEM_HEADS * V_ROWS, m), lambda bb, i: (bb, 0, 0)),
        pl.BlockSpec((BRANCH_WIDTH, d), lambda bb, i: (0, 0)),
        pl.BlockSpec((1, d), lambda bb, i: (0, 0)),
    ]
    args = [y_self, qmT, z, x, mem_k[layer], mem_vT[layer], w_out.astype(BF16), g.reshape(1, d)]
    x_spec = pl.BlockSpec((1, tm, d), lambda bb, i: (bb, i, 0))
    x_shape = jax.ShapeDtypeStruct((b, s, d), F32)
    if mixer is None:
        out_specs, out_shape = [x_spec], [x_shape]
    else:
        wT, w = _split_in_weights(w_in_next, mixer)
        in_specs += [pl.BlockSpec(wT.shape, lambda bb, i: (0, 0)),
                     pl.BlockSpec(w.shape, lambda bb, i: (0, 0))]
        args += [wT, w]
        specs, shapes = _PROJECTIONS[mixer][1](b, s, tm)
        out_specs, out_shape = [x_spec] + specs, [x_shape] + shapes
    return pl.pallas_call(
        functools.partial(_epilogue_kernel, mixer=mixer),
        grid=(b, s // tm),
        in_specs=in_specs,
        out_specs=out_specs,
        out_shape=out_shape,
        compiler_params=_params("parallel", "arbitrary"),
        name=f"epilogue_{mixer}",
    )(*args)


def _moba_kernel(qT_ref, k_ref, kext_ref, vT_ref, kmean_ref, kabs_ref, srows_ref, slope_ref, o_ref,
                 qaug_ref):
    nblk = kmean_ref.shape[1]
    tq = qT_ref.shape[-1]
    kmean = kmean_ref[0]
    drow = lax.broadcasted_iota(jnp.int32, (LANES, tq), 0)
    klane = lax.broadcasted_iota(jnp.int32, kmean.shape, 1)
    blk = lax.broadcasted_iota(jnp.int32, (nblk, tq), 0)
    blk_f32 = blk.astype(F32)
    blk_col = lax.broadcasted_iota(jnp.int32, (nblk, 1), 0)
    krow = lax.broadcasted_iota(jnp.int32, (MOBA_BLOCK, tq), 0)
    qlane = lax.broadcasted_iota(jnp.int32, (MOBA_BLOCK, tq), 1)
    causal = krow <= qlane
    head_lanes = [(klane >= hh * HEAD_DIM) & (klane < (hh + 1) * HEAD_DIM) for hh in range(HEAD_PAIR)]
    head_rows = [(drow >= hh * HEAD_DIM) & (drow < (hh + 1) * HEAD_DIM) for hh in range(HEAD_PAIR)]
    no_choice = jnp.zeros((SEL_COLS, tq), BF16)

    def k_aug(n):
        return jnp.concatenate([k_ref[0, n], kext_ref[0, n]], axis=1)

    def weighted_values(n, hh, p):
        return jnp.dot(vT_ref[0, 0, n, hh * V_ROWS:(hh + 1) * V_ROWS, :], p.astype(BF16),
                       preferred_element_type=F32)

    chains = [(t, hh) for t in range(Q_BLOCKS_PER_STEP) for hh in range(HEAD_PAIR)]
    block_j = [pl.program_id(2) * Q_BLOCKS_PER_STEP + t for t in range(Q_BLOCKS_PER_STEP)]
    last_j = block_j[-1]
    qT = [qT_ref[0, 0, t] for t in range(Q_BLOCKS_PER_STEP)]
    q_abs = [jnp.abs(q) for q in qT]

    q_head = {(t, hh): jnp.where(head_rows[hh], qT[t], jnp.zeros_like(qT[t])) for t, hh in chains}
    kmean_h = [jnp.where(head_lanes[hh], kmean, 0.0).astype(BF16) for hh in range(HEAD_PAIR)]
    gates = {(t, hh): jnp.dot(kmean_h[hh], qT[t], preferred_element_type=F32)
             for t, hh in chains}
    own_sT = {(t, hh): jnp.dot(
        k_aug(block_j[t]), jnp.concatenate([q_head[t, hh], no_choice, srows_ref[0, hh]], axis=0),
        preferred_element_type=F32) for t, hh in chains}
    kabs_h = [jnp.where(head_lanes[hh], kabs_ref[0] * BF16_ROUND_UP, 0.0).astype(BF16)
              for hh in range(HEAD_PAIR)]
    qk_bound = {(t, hh): jnp.dot(kabs_h[hh], q_abs[t], preferred_element_type=F32)
                for t, hh in chains}

    own_max, own_acc = {}, {}
    for t, hh in chains:
        sT = jnp.where(causal, own_sT[t, hh], MASKED)
        own_max[t, hh] = jnp.max(sT, axis=0, keepdims=True)
        own_acc[t, hh] = weighted_values(block_j[t], hh, jnp.exp2(sT - own_max[t, hh]))

    first_needed = last_j
    for t in range(Q_BLOCKS_PER_STEP):
        j = block_j[t]
        needed = blk_col < 0
        for hh in range(HEAD_PAIR):
            qk_over_shift = jnp.max(qk_bound[t, hh] - own_max[t, hh], axis=1, keepdims=True)
            alibi = slope_ref[0, hh][:, :1] * ((MOBA_BLOCK - 1) - MOBA_BLOCK * (j - blk_col)).astype(F32)
            needed = needed | (qk_over_shift + alibi + BOUND_SLACK >= EXP2_IS_ZERO_BELOW)
        first_needed = jnp.minimum(first_needed, jnp.min(
            jnp.where(needed & (blk_col < j - NEAR_BLOCKS), blk_col, last_j).astype(F32)
        ).astype(jnp.int32))

    gate = {c: jnp.where(blk < block_j[c[0]], gates[c], -jnp.inf) for c in chains}
    bias = {c: jnp.full((nblk, tq), MASKED, F32) for c in chains}
    for k in range(MOBA_TOPK):
        for c in chains:
            best = jnp.max(gate[c], axis=0, keepdims=True)
            first = jnp.min(jnp.where(gate[c] == best, blk_f32, float(nblk)), axis=0, keepdims=True)
            pick = blk_f32 == jnp.where(best > -jnp.inf, first, float(nblk))
            bias[c] = jnp.where(pick, 0.0, bias[c])
            if k + 1 < MOBA_TOPK:
                gate[c] = jnp.where(pick, -jnp.inf, gate[c])
    for t, hh in chains:
        rows = bias[t, hh].astype(BF16)
        if nblk < SEL_COLS:
            rows = jnp.concatenate([rows, jnp.zeros((SEL_COLS - nblk, tq), BF16)], axis=0)
        qaug_ref[t, hh] = jnp.concatenate([q_head[t, hh], rows, srows_ref[0, hh]], axis=0)

    def scores(t, n, hh):
        return jnp.dot(k_aug(n), qaug_ref[t, hh], preferred_element_type=F32)

    def block_offset(t, n, hh):
        return slope_ref[0, hh] * ((n - block_j[t]) * MOBA_BLOCK).astype(F32)

    def finish(t, accs):
        oT = jnp.concatenate(
            [acc[:HEAD_DIM] * (1.0 / acc[HEAD_DIM:HEAD_DIM + 1]) for acc in accs], axis=0)
        o_ref[0, t * tq:(t + 1) * tq, :] = oT.T.astype(o_ref.dtype)

    def fixed_shift_blocks(first, count, accs):
        slots = [first + u for u in range(count)]
        blocks = [jnp.minimum(n, last_j) for n in slots]
        sT = [{c: scores(c[0], n, c[1]) for c in chains} for n in blocks]
        accs = list(accs)
        for u, n in enumerate(blocks):
            for i, (t, hh) in enumerate(chains):
                shift = jnp.where(slots[u] < block_j[t] - NEAR_BLOCKS,
                                  own_max[t, hh] - block_offset(t, n, hh), jnp.inf)
                accs[i] += weighted_values(n, hh, jnp.exp2(sT[u][t, hh] - shift))
        return accs

    near = [(t, hh, back) for back in range(1, NEAR_BLOCKS + 1) for t, hh in chains]
    near_sT = {c: scores(c[0], jnp.maximum(block_j[c[0]] - c[2], 0), c[1]) for c in near}
    fast = {c: own_acc[c] for c in chains}
    for t, hh, back in near:
        n = jnp.maximum(block_j[t] - back, 0)
        shift = jnp.where(block_j[t] >= back, own_max[t, hh] - block_offset(t, n, hh), jnp.inf)
        fast[t, hh] = fast[t, hh] + weighted_values(n, hh, jnp.exp2(near_sT[t, hh, back] - shift))
    fast = [fast[c] for c in chains]

    loop_end = last_j - NEAR_BLOCKS

    def far_blocks(accs):
        start = first_needed
        for unroll in TRIP_SIZES:
            left = loop_end - start
            trips = jnp.maximum((left + 1) // unroll if unroll == MASKED_TRIP else left // unroll, 0)
            accs = lax.fori_loop(
                0, trips,
                lambda g, a, start=start, unroll=unroll: fixed_shift_blocks(start + g * unroll, unroll, a),
                accs)
            start = start + trips * unroll
        return accs

    fast = lax.cond(first_needed < loop_end, far_blocks, lambda accs: accs, fast)

    in_range = []
    for t in range(Q_BLOCKS_PER_STEP):
        mine = fast[t * HEAD_PAIR:(t + 1) * HEAD_PAIR]
        finish(t, mine)
        denominators = functools.reduce(jnp.maximum, [a[HEAD_DIM:HEAD_DIM + 1] for a in mine])
        in_range.append(jnp.max(denominators) < OVERFLOW_GUARD)

    for t in range(Q_BLOCKS_PER_STEP):
        @pl.when(jnp.logical_not(in_range[t]))
        def _(t=t):
            def running_max_block(n, carry):
                out = []
                for hh in range(HEAD_PAIR):
                    m, acc = carry[2 * hh:2 * hh + 2]
                    sT = scores(t, n, hh)
                    c = block_offset(t, n, hh)
                    m_new = jnp.maximum(m, jnp.max(sT, axis=0, keepdims=True) + c)
                    acc = (jnp.exp2(m - m_new) * acc
                           + weighted_values(n, hh, jnp.exp2(sT - (m_new - c))))
                    out += [m_new, acc]
                return out

            slow = lax.fori_loop(
                0, block_j[t], running_max_block,
                [x for hh in range(HEAD_PAIR) for x in (own_max[t, hh], own_acc[t, hh])])
            finish(t, slow[1::2])


def _moba_tables(tq):
    slopes = _alibi_slopes(N_HEADS).astype(np.float64) * LOG2E
    pieces = _bf16_pieces(slopes)
    rows = np.zeros((N_HEADS, AUG - LANES - SEL_COLS, tq), np.float32)
    for c, piece in enumerate(pieces):
        rows[:, c, :] = piece[:, None]
    srows = jnp.asarray(rows.reshape(N_PAIRS, HEAD_PAIR, AUG - LANES - SEL_COLS, tq), BF16)
    svec = jnp.asarray(np.broadcast_to(slopes[:, None, None], (N_HEADS, 1, tq))
                       .reshape(N_PAIRS, HEAD_PAIR, 1, tq), F32)
    return srows, svec


def _moba_attention(qT, k, kext, vT, kstat):
    b, _, nblk, _, tq = qT.shape
    assert nblk <= SEL_COLS and tq == MOBA_BLOCK and nblk % Q_BLOCKS_PER_STEP == 0
    s = nblk * MOBA_BLOCK
    k = k.reshape(b, nblk, MOBA_BLOCK, ATTN_WIDTH)
    kext = kext.reshape(b, nblk, MOBA_BLOCK, LANES)
    kmean = kstat[:, :, 0].reshape(b, nblk, ATTN_WIDTH)
    kabs = kstat[:, :, 1].reshape(b, nblk, ATTN_WIDTH)
    srows, svec = _moba_tables(tq)
    return pl.pallas_call(
        _moba_kernel,
        grid=(b, N_PAIRS, nblk // Q_BLOCKS_PER_STEP),
        in_specs=[
            pl.BlockSpec((1, 1, Q_BLOCKS_PER_STEP, LANES, tq), lambda bb, p, j: (bb, p, j, 0, 0)),
            pl.BlockSpec((1, nblk, MOBA_BLOCK, LANES), lambda bb, p, j: (bb, 0, 0, p)),
            pl.BlockSpec((1, nblk, MOBA_BLOCK, LANES), lambda bb, p, j: (bb, 0, 0, 0)),
            pl.BlockSpec((1, 1, nblk, HEAD_PAIR * V_ROWS, MOBA_BLOCK),
                         lambda bb, p, j: (bb, p, 0, 0, 0)),
            pl.BlockSpec((1, nblk, LANES), lambda bb, p, j: (bb, 0, p)),
            pl.BlockSpec((1, nblk, LANES), lambda bb, p, j: (bb, 0, p)),
            pl.BlockSpec((1, HEAD_PAIR, AUG - LANES - SEL_COLS, tq), lambda bb, p, j: (p, 0, 0, 0)),
            pl.BlockSpec((1, HEAD_PAIR, 1, tq), lambda bb, p, j: (p, 0, 0, 0)),
        ],
        out_specs=pl.BlockSpec((1, Q_BLOCKS_PER_STEP * tq, LANES), lambda bb, p, j: (bb, j, p)),
        out_shape=jax.ShapeDtypeStruct((b, s, ATTN_WIDTH), BF16),
        scratch_shapes=[pltpu.VMEM((Q_BLOCKS_PER_STEP, HEAD_PAIR, AUG, tq), BF16)],
        compiler_params=_params("parallel", "parallel", "arbitrary"),
        name="moba_attention",
    )(qT, k, kext, vT, kmean, kabs, srows, svec)


def _swa_kernel(qT_ref, kp_ref, kc_ref, vTp_ref, vTc_ref, bias_ref, sink_ref, o_ref, yT_ref):
    j = pl.program_id(1)
    tq = qT_ref.shape[-1]
    kwin = jnp.concatenate([kp_ref[0], kc_ref[0]], axis=0)
    vTwin = jnp.concatenate([vTp_ref[0], vTc_ref[0]], axis=1)
    zeros = jnp.zeros((HEAD_DIM, WINDOW), BF16)
    group = N_HEADS // N_KV_HEADS_B
    units = [(u, g) for u in range(tq // WINDOW) for g in range(N_KV_HEADS_B)]

    def queries(u):
        return slice(u * WINDOW, (u + 1) * WINDOW)

    def keys(u):
        return slice(u * WINDOW, (u + 2) * WINDOW)

    def padded_q(h, u):
        q_h = qT_ref[0, h * HEAD_DIM:(h + 1) * HEAD_DIM, queries(u)]
        return jnp.concatenate([q_h, zeros] if h // group == 0 else [zeros, q_h], axis=0)

    def bias(u, g):
        return bias_ref[jnp.minimum(j, 1), g] if u == 0 else bias_ref[1, g]

    heads = {g: range(g * group, (g + 1) * group) for g in range(N_KV_HEADS_B)}
    sink = {g: jnp.concatenate([sink_ref[h][:, :WINDOW] * LOG2E for h in heads[g]], axis=1)
            for g in range(N_KV_HEADS_B)}
    s = {(u, g): jnp.dot(kwin[keys(u)], jnp.concatenate([padded_q(h, u) for h in heads[g]], axis=1),
                         preferred_element_type=F32) for u, g in units}
    p = {(u, g): jnp.exp2(s[u, g] + bias(u, g) - sink[g]).astype(BF16) for u, g in units}
    acc = {(u, g): jnp.dot(vTwin[g * V_ROWS:(g + 1) * V_ROWS, keys(u)], p[u, g],
                           preferred_element_type=F32) for u, g in units}
    worst = jnp.zeros((1, group * WINDOW), F32)
    for u, g in units:
        denom = acc[u, g][HEAD_DIM:HEAD_DIM + 1] + 1.0
        worst = jnp.maximum(worst, denom)
        out = acc[u, g][:HEAD_DIM] * (1.0 / denom)
        for i, h in enumerate(heads[g]):
            yT_ref[h * HEAD_DIM:(h + 1) * HEAD_DIM, queries(u)] = out[:, i * WINDOW:(i + 1) * WINDOW]
    o_ref[0] = yT_ref[...].T.astype(o_ref.dtype)
    in_range = jnp.max(worst) < OVERFLOW_GUARD

    @pl.when(jnp.logical_not(in_range))
    def _():
        for u, g in units:
            for i, h in enumerate(heads[g]):
                s_h = (jnp.dot(kwin[keys(u)], padded_q(h, u), preferred_element_type=F32)
                       + bias(u, g)[:, i * WINDOW:(i + 1) * WINDOW])
                sink_h = sink_ref[h][:, :WINDOW] * LOG2E
                m = jnp.maximum(jnp.max(s_h, axis=0, keepdims=True), sink_h)
                acc_h = jnp.dot(vTwin[g * V_ROWS:(g + 1) * V_ROWS, keys(u)],
                                jnp.exp2(s_h - m).astype(BF16), preferred_element_type=F32)
                denom = acc_h[HEAD_DIM:HEAD_DIM + 1] + jnp.exp2(sink_h - m)
                yT_ref[h * HEAD_DIM:(h + 1) * HEAD_DIM, queries(u)] = acc_h[:HEAD_DIM] * (1.0 / denom)
        o_ref[0] = yT_ref[...].T.astype(o_ref.dtype)


def _swa_bias_table():
    tq = WINDOW
    slopes = _alibi_slopes(N_HEADS).astype(np.float64) * LOG2E
    kw = np.arange(WINDOW + tq)[:, None]
    q = np.arange(tq)[None, :]
    dist = q + WINDOW - kw
    ok = (dist >= 0) & (dist < WINDOW)
    first_tile = ok & (kw >= WINDOW)
    table = np.stack([np.where(valid[None], -slopes[:, None, None] * dist[None], MASKED)
                      for valid in (first_tile, ok)])
    table = table.reshape(2, N_KV_HEADS_B, N_HEADS // N_KV_HEADS_B, WINDOW + tq, tq)
    table = np.moveaxis(table, 2, 3).reshape(2, N_KV_HEADS_B, WINDOW + tq, -1)
    return jnp.asarray(table, F32)


def _swa_attention(qT, k, vT, sinks):
    b, _, s = qT.shape
    tq = min(SWA_TILE, s)
    assert tq % WINDOW == 0
    r = tq // WINDOW
    bias = _swa_bias_table()
    sink_rows = jnp.broadcast_to(sinks.astype(F32)[:, None, None], (N_HEADS, 1, tq))
    prev = lambda j: jnp.maximum(j * r - 1, 0)
    return pl.pallas_call(
        _swa_kernel,
        grid=(b, s // tq),
        in_specs=[
            pl.BlockSpec((1, ATTN_WIDTH, tq), lambda bb, j: (bb, 0, j)),
            pl.BlockSpec((1, WINDOW, KV_WIDTH_B), lambda bb, j: (bb, prev(j), 0)),
            pl.BlockSpec((1, tq, KV_WIDTH_B), lambda bb, j: (bb, j, 0)),
            pl.BlockSpec((1, N_KV_HEADS_B * V_ROWS, WINDOW), lambda bb, j: (bb, 0, prev(j))),
            pl.BlockSpec((1, N_KV_HEADS_B * V_ROWS, tq), lambda bb, j: (bb, 0, j)),
            pl.BlockSpec(bias.shape, lambda bb, j: (0, 0, 0, 0)),
            pl.BlockSpec(sink_rows.shape, lambda bb, j: (0, 0, 0)),
        ],
        out_specs=pl.BlockSpec((1, tq, ATTN_WIDTH), lambda bb, j: (bb, j, 0)),
        out_shape=jax.ShapeDtypeStruct((b, s, ATTN_WIDTH), BF16),
        scratch_shapes=[pltpu.VMEM((ATTN_WIDTH, tq), F32)],
        compiler_params=_params("parallel", "arbitrary"),
        name="swa_attention",
    )(qT, k, k, vT, vT, bias, sink_rows)


def kernel(x, mem, norm_g, w_in_a, w_in_b, sinks_b, w_mem_kv, w_out, mem_norm_g, final_norm_g):
    depth = norm_g.shape[0]
    b, s, _ = x.shape
    assert s % ROW_TILE == 0 or s < ROW_TILE
    mem_k, mem_vT = _mem_kv(mem, mem_norm_g, w_mem_kv)
    mixers = ["moba" if i % 2 == 0 else "swa" for i in range(depth)]
    w_in = [w_in_a[i // 2] if i % 2 == 0 else w_in_b[i // 2] for i in range(depth)]
    proj = _first_projection(x, norm_g[0], w_in[0], mixers[0])
    for i in range(depth):
        if mixers[i] == "moba":
            qT, k, kext, vT, kstat, qmT, z = proj
            y_self = _moba_attention(qT, k, kext, vT, kstat)
        else:
            qT, k, vT, qmT, z = proj
            y_self = _swa_attention(qT, k, vT, sinks_b[i // 2])
        if i + 1 < depth:
            x, *proj = _epilogue(y_self, qmT, z, x, mem_k, mem_vT, i, w_out[i], norm_g[i + 1],
                                 w_in[i + 1], mixers[i + 1])
        else:
            (x,) = _epilogue(y_self, qmT, z, x, mem_k, mem_vT, i, w_out[i], final_norm_g)
    return x
```

```python
import functools
import math

import jax
import jax.numpy as jnp
import numpy as np
from jax import lax
from jax.experimental import pallas as pl
from jax.experimental.pallas import tpu as pltpu

HEAD_DIM = 64
N_HEADS = 12
N_KV_HEADS_B = 2
N_MEM_HEADS = 4
ATTN_WIDTH = N_HEADS * HEAD_DIM
MEM_WIDTH = N_MEM_HEADS * HEAD_DIM
BRANCH_WIDTH = ATTN_WIDTH + MEM_WIDTH
KV_WIDTH_B = N_KV_HEADS_B * HEAD_DIM
MOBA_BLOCK = 256
MOBA_TOPK = 3
WINDOW = 128
RMS_EPS = 1e-6

LANES = 128
HEAD_PAIR = LANES // HEAD_DIM
N_PAIRS = N_HEADS // HEAD_PAIR
SEL_COLS = 64
ALIBI_COLS = 4
ONES_ROWS = 16
V_ROWS = HEAD_DIM + ONES_ROWS
LOG2E = math.log2(math.e)
AUG = 2 * LANES
MASKED = -(2.0 ** 30)
OVERFLOW_GUARD = 2.0 ** 100
EXP2_IS_ZERO_BELOW = -135.0
BOUND_SLACK = 1.0
BF16_ROUND_UP = 1.0 + 2.0 ** -7
TRIP_SIZES = (16, 4, 2, 1)
MASKED_TRIP = 4
Q_BLOCKS_PER_STEP = 2
NEAR_BLOCKS = 2
ROW_TILE = 512
SWA_TILE = 256
VMEM_LIMIT = 56 * 1024 * 1024

F32 = jnp.float32
BF16 = jnp.bfloat16
_NT = (((1,), (1,)), ((), ()))


def _alibi_slopes(n_heads):
    def pow2_slopes(n):
        start = 2.0 ** (-8.0 / n)
        return [start ** (i + 1) for i in range(n)]
    if math.log2(n_heads).is_integer():
        vals = pow2_slopes(n_heads)
    else:
        c = 2 ** math.floor(math.log2(n_heads))
        vals = pow2_slopes(c) + pow2_slopes(2 * c)[0::2][: n_heads - c]
    return np.array(vals, dtype=np.float32)


def _bf16_pieces(v):
    rest = np.asarray(v, np.float64)
    pieces = []
    for _ in range(ALIBI_COLS):
        p = rest.astype(BF16).astype(np.float64)
        pieces.append(p.astype(np.float32))
        rest = rest - p
    assert np.all(np.abs(rest) <= np.abs(v) * 2.0 ** -30)
    return pieces


def _rms_bf16(x, g):
    ms = jnp.mean(x * x, axis=-1, keepdims=True)
    return (x * lax.rsqrt(ms + RMS_EPS) * g).astype(BF16)


def _params(*sem):
    return pltpu.CompilerParams(dimension_semantics=sem, vmem_limit_bytes=VMEM_LIMIT)


def _mem_kv_kernel(mem_ref, g_ref, wk_ref, wvT_ref, k_ref, vT_ref):
    h = _rms_bf16(mem_ref[0], g_ref[...])
    k = jnp.dot(h, wk_ref[0], preferred_element_type=F32)
    vT = lax.dot_general(wvT_ref[0], h, _NT, preferred_element_type=F32)
    lane = lax.broadcasted_iota(jnp.int32, k.shape, 1)
    for hh in range(N_MEM_HEADS):
        mine = (lane >= hh * HEAD_DIM) & (lane < (hh + 1) * HEAD_DIM)
        k_ref[0, 0, hh] = jnp.where(mine, k, 0.0).astype(BF16)
        vT_ref[0, 0, hh * V_ROWS:hh * V_ROWS + HEAD_DIM] = (
            vT[hh * HEAD_DIM:(hh + 1) * HEAD_DIM].astype(BF16))
        vT_ref[0, 0, hh * V_ROWS + HEAD_DIM:(hh + 1) * V_ROWS] = jnp.ones(
            (ONES_ROWS, vT.shape[1]), BF16)


def _mem_kv(mem, mem_norm_g, w_mem_kv):
    b, m, d = mem.shape
    depth = w_mem_kv.shape[0]
    wk = w_mem_kv[:, :, :MEM_WIDTH].astype(BF16)
    wvT = jnp.swapaxes(w_mem_kv[:, :, MEM_WIDTH:], 1, 2).astype(BF16)
    return pl.pallas_call(
        _mem_kv_kernel,
        grid=(depth, b),
        in_specs=[
            pl.BlockSpec((1, m, d), lambda i, bb: (bb, 0, 0)),
            pl.BlockSpec((1, d), lambda i, bb: (0, 0)),
            pl.BlockSpec((1, d, MEM_WIDTH), lambda i, bb: (i, 0, 0)),
            pl.BlockSpec((1, MEM_WIDTH, d), lambda i, bb: (i, 0, 0)),
        ],
        out_specs=[
            pl.BlockSpec((1, 1, N_MEM_HEADS, m, MEM_WIDTH), lambda i, bb: (i, bb, 0, 0, 0)),
            pl.BlockSpec((1, 1, N_MEM_HEADS * V_ROWS, m), lambda i, bb: (i, bb, 0, 0)),
        ],
        out_shape=[
            jax.ShapeDtypeStruct((depth, b, N_MEM_HEADS, m, MEM_WIDTH), BF16),
            jax.ShapeDtypeStruct((depth, b, N_MEM_HEADS * V_ROWS, m), BF16),
        ],
        compiler_params=_params("arbitrary", "arbitrary"),
        name="mem_kv",
    )(mem, mem_norm_g.reshape(1, d), wk, wvT)


def _moba_projection(h, wT_ref, w_ref, qT_ref, k_ref, kext_ref, vT_ref, kstat_ref, qmT_ref, z_ref):
    tm = h.shape[0]
    nb = tm // MOBA_BLOCK
    outT = lax.dot_general(wT_ref[...], h, _NT, preferred_element_type=F32)
    out = jnp.dot(h, w_ref[...], preferred_element_type=F32)
    ones = jnp.ones((ONES_ROWS, MOBA_BLOCK), BF16)
    for p in range(N_PAIRS):
        for bb in range(nb):
            cols = slice(bb * MOBA_BLOCK, (bb + 1) * MOBA_BLOCK)
            qT_ref[0, p, bb] = outT[p * LANES:(p + 1) * LANES, cols].astype(BF16)
            for hh in range(HEAD_PAIR):
                v0 = ATTN_WIDTH + (p * HEAD_PAIR + hh) * HEAD_DIM
                vT_ref[0, p, bb, hh * V_ROWS:hh * V_ROWS + HEAD_DIM] = (
                    outT[v0:v0 + HEAD_DIM, cols].astype(BF16))
                vT_ref[0, p, bb, hh * V_ROWS + HEAD_DIM:(hh + 1) * V_ROWS] = ones
    qmT_ref[0] = outT[2 * ATTN_WIDTH:, :].astype(BF16)
    k = out[:, :ATTN_WIDTH]
    z_ref[0] = out[:, ATTN_WIDTH:].astype(BF16)
    row = lax.broadcasted_iota(jnp.int32, (tm, LANES), 0)
    lane = lax.broadcasted_iota(jnp.int32, (tm, LANES), 1)
    blk = pl.program_id(1) * nb + row // MOBA_BLOCK
    s_rel = (row % MOBA_BLOCK).astype(F32)
    ext = jnp.where(lane < SEL_COLS, (lane == blk).astype(F32),
                    jnp.where(lane < SEL_COLS + ALIBI_COLS, s_rel, 0.0)).astype(BF16)
    k_bf16 = k.astype(BF16)
    k_ref[0] = k_bf16
    kext_ref[0] = ext
    k_abs = jnp.abs(k_bf16.astype(F32))
    for bb in range(nb):
        rows = slice(bb * MOBA_BLOCK, (bb + 1) * MOBA_BLOCK)
        kstat_ref[0, 0, 0, bb:bb + 1, :] = jnp.mean(k[rows], axis=0, keepdims=True)
        kstat_ref[0, 0, 1, bb:bb + 1, :] = jnp.max(k_abs[rows], axis=0, keepdims=True)


def _moba_projection_specs(b, s, tm):
    nblk = s // MOBA_BLOCK
    nb = tm // MOBA_BLOCK
    specs = [
        pl.BlockSpec((1, N_PAIRS, nb, LANES, MOBA_BLOCK), lambda bb, i: (bb, 0, i, 0, 0)),
        pl.BlockSpec((1, tm, ATTN_WIDTH), lambda bb, i: (bb, i, 0)),
        pl.BlockSpec((1, tm, LANES), lambda bb, i: (bb, i, 0)),
        pl.BlockSpec((1, N_PAIRS, nb, HEAD_PAIR * V_ROWS, MOBA_BLOCK),
                     lambda bb, i: (bb, 0, i, 0, 0)),
        pl.BlockSpec((1, 1, 2, nb, ATTN_WIDTH), lambda bb, i: (bb, i, 0, 0, 0)),
        pl.BlockSpec((1, MEM_WIDTH, tm), lambda bb, i: (bb, 0, i)),
        pl.BlockSpec((1, tm, BRANCH_WIDTH), lambda bb, i: (bb, i, 0)),
    ]
    shapes = [
        jax.ShapeDtypeStruct((b, N_PAIRS, nblk, LANES, MOBA_BLOCK), BF16),
        jax.ShapeDtypeStruct((b, s, ATTN_WIDTH), BF16),
        jax.ShapeDtypeStruct((b, s, LANES), BF16),
        jax.ShapeDtypeStruct((b, N_PAIRS, nblk, HEAD_PAIR * V_ROWS, MOBA_BLOCK), BF16),
        jax.ShapeDtypeStruct((b, s // tm, 2, nb, ATTN_WIDTH), F32),
        jax.ShapeDtypeStruct((b, MEM_WIDTH, s), BF16),
        jax.ShapeDtypeStruct((b, s, BRANCH_WIDTH), BF16),
    ]
    return specs, shapes


def _swa_projection(h, wT_ref, w_ref, qT_ref, k_ref, vT_ref, qmT_ref, z_ref):
    outT = lax.dot_general(wT_ref[...], h, _NT, preferred_element_type=F32)
    out = jnp.dot(h, w_ref[...], preferred_element_type=F32)
    tm = h.shape[0]
    qT_ref[0] = outT[:ATTN_WIDTH].astype(BF16)
    for g in range(N_KV_HEADS_B):
        v0 = ATTN_WIDTH + g * HEAD_DIM
        vT_ref[0, g * V_ROWS:g * V_ROWS + HEAD_DIM] = outT[v0:v0 + HEAD_DIM].astype(BF16)
        vT_ref[0, g * V_ROWS + HEAD_DIM:(g + 1) * V_ROWS] = jnp.ones((ONES_ROWS, tm), BF16)
    qmT_ref[0] = outT[ATTN_WIDTH + KV_WIDTH_B:].astype(BF16)
    k_ref[0] = out[:, :KV_WIDTH_B].astype(BF16)
    z_ref[0] = out[:, KV_WIDTH_B:].astype(BF16)


def _swa_projection_specs(b, s, tm):
    specs = [
        pl.BlockSpec((1, ATTN_WIDTH, tm), lambda bb, i: (bb, 0, i)),
        pl.BlockSpec((1, tm, KV_WIDTH_B), lambda bb, i: (bb, i, 0)),
        pl.BlockSpec((1, N_KV_HEADS_B * V_ROWS, tm), lambda bb, i: (bb, 0, i)),
        pl.BlockSpec((1, MEM_WIDTH, tm), lambda bb, i: (bb, 0, i)),
        pl.BlockSpec((1, tm, BRANCH_WIDTH), lambda bb, i: (bb, i, 0)),
    ]
    shapes = [
        jax.ShapeDtypeStruct((b, ATTN_WIDTH, s), BF16),
        jax.ShapeDtypeStruct((b, s, KV_WIDTH_B), BF16),
        jax.ShapeDtypeStruct((b, N_KV_HEADS_B * V_ROWS, s), BF16),
        jax.ShapeDtypeStruct((b, MEM_WIDTH, s), BF16),
        jax.ShapeDtypeStruct((b, s, BRANCH_WIDTH), BF16),
    ]
    return specs, shapes


_PROJECTIONS = {
    "moba": (_moba_projection, _moba_projection_specs),
    "swa": (_swa_projection, _swa_projection_specs),
}


def _moba_head_order():
    return np.argsort(-_alibi_slopes(N_HEADS), kind="stable")


def _permute_heads(w, axis):
    shape = w.shape
    grouped = w.reshape(shape[:axis] + (N_HEADS, HEAD_DIM) + shape[axis + 1:])
    return jnp.take(grouped, _moba_head_order(), axis=axis).reshape(shape)


def _split_in_weights(w_in, mixer):
    scale = HEAD_DIM ** -0.5 * LOG2E
    kvw = ATTN_WIDTH if mixer == "moba" else KV_WIDTH_B
    o = np.cumsum([0, ATTN_WIDTH, kvw, kvw, MEM_WIDTH, BRANCH_WIDTH])
    q, k, v, qm, z = (w_in[:, o[i]:o[i + 1]] for i in range(5))
    if mixer == "moba":
        q, k, v = (_permute_heads(t, axis=1) for t in (q, k, v))
        z = jnp.concatenate([_permute_heads(z[:, :ATTN_WIDTH], axis=1), z[:, ATTN_WIDTH:]], axis=1)
    wT = jnp.concatenate([q * scale, v, qm * scale], axis=1).T.astype(BF16)
    w = jnp.concatenate([k, z], axis=1).astype(BF16)
    return wT, w


def _first_projection_kernel(x_ref, g_ref, wT_ref, w_ref, *out_refs, mixer):
    h = _rms_bf16(x_ref[0], g_ref[...])
    _PROJECTIONS[mixer][0](h, wT_ref, w_ref, *out_refs)


def _first_projection(x, g, w_in, mixer):
    b, s, d = x.shape
    tm = min(ROW_TILE, s)
    wT, w = _split_in_weights(w_in, mixer)
    specs, shapes = _PROJECTIONS[mixer][1](b, s, tm)
    return pl.pallas_call(
        functools.partial(_first_projection_kernel, mixer=mixer),
        grid=(b, s // tm),
        in_specs=[
            pl.BlockSpec((1, tm, d), lambda bb, i: (bb, i, 0)),
            pl.BlockSpec((1, d), lambda bb, i: (0, 0)),
            pl.BlockSpec(wT.shape, lambda bb, i: (0, 0)),
            pl.BlockSpec(w.shape, lambda bb, i: (0, 0)),
        ],
        out_specs=specs,
        out_shape=shapes,
        compiler_params=_params("parallel", "arbitrary"),
        name=f"in_proj_{mixer}",
    )(x, g.reshape(1, d), wT, w)


def _memory_attention_T(qmT, mk_ref, mvT_ref):
    sT = [jnp.dot(mk_ref[0, hh], qmT, preferred_element_type=F32)
          for hh in range(N_MEM_HEADS)]
    outs = []
    for hh in range(N_MEM_HEADS):
        p = jnp.exp2(sT[hh] - jnp.max(sT[hh], axis=0, keepdims=True))
        acc = jnp.dot(mvT_ref[0, hh * V_ROWS:(hh + 1) * V_ROWS, :], p.astype(BF16),
                      preferred_element_type=F32)
        outs.append(acc[:HEAD_DIM] * (1.0 / acc[HEAD_DIM:HEAD_DIM + 1]))
    return jnp.concatenate(outs, axis=0)


def _epilogue_kernel(y_ref, qmT_ref, z_ref, x_ref, mk_ref, mvT_ref, wout_ref, g_ref, *rest, mixer):
    ymem = _memory_attention_T(qmT_ref[0], mk_ref, mvT_ref).T
    y = jnp.concatenate([y_ref[0].astype(F32), ymem], axis=-1)
    z = z_ref[0].astype(F32)
    gated = (y * (z / (1.0 + jnp.exp(-z)))).astype(BF16)
    xn = x_ref[0] + jnp.dot(gated, wout_ref[...], preferred_element_type=F32)
    if mixer is None:
        (o_ref,) = rest
        ms = jnp.mean(xn * xn, axis=-1, keepdims=True)
        o_ref[0] = xn * lax.rsqrt(ms + RMS_EPS) * g_ref[...]
    else:
        wT_ref, w_ref, xo_ref = rest[:3]
        xo_ref[0] = xn
        _PROJECTIONS[mixer][0](_rms_bf16(xn, g_ref[...]), wT_ref, w_ref, *rest[3:])


def _epilogue(y_self, qmT, z, x, mem_k, mem_vT, layer, w_out, g, w_in_next=None, mixer=None):
    b, s, d = x.shape
    tm = min(ROW_TILE, s)
    m = mem_k.shape[3]
    in_specs = [
        pl.BlockSpec((1, tm, ATTN_WIDTH), lambda bb, i: (bb, i, 0)),
        pl.BlockSpec((1, MEM_WIDTH, tm), lambda bb, i: (bb, 0, i)),
        pl.BlockSpec((1, tm, BRANCH_WIDTH), lambda bb, i: (bb, i, 0)),
        pl.BlockSpec((1, tm, d), lambda bb, i: (bb, i, 0)),
        pl.BlockSpec((1, N_MEM_HEADS, m, MEM_WIDTH), lambda bb, i: (bb, 0, 0, 0)),
        pl.BlockSpec((1, N_MEM_HEADS * V_ROWS, m), lambda bb, i: (bb, 0, 0)),
        pl.BlockSpec((BRANCH_WIDTH, d), lambda bb, i: (0, 0)),
        pl.BlockSpec((1, d), lambda bb, i: (0, 0)),
    ]
    args = [y_self, qmT, z, x, mem_k[layer], mem_vT[layer], w_out.astype(BF16), g.reshape(1, d)]
    x_spec = pl.BlockSpec((1, tm, d), lambda bb, i: (bb, i, 0))
    x_shape = jax.ShapeDtypeStruct((b, s, d), F32)
    if mixer is None:
        out_specs, out_shape = [x_spec], [x_shape]
    else:
        wT, w = _split_in_weights(w_in_next, mixer)
        in_specs += [pl.BlockSpec(wT.shape, lambda bb, i: (0, 0)),
                     pl.BlockSpec(w.shape, lambda bb, i: (0, 0))]
        args += [wT, w]
        specs, shapes = _PROJECTIONS[mixer][1](b, s, tm)
        out_specs, out_shape = [x_spec] + specs, [x_shape] + shapes
    return pl.pallas_call(
        functools.partial(_epilogue_kernel, mixer=mixer),
        grid=(b, s // tm),
        in_specs=in_specs,
        out_specs=out_specs,
        out_shape=out_shape,
        compiler_params=_params("parallel", "arbitrary"),
        name=f"epilogue_{mixer}",
    )(*args)


def _moba_kernel(qT_ref, k_ref, kext_ref, vT_ref, kmean_ref, kabs_ref, srows_ref, slope_ref, o_ref,
                 qaug_ref):
    nblk = kmean_ref.shape[1]
    tq = qT_ref.shape[-1]
    kmean = kmean_ref[0]
    drow = lax.broadcasted_iota(jnp.int32, (LANES, tq), 0)
    klane = lax.broadcasted_iota(jnp.int32, kmean.shape, 1)
    blk = lax.broadcasted_iota(jnp.int32, (nblk, tq), 0)
    blk_f32 = blk.astype(F32)
    blk_col = lax.broadcasted_iota(jnp.int32, (nblk, 1), 0)
    krow = lax.broadcasted_iota(jnp.int32, (MOBA_BLOCK, tq), 0)
    qlane = lax.broadcasted_iota(jnp.int32, (MOBA_BLOCK, tq), 1)
    causal = krow <= qlane
    head_lanes = [(klane >= hh * HEAD_DIM) & (klane < (hh + 1) * HEAD_DIM) for hh in range(HEAD_PAIR)]
    head_rows = [(drow >= hh * HEAD_DIM) & (drow < (hh + 1) * HEAD_DIM) for hh in range(HEAD_PAIR)]
    no_choice = jnp.zeros((SEL_COLS, tq), BF16)

    def k_aug(n):
        return jnp.concatenate([k_ref[0, n], kext_ref[0, n]], axis=1)

    def weighted_values(n, hh, p):
        return jnp.dot(vT_ref[0, 0, n, hh * V_ROWS:(hh + 1) * V_ROWS, :], p.astype(BF16),
                       preferred_element_type=F32)

    chains = [(t, hh) for t in range(Q_BLOCKS_PER_STEP) for hh in range(HEAD_PAIR)]
    block_j = [pl.program_id(2) * Q_BLOCKS_PER_STEP + t for t in range(Q_BLOCKS_PER_STEP)]
    last_j = block_j[-1]
    qT = [qT_ref[0, 0, t] for t in range(Q_BLOCKS_PER_STEP)]
    q_abs = [jnp.abs(q) for q in qT]

    q_head = {(t, hh): jnp.where(head_rows[hh], qT[t], jnp.zeros_like(qT[t])) for t, hh in chains}
    kmean_h = [jnp.where(head_lanes[hh], kmean, 0.0).astype(BF16) for hh in range(HEAD_PAIR)]
    gates = {(t, hh): jnp.dot(kmean_h[hh], qT[t], preferred_element_type=F32)
             for t, hh in chains}
    own_sT = {(t, hh): jnp.dot(
        k_aug(block_j[t]), jnp.concatenate([q_head[t, hh], no_choice, srows_ref[0, hh]], axis=0),
        preferred_element_type=F32) for t, hh in chains}
    kabs_h = [jnp.where(head_lanes[hh], kabs_ref[0] * BF16_ROUND_UP, 0.0).astype(BF16)
              for hh in range(HEAD_PAIR)]
    qk_bound = {(t, hh): jnp.dot(kabs_h[hh], q_abs[t], preferred_element_type=F32)
                for t, hh in chains}

    own_max, own_acc = {}, {}
    for t, hh in chains:
        sT = jnp.where(causal, own_sT[t, hh], MASKED)
        own_max[t, hh] = jnp.max(sT, axis=0, keepdims=True)
        own_acc[t, hh] = weighted_values(block_j[t], hh, jnp.exp2(sT - own_max[t, hh]))

    first_needed = last_j
    for t in range(Q_BLOCKS_PER_STEP):
        j = block_j[t]
        needed = blk_col < 0
        for hh in range(HEAD_PAIR):
            qk_over_shift = jnp.max(qk_bound[t, hh] - own_max[t, hh], axis=1, keepdims=True)
            alibi = slope_ref[0, hh][:, :1] * ((MOBA_BLOCK - 1) - MOBA_BLOCK * (j - blk_col)).astype(F32)
            needed = needed | (qk_over_shift + alibi + BOUND_SLACK >= EXP2_IS_ZERO_BELOW)
        first_needed = jnp.minimum(first_needed, jnp.min(
            jnp.where(needed & (blk_col < j - NEAR_BLOCKS), blk_col, last_j).astype(F32)
        ).astype(jnp.int32))

    gate = {c: jnp.where(blk < block_j[c[0]], gates[c], -jnp.inf) for c in chains}
    bias = {c: jnp.full((nblk, tq), MASKED, F32) for c in chains}
    for k in range(MOBA_TOPK):
        for c in chains:
            best = jnp.max(gate[c], axis=0, keepdims=True)
            first = jnp.min(jnp.where(gate[c] == best, blk_f32, float(nblk)), axis=0, keepdims=True)
            pick = blk_f32 == jnp.where(best > -jnp.inf, first, float(nblk))
            bias[c] = jnp.where(pick, 0.0, bias[c])
            if k + 1 < MOBA_TOPK:
                gate[c] = jnp.where(pick, -jnp.inf, gate[c])
    for t, hh in chains:
        rows = bias[t, hh].astype(BF16)
        if nblk < SEL_COLS:
            rows = jnp.concatenate([rows, jnp.zeros((SEL_COLS - nblk, tq), BF16)], axis=0)
        qaug_ref[t, hh] = jnp.concatenate([q_head[t, hh], rows, srows_ref[0, hh]], axis=0)

    def scores(t, n, hh):
        return jnp.dot(k_aug(n), qaug_ref[t, hh], preferred_element_type=F32)

    def block_offset(t, n, hh):
        return slope_ref[0, hh] * ((n - block_j[t]) * MOBA_BLOCK).astype(F32)

    def finish(t, accs):
        oT = jnp.concatenate(
            [acc[:HEAD_DIM] * (1.0 / acc[HEAD_DIM:HEAD_DIM + 1]) for acc in accs], axis=0)
        o_ref[0, t * tq:(t + 1) * tq, :] = oT.T.astype(o_ref.dtype)

    def fixed_shift_blocks(first, count, accs):
        slots = [first + u for u in range(count)]
        blocks = [jnp.minimum(n, last_j) for n in slots]
        sT = [{c: scores(c[0], n, c[1]) for c in chains} for n in blocks]
        accs = list(accs)
        for u, n in enumerate(blocks):
            for i, (t, hh) in enumerate(chains):
                shift = jnp.where(slots[u] < block_j[t] - NEAR_BLOCKS,
                                  own_max[t, hh] - block_offset(t, n, hh), jnp.inf)
                accs[i] += weighted_values(n, hh, jnp.exp2(sT[u][t, hh] - shift))
        return accs

    near = [(t, hh, back) for back in range(1, NEAR_BLOCKS + 1) for t, hh in chains]
    near_sT = {c: scores(c[0], jnp.maximum(block_j[c[0]] - c[2], 0), c[1]) for c in near}
    fast = {c: own_acc[c] for c in chains}
    for t, hh, back in near:
        n = jnp.maximum(block_j[t] - back, 0)
        shift = jnp.where(block_j[t] >= back, own_max[t, hh] - block_offset(t, n, hh), jnp.inf)
        fast[t, hh] = fast[t, hh] + weighted_values(n, hh, jnp.exp2(near_sT[t, hh, back] - shift))
    fast = [fast[c] for c in chains]

    start = first_needed
    loop_end = last_j - NEAR_BLOCKS
    for unroll in TRIP_SIZES:
        left = loop_end - start
        trips = jnp.maximum((left + 1) // unroll if unroll == MASKED_TRIP else left // unroll, 0)
        fast = lax.fori_loop(
            0, trips,
            lambda g, a, start=start, unroll=unroll: fixed_shift_blocks(start + g * unroll, unroll, a),
            fast)
        start = start + trips * unroll

    in_range = []
    for t in range(Q_BLOCKS_PER_STEP):
        mine = fast[t * HEAD_PAIR:(t + 1) * HEAD_PAIR]
        finish(t, mine)
        denominators = functools.reduce(jnp.maximum, [a[HEAD_DIM:HEAD_DIM + 1] for a in mine])
        in_range.append(jnp.max(denominators) < OVERFLOW_GUARD)

    for t in range(Q_BLOCKS_PER_STEP):
        @pl.when(jnp.logical_not(in_range[t]))
        def _(t=t):
            def running_max_block(n, carry):
                out = []
                for hh in range(HEAD_PAIR):
                    m, acc = carry[2 * hh:2 * hh + 2]
                    sT = scores(t, n, hh)
                    c = block_offset(t, n, hh)
                    m_new = jnp.maximum(m, jnp.max(sT, axis=0, keepdims=True) + c)
                    acc = (jnp.exp2(m - m_new) * acc
                           + weighted_values(n, hh, jnp.exp2(sT - (m_new - c))))
                    out += [m_new, acc]
                return out

            slow = lax.fori_loop(
                0, block_j[t], running_max_block,
                [x for hh in range(HEAD_PAIR) for x in (own_max[t, hh], own_acc[t, hh])])
            finish(t, slow[1::2])


def _moba_tables(tq):
    slopes = _alibi_slopes(N_HEADS)[_moba_head_order()].astype(np.float64) * LOG2E
    pieces = _bf16_pieces(slopes)
    rows = np.zeros((N_HEADS, AUG - LANES - SEL_COLS, tq), np.float32)
    for c, piece in enumerate(pieces):
        rows[:, c, :] = piece[:, None]
    srows = jnp.asarray(rows.reshape(N_PAIRS, HEAD_PAIR, AUG - LANES - SEL_COLS, tq), BF16)
    svec = jnp.asarray(np.broadcast_to(slopes[:, None, None], (N_HEADS, 1, tq))
                       .reshape(N_PAIRS, HEAD_PAIR, 1, tq), F32)
    return srows, svec


def _moba_attention(qT, k, kext, vT, kstat):
    b, _, nblk, _, tq = qT.shape
    assert nblk <= SEL_COLS and tq == MOBA_BLOCK and nblk % Q_BLOCKS_PER_STEP == 0
    s = nblk * MOBA_BLOCK
    k = k.reshape(b, nblk, MOBA_BLOCK, ATTN_WIDTH)
    kext = kext.reshape(b, nblk, MOBA_BLOCK, LANES)
    kmean = kstat[:, :, 0].reshape(b, nblk, ATTN_WIDTH)
    kabs = kstat[:, :, 1].reshape(b, nblk, ATTN_WIDTH)
    srows, svec = _moba_tables(tq)
    return pl.pallas_call(
        _moba_kernel,
        grid=(b, N_PAIRS, nblk // Q_BLOCKS_PER_STEP),
        in_specs=[
            pl.BlockSpec((1, 1, Q_BLOCKS_PER_STEP, LANES, tq), lambda bb, p, j: (bb, p, j, 0, 0)),
            pl.BlockSpec((1, nblk, MOBA_BLOCK, LANES), lambda bb, p, j: (bb, 0, 0, p)),
            pl.BlockSpec((1, nblk, MOBA_BLOCK, LANES), lambda bb, p, j: (bb, 0, 0, 0)),
            pl.BlockSpec((1, 1, nblk, HEAD_PAIR * V_ROWS, MOBA_BLOCK),
                         lambda bb, p, j: (bb, p, 0, 0, 0)),
            pl.BlockSpec((1, nblk, LANES), lambda bb, p, j: (bb, 0, p)),
            pl.BlockSpec((1, nblk, LANES), lambda bb, p, j: (bb, 0, p)),
            pl.BlockSpec((1, HEAD_PAIR, AUG - LANES - SEL_COLS, tq), lambda bb, p, j: (p, 0, 0, 0)),
            pl.BlockSpec((1, HEAD_PAIR, 1, tq), lambda bb, p, j: (p, 0, 0, 0)),
        ],
        out_specs=pl.BlockSpec((1, Q_BLOCKS_PER_STEP * tq, LANES), lambda bb, p, j: (bb, j, p)),
        out_shape=jax.ShapeDtypeStruct((b, s, ATTN_WIDTH), BF16),
        scratch_shapes=[pltpu.VMEM((Q_BLOCKS_PER_STEP, HEAD_PAIR, AUG, tq), BF16)],
        compiler_params=_params("parallel", "parallel", "arbitrary"),
        name="moba_attention",
    )(qT, k, kext, vT, kmean, kabs, srows, svec)


def _swa_kernel(qT_ref, kp_ref, kc_ref, vTp_ref, vTc_ref, bias_ref, sink_ref, o_ref, yT_ref):
    j = pl.program_id(1)
    tq = qT_ref.shape[-1]
    kwin = jnp.concatenate([kp_ref[0], kc_ref[0]], axis=0)
    vTwin = jnp.concatenate([vTp_ref[0], vTc_ref[0]], axis=1)
    zeros = jnp.zeros((HEAD_DIM, WINDOW), BF16)
    group = N_HEADS // N_KV_HEADS_B
    units = [(u, g) for u in range(tq // WINDOW) for g in range(N_KV_HEADS_B)]

    def queries(u):
        return slice(u * WINDOW, (u + 1) * WINDOW)

    def keys(u):
        return slice(u * WINDOW, (u + 2) * WINDOW)

    def padded_q(h, u):
        q_h = qT_ref[0, h * HEAD_DIM:(h + 1) * HEAD_DIM, queries(u)]
        return jnp.concatenate([q_h, zeros] if h // group == 0 else [zeros, q_h], axis=0)

    def bias(u, g):
        return bias_ref[jnp.minimum(j, 1), g] if u == 0 else bias_ref[1, g]

    heads = {g: range(g * group, (g + 1) * group) for g in range(N_KV_HEADS_B)}
    sink = {g: jnp.concatenate([sink_ref[h][:, :WINDOW] * LOG2E for h in heads[g]], axis=1)
            for g in range(N_KV_HEADS_B)}
    s = {(u, g): jnp.dot(kwin[keys(u)], jnp.concatenate([padded_q(h, u) for h in heads[g]], axis=1),
                         preferred_element_type=F32) for u, g in units}
    p = {(u, g): jnp.exp2(s[u, g] + bias(u, g) - sink[g]).astype(BF16) for u, g in units}
    acc = {(u, g): jnp.dot(vTwin[g * V_ROWS:(g + 1) * V_ROWS, keys(u)], p[u, g],
                           preferred_element_type=F32) for u, g in units}
    worst = jnp.zeros((1, group * WINDOW), F32)
    for u, g in units:
        denom = acc[u, g][HEAD_DIM:HEAD_DIM + 1] + 1.0
        worst = jnp.maximum(worst, denom)
        out = acc[u, g][:HEAD_DIM] * (1.0 / denom)
        for i, h in enumerate(heads[g]):
            yT_ref[h * HEAD_DIM:(h + 1) * HEAD_DIM, queries(u)] = out[:, i * WINDOW:(i + 1) * WINDOW]
    o_ref[0] = yT_ref[...].T.astype(o_ref.dtype)
    in_range = jnp.max(worst) < OVERFLOW_GUARD

    @pl.when(jnp.logical_not(in_range))
    def _():
        for u, g in units:
            for i, h in enumerate(heads[g]):
                s_h = (jnp.dot(kwin[keys(u)], padded_q(h, u), preferred_element_type=F32)
                       + bias(u, g)[:, i * WINDOW:(i + 1) * WINDOW])
                sink_h = sink_ref[h][:, :WINDOW] * LOG2E
                m = jnp.maximum(jnp.max(s_h, axis=0, keepdims=True), sink_h)
                acc_h = jnp.dot(vTwin[g * V_ROWS:(g + 1) * V_ROWS, keys(u)],
                                jnp.exp2(s_h - m).astype(BF16), preferred_element_type=F32)
                denom = acc_h[HEAD_DIM:HEAD_DIM + 1] + jnp.exp2(sink_h - m)
                yT_ref[h * HEAD_DIM:(h + 1) * HEAD_DIM, queries(u)] = acc_h[:HEAD_DIM] * (1.0 / denom)
        o_ref[0] = yT_ref[...].T.astype(o_ref.dtype)


def _swa_bias_table():
    tq = WINDOW
    slopes = _alibi_slopes(N_HEADS).astype(np.float64) * LOG2E
    kw = np.arange(WINDOW + tq)[:, None]
    q = np.arange(tq)[None, :]
    dist = q + WINDOW - kw
    ok = (dist >= 0) & (dist < WINDOW)
    first_tile = ok & (kw >= WINDOW)
    table = np.stack([np.where(valid[None], -slopes[:, None, None] * dist[None], MASKED)
                      for valid in (first_tile, ok)])
    table = table.reshape(2, N_KV_HEADS_B, N_HEADS // N_KV_HEADS_B, WINDOW + tq, tq)
    table = np.moveaxis(table, 2, 3).reshape(2, N_KV_HEADS_B, WINDOW + tq, -1)
    return jnp.asarray(table, F32)


def _swa_attention(qT, k, vT, sinks):
    b, _, s = qT.shape
    tq = min(SWA_TILE, s)
    assert tq % WINDOW == 0
    r = tq // WINDOW
    bias = _swa_bias_table()
    sink_rows = jnp.broadcast_to(sinks.astype(F32)[:, None, None], (N_HEADS, 1, tq))
    prev = lambda j: jnp.maximum(j * r - 1, 0)
    return pl.pallas_call(
        _swa_kernel,
        grid=(b, s // tq),
        in_specs=[
            pl.BlockSpec((1, ATTN_WIDTH, tq), lambda bb, j: (bb, 0, j)),
            pl.BlockSpec((1, WINDOW, KV_WIDTH_B), lambda bb, j: (bb, prev(j), 0)),
            pl.BlockSpec((1, tq, KV_WIDTH_B), lambda bb, j: (bb, j, 0)),
            pl.BlockSpec((1, N_KV_HEADS_B * V_ROWS, WINDOW), lambda bb, j: (bb, 0, prev(j))),
            pl.BlockSpec((1, N_KV_HEADS_B * V_ROWS, tq), lambda bb, j: (bb, 0, j)),
            pl.BlockSpec(bias.shape, lambda bb, j: (0, 0, 0, 0)),
            pl.BlockSpec(sink_rows.shape, lambda bb, j: (0, 0, 0)),
        ],
        out_specs=pl.BlockSpec((1, tq, ATTN_WIDTH), lambda bb, j: (bb, j, 0)),
        out_shape=jax.ShapeDtypeStruct((b, s, ATTN_WIDTH), BF16),
        scratch_shapes=[pltpu.VMEM((ATTN_WIDTH, tq), F32)],
        compiler_params=_params("parallel", "arbitrary"),
        name="swa_attention",
    )(qT, k, k, vT, vT, bias, sink_rows)


def kernel(x, mem, norm_g, w_in_a, w_in_b, sinks_b, w_mem_kv, w_out, mem_norm_g, final_norm_g):
    depth = norm_g.shape[0]
    b, s, _ = x.shape
    assert s % ROW_TILE == 0 or s < ROW_TILE
    mem_k, mem_vT = _mem_kv(mem, mem_norm_g, w_mem_kv)
    mixers = ["moba" if i % 2 == 0 else "swa" for i in range(depth)]
    w_in = [w_in_a[i // 2] if i % 2 == 0 else w_in_b[i // 2] for i in range(depth)]
    proj = _first_projection(x, norm_g[0], w_in[0], mixers[0])
    for i in range(depth):
        w_out_i = w_out[i]
        if mixers[i] == "moba":
            qT, k, kext, vT, kstat, qmT, z = proj
            y_self = _moba_attention(qT, k, kext, vT, kstat)
            w_out_i = jnp.concatenate([_permute_heads(w_out_i[:ATTN_WIDTH], axis=0),
                                       w_out_i[ATTN_WIDTH:]], axis=0)
        else:
            qT, k, vT, qmT, z = proj
            y_self = _swa_attention(qT, k, vT, sinks_b[i // 2])
        if i + 1 < depth:
            x, *proj = _epilogue(y_self, qmT, z, x, mem_k, mem_vT, i, w_out_i, norm_g[i + 1],
                                 w_in[i + 1], mixers[i + 1])
        else:
            (x,) = _epilogue(y_self, qmT, z, x, mem_k, mem_vT, i, w_out_i, final_norm_g)
    return x
```

```python
import functools
import math

import jax
import jax.numpy as jnp
import numpy as np
from jax import lax
from jax.experimental import pallas as pl
from jax.experimental.pallas import tpu as pltpu

HEAD_DIM = 64
N_HEADS = 12
N_KV_HEADS_B = 2
N_MEM_HEADS = 4
ATTN_WIDTH = N_HEADS * HEAD_DIM
MEM_WIDTH = N_MEM_HEADS * HEAD_DIM
BRANCH_WIDTH = ATTN_WIDTH + MEM_WIDTH
KV_WIDTH_B = N_KV_HEADS_B * HEAD_DIM
MOBA_BLOCK = 256
MOBA_TOPK = 3
WINDOW = 128
RMS_EPS = 1e-6

LANES = 128
HEAD_PAIR = LANES // HEAD_DIM
N_PAIRS = N_HEADS // HEAD_PAIR
SEL_COLS = 64
ALIBI_COLS = 4
ONES_ROWS = 16
V_ROWS = HEAD_DIM + ONES_ROWS
LOG2E = math.log2(math.e)
AUG = 2 * LANES
MASKED = -(2.0 ** 30)
OVERFLOW_GUARD = 2.0 ** 100
EXP2_IS_ZERO_BELOW = -135.0
BOUND_SLACK = 1.0
BF16_ROUND_UP = 1.0 + 2.0 ** -7
TRIP_SIZES = (16, 4, 2, 1)
MASKED_TRIP = 4
Q_BLOCKS_PER_STEP = 2
NEAR_BLOCKS = 2
ROW_TILE = 512
FINAL_ROW_TILE = 1024
SWA_TILE = 256
VMEM_LIMIT = 56 * 1024 * 1024

F32 = jnp.float32
BF16 = jnp.bfloat16
_NT = (((1,), (1,)), ((), ()))


def _alibi_slopes(n_heads):
    def pow2_slopes(n):
        start = 2.0 ** (-8.0 / n)
        return [start ** (i + 1) for i in range(n)]
    if math.log2(n_heads).is_integer():
        vals = pow2_slopes(n_heads)
    else:
        c = 2 ** math.floor(math.log2(n_heads))
        vals = pow2_slopes(c) + pow2_slopes(2 * c)[0::2][: n_heads - c]
    return np.array(vals, dtype=np.float32)


def _bf16_pieces(v):
    rest = np.asarray(v, np.float64)
    pieces = []
    for _ in range(ALIBI_COLS):
        p = rest.astype(BF16).astype(np.float64)
        pieces.append(p.astype(np.float32))
        rest = rest - p
    assert np.all(np.abs(rest) <= np.abs(v) * 2.0 ** -30)
    return pieces


def _rms_bf16(x, g):
    ms = jnp.mean(x * x, axis=-1, keepdims=True)
    return (x * lax.rsqrt(ms + RMS_EPS) * g).astype(BF16)


def _params(*sem):
    return pltpu.CompilerParams(dimension_semantics=sem, vmem_limit_bytes=VMEM_LIMIT)


def _mem_kv_kernel(mem_ref, g_ref, wk_ref, wvT_ref, k_ref, vT_ref):
    h = _rms_bf16(mem_ref[0], g_ref[...])
    k = jnp.dot(h, wk_ref[0], preferred_element_type=F32)
    vT = lax.dot_general(wvT_ref[0], h, _NT, preferred_element_type=F32)
    lane = lax.broadcasted_iota(jnp.int32, k.shape, 1)
    for hh in range(N_MEM_HEADS):
        mine = (lane >= hh * HEAD_DIM) & (lane < (hh + 1) * HEAD_DIM)
        k_ref[0, 0, hh] = jnp.where(mine, k, 0.0).astype(BF16)
        vT_ref[0, 0, hh * V_ROWS:hh * V_ROWS + HEAD_DIM] = (
            vT[hh * HEAD_DIM:(hh + 1) * HEAD_DIM].astype(BF16))
        vT_ref[0, 0, hh * V_ROWS + HEAD_DIM:(hh + 1) * V_ROWS] = jnp.ones(
            (ONES_ROWS, vT.shape[1]), BF16)


def _mem_kv(mem, mem_norm_g, w_mem_kv):
    b, m, d = mem.shape
    depth = w_mem_kv.shape[0]
    wk = w_mem_kv[:, :, :MEM_WIDTH].astype(BF16)
    wvT = jnp.swapaxes(w_mem_kv[:, :, MEM_WIDTH:], 1, 2).astype(BF16)
    return pl.pallas_call(
        _mem_kv_kernel,
        grid=(depth, b),
        in_specs=[
            pl.BlockSpec((1, m, d), lambda i, bb: (bb, 0, 0)),
            pl.BlockSpec((1, d), lambda i, bb: (0, 0)),
            pl.BlockSpec((1, d, MEM_WIDTH), lambda i, bb: (i, 0, 0)),
            pl.BlockSpec((1, MEM_WIDTH, d), lambda i, bb: (i, 0, 0)),
        ],
        out_specs=[
            pl.BlockSpec((1, 1, N_MEM_HEADS, m, MEM_WIDTH), lambda i, bb: (i, bb, 0, 0, 0)),
            pl.BlockSpec((1, 1, N_MEM_HEADS * V_ROWS, m), lambda i, bb: (i, bb, 0, 0)),
        ],
        out_shape=[
            jax.ShapeDtypeStruct((depth, b, N_MEM_HEADS, m, MEM_WIDTH), BF16),
            jax.ShapeDtypeStruct((depth, b, N_MEM_HEADS * V_ROWS, m), BF16),
        ],
        compiler_params=_params("arbitrary", "arbitrary"),
        name="mem_kv",
    )(mem, mem_norm_g.reshape(1, d), wk, wvT)


def _moba_projection(h, wT_ref, w_ref, qT_ref, k_ref, kext_ref, vT_ref, kstat_ref, qmT_ref, z_ref):
    tm = h.shape[0]
    nb = tm // MOBA_BLOCK
    outT = lax.dot_general(wT_ref[...], h, _NT, preferred_element_type=F32)
    out = jnp.dot(h, w_ref[...], preferred_element_type=F32)
    ones = jnp.ones((ONES_ROWS, MOBA_BLOCK), BF16)
    for p in range(N_PAIRS):
        for bb in range(nb):
            cols = slice(bb * MOBA_BLOCK, (bb + 1) * MOBA_BLOCK)
            qT_ref[0, p, bb] = outT[p * LANES:(p + 1) * LANES, cols].astype(BF16)
            for hh in range(HEAD_PAIR):
                v0 = ATTN_WIDTH + (p * HEAD_PAIR + hh) * HEAD_DIM
                vT_ref[0, p, bb, hh * V_ROWS:hh * V_ROWS + HEAD_DIM] = (
                    outT[v0:v0 + HEAD_DIM, cols].astype(BF16))
                vT_ref[0, p, bb, hh * V_ROWS + HEAD_DIM:(hh + 1) * V_ROWS] = ones
    qmT_ref[0] = outT[2 * ATTN_WIDTH:, :].astype(BF16)
    k = out[:, :ATTN_WIDTH]
    z_ref[0] = out[:, ATTN_WIDTH:].astype(BF16)
    row = lax.broadcasted_iota(jnp.int32, (tm, LANES), 0)
    lane = lax.broadcasted_iota(jnp.int32, (tm, LANES), 1)
    blk = pl.program_id(1) * nb + row // MOBA_BLOCK
    s_rel = (row % MOBA_BLOCK).astype(F32)
    ext = jnp.where(lane < SEL_COLS, (lane == blk).astype(F32),
                    jnp.where(lane < SEL_COLS + ALIBI_COLS, s_rel, 0.0)).astype(BF16)
    k_bf16 = k.astype(BF16)
    k_ref[0] = k_bf16
    kext_ref[0] = ext
    k_abs = jnp.abs(k_bf16.astype(F32))
    for bb in range(nb):
        rows = slice(bb * MOBA_BLOCK, (bb + 1) * MOBA_BLOCK)
        kstat_ref[0, 0, 0, bb:bb + 1, :] = jnp.mean(k[rows], axis=0, keepdims=True)
        kstat_ref[0, 0, 1, bb:bb + 1, :] = jnp.max(k_abs[rows], axis=0, keepdims=True)


def _moba_projection_specs(b, s, tm):
    nblk = s // MOBA_BLOCK
    nb = tm // MOBA_BLOCK
    specs = [
        pl.BlockSpec((1, N_PAIRS, nb, LANES, MOBA_BLOCK), lambda bb, i: (bb, 0, i, 0, 0)),
        pl.BlockSpec((1, tm, ATTN_WIDTH), lambda bb, i: (bb, i, 0)),
        pl.BlockSpec((1, tm, LANES), lambda bb, i: (bb, i, 0)),
        pl.BlockSpec((1, N_PAIRS, nb, HEAD_PAIR * V_ROWS, MOBA_BLOCK),
                     lambda bb, i: (bb, 0, i, 0, 0)),
        pl.BlockSpec((1, 1, 2, nb, ATTN_WIDTH), lambda bb, i: (bb, i, 0, 0, 0)),
        pl.BlockSpec((1, MEM_WIDTH, tm), lambda bb, i: (bb, 0, i)),
        pl.BlockSpec((1, tm, BRANCH_WIDTH), lambda bb, i: (bb, i, 0)),
    ]
    shapes = [
        jax.ShapeDtypeStruct((b, N_PAIRS, nblk, LANES, MOBA_BLOCK), BF16),
        jax.ShapeDtypeStruct((b, s, ATTN_WIDTH), BF16),
        jax.ShapeDtypeStruct((b, s, LANES), BF16),
        jax.ShapeDtypeStruct((b, N_PAIRS, nblk, HEAD_PAIR * V_ROWS, MOBA_BLOCK), BF16),
        jax.ShapeDtypeStruct((b, s // tm, 2, nb, ATTN_WIDTH), F32),
        jax.ShapeDtypeStruct((b, MEM_WIDTH, s), BF16),
        jax.ShapeDtypeStruct((b, s, BRANCH_WIDTH), BF16),
    ]
    return specs, shapes


def _swa_projection(h, wT_ref, w_ref, qT_ref, k_ref, vT_ref, qmT_ref, z_ref):
    outT = lax.dot_general(wT_ref[...], h, _NT, preferred_element_type=F32)
    out = jnp.dot(h, w_ref[...], preferred_element_type=F32)
    tm = h.shape[0]
    qT_ref[0] = outT[:ATTN_WIDTH].astype(BF16)
    for g in range(N_KV_HEADS_B):
        v0 = ATTN_WIDTH + g * HEAD_DIM
        vT_ref[0, g * V_ROWS:g * V_ROWS + HEAD_DIM] = outT[v0:v0 + HEAD_DIM].astype(BF16)
        vT_ref[0, g * V_ROWS + HEAD_DIM:(g + 1) * V_ROWS] = jnp.ones((ONES_ROWS, tm), BF16)
    qmT_ref[0] = outT[ATTN_WIDTH + KV_WIDTH_B:].astype(BF16)
    k_ref[0] = out[:, :KV_WIDTH_B].astype(BF16)
    z_ref[0] = out[:, KV_WIDTH_B:].astype(BF16)


def _swa_projection_specs(b, s, tm):
    specs = [
        pl.BlockSpec((1, ATTN_WIDTH, tm), lambda bb, i: (bb, 0, i)),
        pl.BlockSpec((1, tm, KV_WIDTH_B), lambda bb, i: (bb, i, 0)),
        pl.BlockSpec((1, N_KV_HEADS_B * V_ROWS, tm), lambda bb, i: (bb, 0, i)),
        pl.BlockSpec((1, MEM_WIDTH, tm), lambda bb, i: (bb, 0, i)),
        pl.BlockSpec((1, tm, BRANCH_WIDTH), lambda bb, i: (bb, i, 0)),
    ]
    shapes = [
        jax.ShapeDtypeStruct((b, ATTN_WIDTH, s), BF16),
        jax.ShapeDtypeStruct((b, s, KV_WIDTH_B), BF16),
        jax.ShapeDtypeStruct((b, N_KV_HEADS_B * V_ROWS, s), BF16),
        jax.ShapeDtypeStruct((b, MEM_WIDTH, s), BF16),
        jax.ShapeDtypeStruct((b, s, BRANCH_WIDTH), BF16),
    ]
    return specs, shapes


_PROJECTIONS = {
    "moba": (_moba_projection, _moba_projection_specs),
    "swa": (_swa_projection, _swa_projection_specs),
}


def _moba_head_order():
    return np.argsort(-_alibi_slopes(N_HEADS), kind="stable")


def _permute_heads(w, axis):
    shape = w.shape
    grouped = w.reshape(shape[:axis] + (N_HEADS, HEAD_DIM) + shape[axis + 1:])
    return jnp.take(grouped, _moba_head_order(), axis=axis).reshape(shape)


def _split_in_weights(w_in, mixer):
    scale = HEAD_DIM ** -0.5 * LOG2E
    kvw = ATTN_WIDTH if mixer == "moba" else KV_WIDTH_B
    o = np.cumsum([0, ATTN_WIDTH, kvw, kvw, MEM_WIDTH, BRANCH_WIDTH])
    q, k, v, qm, z = (w_in[:, o[i]:o[i + 1]] for i in range(5))
    if mixer == "moba":
        q, k, v = (_permute_heads(t, axis=1) for t in (q, k, v))
        z = jnp.concatenate([_permute_heads(z[:, :ATTN_WIDTH], axis=1), z[:, ATTN_WIDTH:]], axis=1)
    wT = jnp.concatenate([q * scale, v, qm * scale], axis=1).T.astype(BF16)
    w = jnp.concatenate([k, z], axis=1).astype(BF16)
    return wT, w


def _first_projection_kernel(x_ref, g_ref, wT_ref, w_ref, *out_refs, mixer):
    h = _rms_bf16(x_ref[0], g_ref[...])
    _PROJECTIONS[mixer][0](h, wT_ref, w_ref, *out_refs)


def _first_projection(x, g, w_in, mixer):
    b, s, d = x.shape
    tm = min(ROW_TILE, s)
    wT, w = _split_in_weights(w_in, mixer)
    specs, shapes = _PROJECTIONS[mixer][1](b, s, tm)
    return pl.pallas_call(
        functools.partial(_first_projection_kernel, mixer=mixer),
        grid=(b, s // tm),
        in_specs=[
            pl.BlockSpec((1, tm, d), lambda bb, i: (bb, i, 0)),
            pl.BlockSpec((1, d), lambda bb, i: (0, 0)),
            pl.BlockSpec(wT.shape, lambda bb, i: (0, 0)),
            pl.BlockSpec(w.shape, lambda bb, i: (0, 0)),
        ],
        out_specs=specs,
        out_shape=shapes,
        compiler_params=_params("parallel", "arbitrary"),
        name=f"in_proj_{mixer}",
    )(x, g.reshape(1, d), wT, w)


def _memory_attention_T(qmT, mk_ref, mvT_ref):
    sT = [jnp.dot(mk_ref[0, hh], qmT, preferred_element_type=F32)
          for hh in range(N_MEM_HEADS)]
    outs = []
    for hh in range(N_MEM_HEADS):
        p = jnp.exp2(sT[hh] - jnp.max(sT[hh], axis=0, keepdims=True))
        acc = jnp.dot(mvT_ref[0, hh * V_ROWS:(hh + 1) * V_ROWS, :], p.astype(BF16),
                      preferred_element_type=F32)
        outs.append(acc[:HEAD_DIM] * (1.0 / acc[HEAD_DIM:HEAD_DIM + 1]))
    return jnp.concatenate(outs, axis=0)


def _epilogue_kernel(y_ref, qmT_ref, z_ref, x_ref, mk_ref, mvT_ref, wout_ref, g_ref, *rest, mixer):
    ymem = _memory_attention_T(qmT_ref[0], mk_ref, mvT_ref).T
    y = jnp.concatenate([y_ref[0].astype(F32), ymem], axis=-1)
    z = z_ref[0].astype(F32)
    gated = (y * (z / (1.0 + jnp.exp(-z)))).astype(BF16)
    xn = x_ref[0] + jnp.dot(gated, wout_ref[...], preferred_element_type=F32)
    if mixer is None:
        (o_ref,) = rest
        ms = jnp.mean(xn * xn, axis=-1, keepdims=True)
        o_ref[0] = xn * lax.rsqrt(ms + RMS_EPS) * g_ref[...]
    else:
        wT_ref, w_ref, xo_ref = rest[:3]
        xo_ref[0] = xn
        _PROJECTIONS[mixer][0](_rms_bf16(xn, g_ref[...]), wT_ref, w_ref, *rest[3:])


def _epilogue(y_self, qmT, z, x, mem_k, mem_vT, layer, w_out, g, w_in_next=None, mixer=None):
    b, s, d = x.shape
    tm = min(FINAL_ROW_TILE if mixer is None and s % FINAL_ROW_TILE == 0 else ROW_TILE, s)
    m = mem_k.shape[3]
    in_specs = [
        pl.BlockSpec((1, tm, ATTN_WIDTH), lambda bb, i: (bb, i, 0)),
        pl.BlockSpec((1, MEM_WIDTH, tm), lambda bb, i: (bb, 0, i)),
        pl.BlockSpec((1, tm, BRANCH_WIDTH), lambda bb, i: (bb, i, 0)),
        pl.BlockSpec((1, tm, d), lambda bb, i: (bb, i, 0)),
        pl.BlockSpec((1, N_MEM_HEADS, m, MEM_WIDTH), lambda bb, i: (bb, 0, 0, 0)),
        pl.BlockSpec((1, N_MEM_HEADS * V_ROWS, m), lambda bb, i: (bb, 0, 0)),
        pl.BlockSpec((BRANCH_WIDTH, d), lambda bb, i: (0, 0)),
        pl.BlockSpec((1, d), lambda bb, i: (0, 0)),
    ]
    args = [y_self, qmT, z, x, mem_k[layer], mem_vT[layer], w_out.astype(BF16), g.reshape(1, d)]
    x_spec = pl.BlockSpec((1, tm, d), lambda bb, i: (bb, i, 0))
    x_shape = jax.ShapeDtypeStruct((b, s, d), F32)
    if mixer is None:
        out_specs, out_shape = [x_spec], [x_shape]
    else:
        wT, w = _split_in_weights(w_in_next, mixer)
        in_specs += [pl.BlockSpec(wT.shape, lambda bb, i: (0, 0)),
                     pl.BlockSpec(w.shape, lambda bb, i: (0, 0))]
        args += [wT, w]
        specs, shapes = _PROJECTIONS[mixer][1](b, s, tm)
        out_specs, out_shape = [x_spec] + specs, [x_shape] + shapes
    return pl.pallas_call(
        functools.partial(_epilogue_kernel, mixer=mixer),
        grid=(b, s // tm),
        in_specs=in_specs,
        out_specs=out_specs,
        out_shape=out_shape,
        compiler_params=_params("parallel", "arbitrary"),
        name=f"epilogue_{mixer}",
    )(*args)


def _moba_kernel(qT_ref, k_ref, kext_ref, vT_ref, kmean_ref, kabs_ref, srows_ref, slope_ref, o_ref,
                 qaug_ref):
    nblk = kmean_ref.shape[1]
    tq = qT_ref.shape[-1]
    kmean = kmean_ref[0]
    drow = lax.broadcasted_iota(jnp.int32, (LANES, tq), 0)
    klane = lax.broadcasted_iota(jnp.int32, kmean.shape, 1)
    blk = lax.broadcasted_iota(jnp.int32, (nblk, tq), 0)
    blk_f32 = blk.astype(F32)
    blk_col = lax.broadcasted_iota(jnp.int32, (nblk, 1), 0)
    krow = lax.broadcasted_iota(jnp.int32, (MOBA_BLOCK, tq), 0)
    qlane = lax.broadcasted_iota(jnp.int32, (MOBA_BLOCK, tq), 1)
    causal = krow <= qlane
    head_lanes = [(klane >= hh * HEAD_DIM) & (klane < (hh + 1) * HEAD_DIM) for hh in range(HEAD_PAIR)]
    head_rows = [(drow >= hh * HEAD_DIM) & (drow < (hh + 1) * HEAD_DIM) for hh in range(HEAD_PAIR)]
    no_choice = jnp.zeros((SEL_COLS, tq), BF16)

    def k_aug(n):
        return jnp.concatenate([k_ref[0, n], kext_ref[0, n]], axis=1)

    def weighted_values(n, hh, p):
        return jnp.dot(vT_ref[0, 0, n, hh * V_ROWS:(hh + 1) * V_ROWS, :], p.astype(BF16),
                       preferred_element_type=F32)

    chains = [(t, hh) for t in range(Q_BLOCKS_PER_STEP) for hh in range(HEAD_PAIR)]
    block_j = [pl.program_id(2) * Q_BLOCKS_PER_STEP + t for t in range(Q_BLOCKS_PER_STEP)]
    last_j = block_j[-1]
    qT = [qT_ref[0, 0, t] for t in range(Q_BLOCKS_PER_STEP)]
    q_abs = [jnp.abs(q) for q in qT]

    q_head = {(t, hh): jnp.where(head_rows[hh], qT[t], jnp.zeros_like(qT[t])) for t, hh in chains}
    kmean_h = [jnp.where(head_lanes[hh], kmean, 0.0).astype(BF16) for hh in range(HEAD_PAIR)]
    gates = {(t, hh): jnp.dot(kmean_h[hh], qT[t], preferred_element_type=F32)
             for t, hh in chains}
    own_sT = {(t, hh): jnp.dot(
        k_aug(block_j[t]), jnp.concatenate([q_head[t, hh], no_choice, srows_ref[0, hh]], axis=0),
        preferred_element_type=F32) for t, hh in chains}
    kabs_h = [jnp.where(head_lanes[hh], kabs_ref[0] * BF16_ROUND_UP, 0.0).astype(BF16)
              for hh in range(HEAD_PAIR)]
    qk_bound = {(t, hh): jnp.dot(kabs_h[hh], q_abs[t], preferred_element_type=F32)
                for t, hh in chains}

    own_max, own_acc = {}, {}
    for t, hh in chains:
        sT = jnp.where(causal, own_sT[t, hh], MASKED)
        own_max[t, hh] = jnp.max(sT, axis=0, keepdims=True)
        own_acc[t, hh] = weighted_values(block_j[t], hh, jnp.exp2(sT - own_max[t, hh]))

    first_needed = last_j
    for t in range(Q_BLOCKS_PER_STEP):
        j = block_j[t]
        needed = blk_col < 0
        for hh in range(HEAD_PAIR):
            qk_over_shift = jnp.max(qk_bound[t, hh] - own_max[t, hh], axis=1, keepdims=True)
            alibi = slope_ref[0, hh][:, :1] * ((MOBA_BLOCK - 1) - MOBA_BLOCK * (j - blk_col)).astype(F32)
            needed = needed | (qk_over_shift + alibi + BOUND_SLACK >= EXP2_IS_ZERO_BELOW)
        first_needed = jnp.minimum(first_needed, jnp.min(
            jnp.where(needed & (blk_col < j - NEAR_BLOCKS), blk_col, last_j).astype(F32)
        ).astype(jnp.int32))

    gate = {c: jnp.where(blk < block_j[c[0]], gates[c], -jnp.inf) for c in chains}
    bias = {c: jnp.full((nblk, tq), MASKED, F32) for c in chains}
    for k in range(MOBA_TOPK):
        for c in chains:
            best = jnp.max(gate[c], axis=0, keepdims=True)
            first = jnp.min(jnp.where(gate[c] == best, blk_f32, float(nblk)), axis=0, keepdims=True)
            pick = blk_f32 == jnp.where(best > -jnp.inf, first, float(nblk))
            bias[c] = jnp.where(pick, 0.0, bias[c])
            if k + 1 < MOBA_TOPK:
                gate[c] = jnp.where(pick, -jnp.inf, gate[c])
    for t, hh in chains:
        rows = bias[t, hh].astype(BF16)
        if nblk < SEL_COLS:
            rows = jnp.concatenate([rows, jnp.zeros((SEL_COLS - nblk, tq), BF16)], axis=0)
        qaug_ref[t, hh] = jnp.concatenate([q_head[t, hh], rows, srows_ref[0, hh]], axis=0)

    def scores(t, n, hh):
        return jnp.dot(k_aug(n), qaug_ref[t, hh], preferred_element_type=F32)

    def block_offset(t, n, hh):
        return slope_ref[0, hh] * ((n - block_j[t]) * MOBA_BLOCK).astype(F32)

    def finish(t, accs):
        oT = jnp.concatenate(
            [acc[:HEAD_DIM] * (1.0 / acc[HEAD_DIM:HEAD_DIM + 1]) for acc in accs], axis=0)
        o_ref[0, t * tq:(t + 1) * tq, :] = oT.T.astype(o_ref.dtype)

    def fixed_shift_blocks(first, count, accs):
        slots = [first + u for u in range(count)]
        blocks = [jnp.minimum(n, last_j) for n in slots]
        sT = [{c: scores(c[0], n, c[1]) for c in chains} for n in blocks]
        accs = list(accs)
        for u, n in enumerate(blocks):
            for i, (t, hh) in enumerate(chains):
                shift = jnp.where(slots[u] < block_j[t] - NEAR_BLOCKS,
                                  own_max[t, hh] - block_offset(t, n, hh), jnp.inf)
                accs[i] += weighted_values(n, hh, jnp.exp2(sT[u][t, hh] - shift))
        return accs

    near = [(t, hh, back) for back in range(1, NEAR_BLOCKS + 1) for t, hh in chains]
    near_sT = {c: scores(c[0], jnp.maximum(block_j[c[0]] - c[2], 0), c[1]) for c in near}
    fast = {c: own_acc[c] for c in chains}
    for t, hh, back in near:
        n = jnp.maximum(block_j[t] - back, 0)
        shift = jnp.where(block_j[t] >= back, own_max[t, hh] - block_offset(t, n, hh), jnp.inf)
        fast[t, hh] = fast[t, hh] + weighted_values(n, hh, jnp.exp2(near_sT[t, hh, back] - shift))
    fast = [fast[c] for c in chains]

    start = first_needed
    loop_end = last_j - NEAR_BLOCKS
    for unroll in TRIP_SIZES:
        left = loop_end - start
        trips = jnp.maximum((left + 1) // unroll if unroll == MASKED_TRIP else left // unroll, 0)
        fast = lax.fori_loop(
            0, trips,
            lambda g, a, start=start, unroll=unroll: fixed_shift_blocks(start + g * unroll, unroll, a),
            fast)
        start = start + trips * unroll

    in_range = []
    for t in range(Q_BLOCKS_PER_STEP):
        mine = fast[t * HEAD_PAIR:(t + 1) * HEAD_PAIR]
        finish(t, mine)
        denominators = functools.reduce(jnp.maximum, [a[HEAD_DIM:HEAD_DIM + 1] for a in mine])
        in_range.append(jnp.max(denominators) < OVERFLOW_GUARD)

    for t in range(Q_BLOCKS_PER_STEP):
        @pl.when(jnp.logical_not(in_range[t]))
        def _(t=t):
            def running_max_block(n, carry):
                out = []
                for hh in range(HEAD_PAIR):
                    m, acc = carry[2 * hh:2 * hh + 2]
                    sT = scores(t, n, hh)
                    c = block_offset(t, n, hh)
                    m_new = jnp.maximum(m, jnp.max(sT, axis=0, keepdims=True) + c)
                    acc = (jnp.exp2(m - m_new) * acc
                           + weighted_values(n, hh, jnp.exp2(sT - (m_new - c))))
                    out += [m_new, acc]
                return out

            slow = lax.fori_loop(
                0, block_j[t], running_max_block,
                [x for hh in range(HEAD_PAIR) for x in (own_max[t, hh], own_acc[t, hh])])
            finish(t, slow[1::2])


def _moba_tables(tq):
    slopes = _alibi_slopes(N_HEADS)[_moba_head_order()].astype(np.float64) * LOG2E
    pieces = _bf16_pieces(slopes)
    rows = np.zeros((N_HEADS, AUG - LANES - SEL_COLS, tq), np.float32)
    for c, piece in enumerate(pieces):
        rows[:, c, :] = piece[:, None]
    srows = jnp.asarray(rows.reshape(N_PAIRS, HEAD_PAIR, AUG - LANES - SEL_COLS, tq), BF16)
    svec = jnp.asarray(np.broadcast_to(slopes[:, None, None], (N_HEADS, 1, tq))
                       .reshape(N_PAIRS, HEAD_PAIR, 1, tq), F32)
    return srows, svec


def _moba_attention(qT, k, kext, vT, kstat):
    b, _, nblk, _, tq = qT.shape
    assert nblk <= SEL_COLS and tq == MOBA_BLOCK and nblk % Q_BLOCKS_PER_STEP == 0
    s = nblk * MOBA_BLOCK
    k = k.reshape(b, nblk, MOBA_BLOCK, ATTN_WIDTH)
    kext = kext.reshape(b, nblk, MOBA_BLOCK, LANES)
    kmean = kstat[:, :, 0].reshape(b, nblk, ATTN_WIDTH)
    kabs = kstat[:, :, 1].reshape(b, nblk, ATTN_WIDTH)
    srows, svec = _moba_tables(tq)
    return pl.pallas_call(
        _moba_kernel,
        grid=(b, N_PAIRS, nblk // Q_BLOCKS_PER_STEP),
        in_specs=[
            pl.BlockSpec((1, 1, Q_BLOCKS_PER_STEP, LANES, tq), lambda bb, p, j: (bb, p, j, 0, 0)),
            pl.BlockSpec((1, nblk, MOBA_BLOCK, LANES), lambda bb, p, j: (bb, 0, 0, p)),
            pl.BlockSpec((1, nblk, MOBA_BLOCK, LANES), lambda bb, p, j: (bb, 0, 0, 0)),
            pl.BlockSpec((1, 1, nblk, HEAD_PAIR * V_ROWS, MOBA_BLOCK),
                         lambda bb, p, j: (bb, p, 0, 0, 0)),
            pl.BlockSpec((1, nblk, LANES), lambda bb, p, j: (bb, 0, p)),
            pl.BlockSpec((1, nblk, LANES), lambda bb, p, j: (bb, 0, p)),
            pl.BlockSpec((1, HEAD_PAIR, AUG - LANES - SEL_COLS, tq), lambda bb, p, j: (p, 0, 0, 0)),
            pl.BlockSpec((1, HEAD_PAIR, 1, tq), lambda bb, p, j: (p, 0, 0, 0)),
        ],
        out_specs=pl.BlockSpec((1, Q_BLOCKS_PER_STEP * tq, LANES), lambda bb, p, j: (bb, j, p)),
        out_shape=jax.ShapeDtypeStruct((b, s, ATTN_WIDTH), BF16),
        scratch_shapes=[pltpu.VMEM((Q_BLOCKS_PER_STEP, HEAD_PAIR, AUG, tq), BF16)],
        compiler_params=_params("parallel", "parallel", "arbitrary"),
        name="moba_attention",
    )(qT, k, kext, vT, kmean, kabs, srows, svec)


def _swa_kernel(qT_ref, kp_ref, kc_ref, vTp_ref, vTc_ref, bias_ref, sink_ref, o_ref, yT_ref):
    j = pl.program_id(1)
    tq = qT_ref.shape[-1]
    kwin = jnp.concatenate([kp_ref[0], kc_ref[0]], axis=0)
    vTwin = jnp.concatenate([vTp_ref[0], vTc_ref[0]], axis=1)
    zeros = jnp.zeros((HEAD_DIM, WINDOW), BF16)
    group = N_HEADS // N_KV_HEADS_B
    units = [(u, g) for u in range(tq // WINDOW) for g in range(N_KV_HEADS_B)]

    def queries(u):
        return slice(u * WINDOW, (u + 1) * WINDOW)

    def keys(u):
        return slice(u * WINDOW, (u + 2) * WINDOW)

    def padded_q(h, u):
        q_h = qT_ref[0, h * HEAD_DIM:(h + 1) * HEAD_DIM, queries(u)]
        return jnp.concatenate([q_h, zeros] if h // group == 0 else [zeros, q_h], axis=0)

    def bias(u, g):
        return bias_ref[jnp.minimum(j, 1), g] if u == 0 else bias_ref[1, g]

    heads = {g: range(g * group, (g + 1) * group) for g in range(N_KV_HEADS_B)}
    sink = {g: jnp.concatenate([sink_ref[h][:, :WINDOW] * LOG2E for h in heads[g]], axis=1)
            for g in range(N_KV_HEADS_B)}
    s = {(u, g): jnp.dot(kwin[keys(u)], jnp.concatenate([padded_q(h, u) for h in heads[g]], axis=1),
                         preferred_element_type=F32) for u, g in units}
    p = {(u, g): jnp.exp2(s[u, g] + bias(u, g) - sink[g]).astype(BF16) for u, g in units}
    acc = {(u, g): jnp.dot(vTwin[g * V_ROWS:(g + 1) * V_ROWS, keys(u)], p[u, g],
                           preferred_element_type=F32) for u, g in units}
    worst = jnp.zeros((1, group * WINDOW), F32)
    for u, g in units:
        denom = acc[u, g][HEAD_DIM:HEAD_DIM + 1] + 1.0
        worst = jnp.maximum(worst, denom)
        out = acc[u, g][:HEAD_DIM] * (1.0 / denom)
        for i, h in enumerate(heads[g]):
            yT_ref[h * HEAD_DIM:(h + 1) * HEAD_DIM, queries(u)] = out[:, i * WINDOW:(i + 1) * WINDOW]
    o_ref[0] = yT_ref[...].T.astype(o_ref.dtype)
    in_range = jnp.max(worst) < OVERFLOW_GUARD

    @pl.when(jnp.logical_not(in_range))
    def _():
        for u, g in units:
            for i, h in enumerate(heads[g]):
                s_h = (jnp.dot(kwin[keys(u)], padded_q(h, u), preferred_element_type=F32)
                       + bias(u, g)[:, i * WINDOW:(i + 1) * WINDOW])
                sink_h = sink_ref[h][:, :WINDOW] * LOG2E
                m = jnp.maximum(jnp.max(s_h, axis=0, keepdims=True), sink_h)
                acc_h = jnp.dot(vTwin[g * V_ROWS:(g + 1) * V_ROWS, keys(u)],
                                jnp.exp2(s_h - m).astype(BF16), preferred_element_type=F32)
                denom = acc_h[HEAD_DIM:HEAD_DIM + 1] + jnp.exp2(sink_h - m)
                yT_ref[h * HEAD_DIM:(h + 1) * HEAD_DIM, queries(u)] = acc_h[:HEAD_DIM] * (1.0 / denom)
        o_ref[0] = yT_ref[...].T.astype(o_ref.dtype)


def _swa_bias_table():
    tq = WINDOW
    slopes = _alibi_slopes(N_HEADS).astype(np.float64) * LOG2E
    kw = np.arange(WINDOW + tq)[:, None]
    q = np.arange(tq)[None, :]
    dist = q + WINDOW - kw
    ok = (dist >= 0) & (dist < WINDOW)
    first_tile = ok & (kw >= WINDOW)
    table = np.stack([np.where(valid[None], -slopes[:, None, None] * dist[None], MASKED)
                      for valid in (first_tile, ok)])
    table = table.reshape(2, N_KV_HEADS_B, N_HEADS // N_KV_HEADS_B, WINDOW + tq, tq)
    table = np.moveaxis(table, 2, 3).reshape(2, N_KV_HEADS_B, WINDOW + tq, -1)
    return jnp.asarray(table, F32)


def _swa_attention(qT, k, vT, sinks):
    b, _, s = qT.shape
    tq = min(SWA_TILE, s)
    assert tq % WINDOW == 0
    r = tq // WINDOW
    bias = _swa_bias_table()
    sink_rows = jnp.broadcast_to(sinks.astype(F32)[:, None, None], (N_HEADS, 1, tq))
    prev = lambda j: jnp.maximum(j * r - 1, 0)
    return pl.pallas_call(
        _swa_kernel,
        grid=(b, s // tq),
        in_specs=[
            pl.BlockSpec((1, ATTN_WIDTH, tq), lambda bb, j: (bb, 0, j)),
            pl.BlockSpec((1, WINDOW, KV_WIDTH_B), lambda bb, j: (bb, prev(j), 0)),
            pl.BlockSpec((1, tq, KV_WIDTH_B), lambda bb, j: (bb, j, 0)),
            pl.BlockSpec((1, N_KV_HEADS_B * V_ROWS, WINDOW), lambda bb, j: (bb, 0, prev(j))),
            pl.BlockSpec((1, N_KV_HEADS_B * V_ROWS, tq), lambda bb, j: (bb, 0, j)),
            pl.BlockSpec(bias.shape, lambda bb, j: (0, 0, 0, 0)),
            pl.BlockSpec(sink_rows.shape, lambda bb, j: (0, 0, 0)),
        ],
        out_specs=pl.BlockSpec((1, tq, ATTN_WIDTH), lambda bb, j: (bb, j, 0)),
        out_shape=jax.ShapeDtypeStruct((b, s, ATTN_WIDTH), BF16),
        scratch_shapes=[pltpu.VMEM((ATTN_WIDTH, tq), F32)],
        compiler_params=_params("parallel", "arbitrary"),
        name="swa_attention",
    )(qT, k, k, vT, vT, bias, sink_rows)


def kernel(x, mem, norm_g, w_in_a, w_in_b, sinks_b, w_mem_kv, w_out, mem_norm_g, final_norm_g):
    depth = norm_g.shape[0]
    b, s, _ = x.shape
    assert s % ROW_TILE == 0 or s < ROW_TILE
    mem_k, mem_vT = _mem_kv(mem, mem_norm_g, w_mem_kv)
    mixers = ["moba" if i % 2 == 0 else "swa" for i in range(depth)]
    w_in = [w_in_a[i // 2] if i % 2 == 0 else w_in_b[i // 2] for i in range(depth)]
    proj = _first_projection(x, norm_g[0], w_in[0], mixers[0])
    for i in range(depth):
        w_out_i = w_out[i]
        if mixers[i] == "moba":
            qT, k, kext, vT, kstat, qmT, z = proj
            y_self = _moba_attention(qT, k, kext, vT, kstat)
            w_out_i = jnp.concatenate([_permute_heads(w_out_i[:ATTN_WIDTH], axis=0),
                                       w_out_i[ATTN_WIDTH:]], axis=0)
        else:
            qT, k, vT, qmT, z = proj
            y_self = _swa_attention(qT, k, vT, sinks_b[i // 2])
        if i + 1 < depth:
            x, *proj = _epilogue(y_self, qmT, z, x, mem_k, mem_vT, i, w_out_i, norm_g[i + 1],
                                 w_in[i + 1], mixers[i + 1])
        else:
            (x,) = _epilogue(y_self, qmT, z, x, mem_k, mem_vT, i, w_out_i, final_norm_g)
    return x
```
